```python
import math
import jax, jax.numpy as jnp
from jax import lax
import numpy as np

D_MODEL = 1024
BATCH = 16
SEQ = 2048
DEPTH = 2

HEAD_DIM = 64
Q_BLOCK = 128
SCALE = HEAD_DIM ** -0.5
A_HEADS = 4
B_HEADS = 6
C_HEADS = 6
D_PAIRS = ((128, 1), (512, 4), (2048, 16))
D_GROUPS = len(D_PAIRS)
D_HEADS_PER_GROUP = 2
D_HEADS = D_GROUPS * D_HEADS_PER_GROUP
N_BRANCH = 4
FFN_HIDDEN = 2816
REL_BUCKETS = 32
REL_MAX_DIST = 128
REL_HEADS = A_HEADS + D_HEADS
RMS_EPS = 1e-6
NEG_INF = -1e30
IN_SPLITS = (
    A_HEADS * 2 * HEAD_DIM, A_HEADS * 2 * HEAD_DIM, A_HEADS * 2 * HEAD_DIM,
    B_HEADS * HEAD_DIM, B_HEADS * HEAD_DIM, B_HEADS * HEAD_DIM,
    C_HEADS * HEAD_DIM, C_HEADS * HEAD_DIM, C_HEADS * HEAD_DIM, C_HEADS,
    D_HEADS * HEAD_DIM, D_HEADS * HEAD_DIM, D_HEADS * HEAD_DIM,
    N_BRANCH * D_MODEL,
)
D_IN = sum(IN_SPLITS)
BRANCH_WIDTHS = (A_HEADS * 2 * HEAD_DIM, B_HEADS * HEAD_DIM, C_HEADS * HEAD_DIM, D_HEADS_PER_GROUP * HEAD_DIM)

kernel_name = 'hybrid_gated_parallel_mixers'


def _offsets(sizes):
    return [int(v) for v in np.cumsum(sizes)[:-1]]


def rms_norm(x, g):
    xf = x.astype(jnp.float32)
    y = xf * lax.rsqrt(jnp.mean(xf * xf, axis=-1, keepdims=True) + RMS_EPS)
    return (y * g.astype(jnp.float32)).astype(x.dtype)


def swiglu(h, w_i, w_o):
    gate, up = jnp.split(h @ w_i, 2, axis=-1)
    return (jax.nn.silu(gate) * up) @ w_o


def split_heads(t, n):
    b, s, _ = t.shape
    return t.reshape(b, s, n, -1).transpose(0, 2, 1, 3)


def merge_heads(t):
    b, h, s, d = t.shape
    return t.transpose(0, 2, 1, 3).reshape(b, s, h * d)


def rel_bucket(n):
    max_exact = REL_BUCKETS // 2
    nf = jnp.maximum(n, 1).astype(jnp.float32)
    large = max_exact + (jnp.log(nf / max_exact) / math.log(REL_MAX_DIST / max_exact)
                         * (REL_BUCKETS - max_exact)).astype(jnp.int32)
    large = jnp.minimum(large, REL_BUCKETS - 1)
    return jnp.where(n < max_exact, n, large)


def causal_rel_bias(table, q0, kv_len):
    n = (q0 + jnp.arange(Q_BLOCK))[:, None] - jnp.arange(kv_len)[None, :]
    return jnp.moveaxis(table[rel_bucket(jnp.maximum(n, 0))], -1, 0).astype(jnp.float32)


def causal_mask(q0, kv_len, strict):
    t = (q0 + jnp.arange(Q_BLOCK))[:, None]
    s = jnp.arange(kv_len)[None, :]
    return s < t if strict else s <= t


def sweep_blocks(block_fn, seq):
    return jnp.concatenate([block_fn(i * Q_BLOCK, (i + 1) * Q_BLOCK) for i in range(seq // Q_BLOCK)], axis=2)


def diff_attention(q, k, v, q_gain, k_gain, lam_vecs, subln_gain, rel_table, lam_init):
    b, s, _ = q.shape
    q = rms_norm(q.reshape(b, s, A_HEADS, 2, HEAD_DIM), q_gain).transpose(0, 3, 2, 1, 4)
    k = rms_norm(k.reshape(b, s, A_HEADS, 2, HEAD_DIM), k_gain).transpose(0, 3, 2, 1, 4)
    v = split_heads(v, A_HEADS)
    lv = lam_vecs.astype(jnp.float32)
    lam = jnp.exp(jnp.sum(lv[0] * lv[1])) - jnp.exp(jnp.sum(lv[2] * lv[3])) + lam_init

    def block(q0, kv_len):
        sc = jnp.einsum('bmhqd,bmhkd->bmhqk', q[:, :, :, q0:q0 + Q_BLOCK], k[:, :, :, :kv_len],
                        preferred_element_type=jnp.float32) * SCALE + causal_rel_bias(rel_table, q0, kv_len)
        p = jax.nn.softmax(jnp.where(causal_mask(q0, kv_len, False), sc, NEG_INF), axis=-1)
        w = p[:, 0] - lam * p[:, 1]
        return jnp.einsum('bhqk,bhkd->bhqd', w.astype(v.dtype), v[:, :, :kv_len])

    o = sweep_blocks(block, s)
    o = rms_norm(o, subln_gain) * (1.0 - lam_init)
    return merge_heads(o)


def stick_breaking_attention(q, k, v):
    b, s, _ = q.shape
    q, k, v = split_heads(q, B_HEADS), split_heads(k, B_HEADS), split_heads(v, B_HEADS)

    def block(q0, kv_len):
        z = jnp.einsum('bhqd,bhkd->bhqk', q[:, :, q0:q0 + Q_BLOCK], k[:, :, :kv_len],
                       preferred_element_type=jnp.float32) * SCALE
        mask = causal_mask(q0, kv_len, True)
        u = jnp.where(mask, jax.nn.log_sigmoid(-z), 0.0)
        tail = lax.cumsum(u, axis=3, reverse=True) - u
        a = jnp.where(mask, jnp.exp(jax.nn.log_sigmoid(z) + tail), 0.0)
        return jnp.einsum('bhqk,bhkd->bhqd', a.astype(v.dtype), v[:, :, :kv_len])

    return merge_heads(sweep_blocks(block, s))


def forgetting_attention(q, k, v, f_logit, f_bias, q_gain, k_gain):
    b, s, _ = q.shape
    q = rms_norm(q.reshape(b, s, C_HEADS, HEAD_DIM), q_gain).transpose(0, 2, 1, 3)
    k = rms_norm(k.reshape(b, s, C_HEADS, HEAD_DIM), k_gain).transpose(0, 2, 1, 3)
    v = split_heads(v, C_HEADS)
    log_f = jax.nn.log_sigmoid((f_logit + f_bias).astype(jnp.float32))
    cum = jnp.cumsum(log_f, axis=1).transpose(0, 2, 1)

    def block(q0, kv_len):
        sc = jnp.einsum('bhqd,bhkd->bhqk', q[:, :, q0:q0 + Q_BLOCK], k[:, :, :kv_len],
                        preferred_element_type=jnp.float32) * SCALE
        sc = sc + cum[:, :, q0:q0 + Q_BLOCK, None] - cum[:, :, None, :kv_len]
        p = jax.nn.softmax(jnp.where(causal_mask(q0, kv_len, False), sc, NEG_INF), axis=-1)
        return jnp.einsum('bhqk,bhkd->bhqd', p.astype(v.dtype), v[:, :, :kv_len])

    return merge_heads(sweep_blocks(block, s))


def dilated_attention(q, k, v, q_gain, k_gain, rel_table):
    b, s, _ = q.shape
    shape5 = (b, s, D_GROUPS, D_HEADS_PER_GROUP, HEAD_DIM)
    q = rms_norm(q.reshape(shape5), q_gain)
    k = rms_norm(k.reshape(shape5), k_gain)
    v = v.reshape(shape5)
    outs, lses = [], []
    for g, (window, dilation) in enumerate(D_PAIRS):
        dist = dilation * jnp.arange(window // dilation + 1)
        bias = rel_table[rel_bucket(dist)][:, g * D_HEADS_PER_GROUP:(g + 1) * D_HEADS_PER_GROUP]
        bias = bias.T.astype(jnp.float32)
        qg, kg, vg = q[:, :, g], k[:, :, g], v[:, :, g]

        def block(q0):
            pos = q0 + jnp.arange(Q_BLOCK)[:, None] - dist[None, :]
            valid = pos >= 0
            idx = jnp.maximum(pos, 0)
            kb = jnp.take(kg, idx, axis=1)
            vb = jnp.take(vg, idx, axis=1)
            qb = lax.dynamic_slice_in_dim(qg, q0, Q_BLOCK, axis=1)
            sc = jnp.einsum('bqhd,bqkhd->bhqk', qb, kb, preferred_element_type=jnp.float32) * SCALE
            sc = jnp.where(valid, sc + bias[None, :, None, :], NEG_INF)
            m = jnp.max(sc, axis=-1, keepdims=True)
            e = jnp.exp(sc - m)
            l = jnp.sum(e, axis=-1, keepdims=True)
            o = jnp.einsum('bhqk,bqkhd->bqhd', (e / l).astype(vg.dtype), vb)
            lse = (m + jnp.log(l))[..., 0].transpose(0, 2, 1)
            return o, lse

        o, lse = lax.map(block, jnp.arange(s // Q_BLOCK) * Q_BLOCK)
        outs.append(o.transpose(1, 0, 2, 3, 4).reshape(b, s, D_HEADS_PER_GROUP, HEAD_DIM))
        lses.append(lse.transpose(1, 0, 2, 3).reshape(b, s, D_HEADS_PER_GROUP))
    alpha = jax.nn.softmax(jnp.stack(lses), axis=0)
    o = jnp.sum(alpha[..., None] * jnp.stack(outs).astype(jnp.float32), axis=0)
    return o.reshape(b, s, -1).astype(v.dtype)


def setup_inputs(seed: int = 0) -> dict:
    key = jax.random.key(seed)
    ks = iter(jax.random.split(key, 32))
    L, D = DEPTH, D_MODEL

    def nrm(shape, scale):
        return jax.random.normal(next(ks), shape, jnp.float32) * scale

    def gain(shape):
        return 1.0 + nrm(shape, 0.02)

    return {
        'x': nrm((BATCH, SEQ, D), 1.0),
        'rel_table': nrm((REL_BUCKETS, REL_HEADS), 0.2),
        'ffn1_norm': gain((L, D)),
        'ffn1_w_in': nrm((L, D, 2 * FFN_HIDDEN), D ** -0.5),
        'ffn1_w_out': nrm((L, FFN_HIDDEN, D), FFN_HIDDEN ** -0.5),
        'mix_norm': gain((L, D)),
        'w_in': nrm((L, D, D_IN), D ** -0.5),
        'gate_bias': nrm((L, N_BRANCH * D), 0.02),
        'forget_bias': 2.0 + nrm((L, C_HEADS), 0.1),
        'a_q_norm': gain((L, HEAD_DIM)),
        'a_k_norm': gain((L, HEAD_DIM)),
        'a_lambda': nrm((L, 4, HEAD_DIM), 0.1),
        'a_subln': gain((L, 2 * HEAD_DIM)),
        'c_q_norm': gain((L, HEAD_DIM)),
        'c_k_norm': gain((L, HEAD_DIM)),
        'd_q_norm': gain((L, HEAD_DIM)),
        'd_k_norm': gain((L, HEAD_DIM)),
        'w_branch': jnp.concatenate([nrm((L, w, D), w ** -0.5) for w in BRANCH_WIDTHS], axis=1),
        'w_out': nrm((L, D, D), D ** -0.5),
        'ffn2_norm': gain((L, D)),
        'ffn2_w_in': nrm((L, D, 2 * FFN_HIDDEN), D ** -0.5),
        'ffn2_w_out': nrm((L, FFN_HIDDEN, D), FFN_HIDDEN ** -0.5),
    }


def reference(x, rel_table, ffn1_norm, ffn1_w_in, ffn1_w_out, mix_norm, w_in, gate_bias, forget_bias,
              a_q_norm, a_k_norm, a_lambda, a_subln, c_q_norm, c_k_norm, d_q_norm, d_k_norm,
              w_branch, w_out, ffn2_norm, ffn2_w_in, ffn2_w_out):
    b, s, _ = x.shape
    in_idx = _offsets(IN_SPLITS)
    br_idx = _offsets(BRANCH_WIDTHS)
    for l in range(DEPTH):
        x = x + 0.5 * swiglu(rms_norm(x, ffn1_norm[l]), ffn1_w_in[l], ffn1_w_out[l])
        h = rms_norm(x, mix_norm[l])
        (aq, ak, av, bq, bk, bv, cq, ck, cv, cf, dq, dk, dv, gl) = jnp.split(h @ w_in[l], in_idx, axis=-1)
        lam_init = 0.8 - 0.6 * math.exp(-0.3 * l)
        oa = diff_attention(aq, ak, av, a_q_norm[l], a_k_norm[l], a_lambda[l], a_subln[l],
                            rel_table[:, :A_HEADS], lam_init)
        ob = stick_breaking_attention(bq, bk, bv)
        oc = forgetting_attention(cq, ck, cv, cf, forget_bias[l], c_q_norm[l], c_k_norm[l])
        od = dilated_attention(dq, dk, dv, d_q_norm[l], d_k_norm[l], rel_table[:, A_HEADS:])
        wa, wb, wc, wd = jnp.split(w_branch[l], br_idx, axis=0)
        gates = jax.nn.sigmoid(gl + gate_bias[l]).reshape(b, s, N_BRANCH, D_MODEL)
        merged = (gates[:, :, 0] * (oa @ wa) + gates[:, :, 1] * (ob @ wb)
                  + gates[:, :, 2] * (oc @ wc) + gates[:, :, 3] * (od @ wd))
        x = x + merged @ w_out[l]
        x = x + 0.5 * swiglu(rms_norm(x, ffn2_norm[l]), ffn2_w_in[l], ffn2_w_out[l])
    return x
```

```python
import functools
import math

import numpy as np
import jax
import jax.numpy as jnp
from jax import lax
from jax.experimental import pallas as pl
from jax.experimental.pallas import tpu as pltpu

F32 = jnp.float32
BF16 = jnp.bfloat16

D_MODEL = 1024
HEAD_DIM = 64
SCALE = HEAD_DIM ** -0.5
A_HEADS = 4
B_HEADS = 6
C_HEADS = 6
D_PAIRS = ((128, 1), (512, 4), (2048, 16))
D_GROUPS = len(D_PAIRS)
N_BRANCH = 4
FFN_HIDDEN = 2816
REL_BUCKETS = 32
REL_MAX_DIST = 128
RMS_EPS = 1e-6
NEG_INF = -1e30

LANES = 128
A_W = A_HEADS * 2 * HEAD_DIM
B_W = B_HEADS * HEAD_DIM
C_W = C_HEADS * HEAD_DIM
D_W = D_GROUPS * 2 * HEAD_DIM
ABC_W = 3 * (A_W + B_W + C_W)

TOK_TILE = 512
FFN_CHUNK = 256
ATT_T = 256
D_T = 128
VMEM_LIMIT = 56 * 1024 * 1024


def _cparams(n_axes):
    return pltpu.CompilerParams(dimension_semantics=("arbitrary",) * n_axes,
                                vmem_limit_bytes=VMEM_LIMIT)


def _rms(x, g):
    ms = jnp.mean(x * x, axis=-1, keepdims=True)
    return x * lax.rsqrt(ms + RMS_EPS) * g


def _split3(a):
    hi = a.astype(BF16)
    r1 = a - hi.astype(F32)
    mid = r1.astype(BF16)
    lo = (r1 - mid.astype(F32)).astype(BF16)
    return hi, mid, lo


def _dot_exact_rhs(a, b_bf16):
    hi, mid, lo = _split3(a)
    d = functools.partial(jnp.dot, preferred_element_type=F32)
    return d(hi, b_bf16) + d(mid, b_bf16) + d(lo, b_bf16)


def _dot_exact_lhs(a_bf16, b):
    hi, mid, lo = _split3(b)
    d = functools.partial(jnp.dot, preferred_element_type=F32)
    return d(a_bf16, hi) + d(a_bf16, mid) + d(a_bf16, lo)


def _qk(q, k):
    return lax.dot_general(q, k, (((1,), (1,)), ((), ())), preferred_element_type=F32)


def _log_sigmoid(x):
    return jnp.minimum(x, 0.0) - jnp.log(1.0 + jnp.exp(-jnp.abs(x)))


def _lane_half(shape):
    return lax.broadcasted_iota(jnp.int32, shape, len(shape) - 1) < HEAD_DIM


def _ffn_kernel(x_ref, g_ref, wi_ref, wo_ref, o_ref):
    x = x_ref[...]
    h = _rms(x, g_ref[...]).astype(BF16)
    acc = jnp.zeros(x.shape, F32)
    for c in range(FFN_HIDDEN // FFN_CHUNK):
        gu = jnp.dot(h, wi_ref[:, c * 2 * FFN_CHUNK:(c + 1) * 2 * FFN_CHUNK],
                     preferred_element_type=F32)
        gate, up = gu[:, :FFN_CHUNK], gu[:, FFN_CHUNK:]
        a = (gate * jax.nn.sigmoid(gate) * up).astype(BF16)
        acc = acc + jnp.dot(a, wo_ref[c * FFN_CHUNK:(c + 1) * FFN_CHUNK, :],
                            preferred_element_type=F32)
    o_ref[...] = x + 0.5 * acc


def _ffn(x2, gain, w_i, w_o):
    n = x2.shape[0]
    nch = FFN_HIDDEN // FFN_CHUNK
    w_gu = jnp.stack([w_i[:, :FFN_HIDDEN].reshape(D_MODEL, nch, FFN_CHUNK),
                      w_i[:, FFN_HIDDEN:].reshape(D_MODEL, nch, FFN_CHUNK)], axis=2)
    w_gu = w_gu.reshape(D_MODEL, 2 * FFN_HIDDEN).astype(BF16)
    const = lambda i: (0, 0)
    return pl.pallas_call(
        _ffn_kernel,
        grid=(n // TOK_TILE,),
        in_specs=[
            pl.BlockSpec((TOK_TILE, D_MODEL), lambda i: (i, 0)),
            pl.BlockSpec((1, D_MODEL), const),
            pl.BlockSpec((D_MODEL, 2 * FFN_HIDDEN), const, pipeline_mode=pl.Buffered(1)),
            pl.BlockSpec((FFN_HIDDEN, D_MODEL), const, pipeline_mode=pl.Buffered(1)),
        ],
        out_specs=pl.BlockSpec((TOK_TILE, D_MODEL), lambda i: (i, 0)),
        out_shape=jax.ShapeDtypeStruct((n, D_MODEL), F32),
        compiler_params=_cparams(1),
        name="ffn",
    )(x2, gain.reshape(1, D_MODEL), w_gu, w_o.astype(BF16))


_ABC_SECTIONS = ((A_W, True), (A_W, True), (A_W, False),
                 (B_W, False), (B_W, False), (B_W, False),
                 (C_W, True), (C_W, True), (C_W, False))
_D_SECTIONS = ((D_W, True), (D_W, True), (D_W, False))


def _proj_sections(h, w_ref, gain_ref, gmat, out_ref, sections):
    off = 0
    for width, normed in sections:
        y = jnp.dot(h, w_ref[:, off:off + width], preferred_element_type=F32)
        for j in range(width // LANES):
            yb = y[:, j * LANES:(j + 1) * LANES]
            gain = gain_ref[:, off + j * LANES:off + (j + 1) * LANES]
            if normed:
                ms = _dot_exact_rhs(yb * yb, gmat)
                yb = yb * lax.rsqrt(ms + RMS_EPS)
            out_ref[:, off + j * LANES:off + (j + 1) * LANES] = (yb * gain).astype(out_ref.dtype)
        off += width


def _proj_kernel(x_ref, g_ref, wabc_ref, wd_ref, wcf_ref, gabc_ref, gd_ref, gmat_ref,
                 pabc_ref, pd_ref, cf_ref):
    h = _rms(x_ref[...], g_ref[...]).astype(BF16)
    gmat = gmat_ref[...]
    _proj_sections(h, wabc_ref, gabc_ref, gmat, pabc_ref, _ABC_SECTIONS)
    _proj_sections(h, wd_ref, gd_ref, gmat, pd_ref, _D_SECTIONS)
    cf_ref[...] = jnp.dot(h, wcf_ref[...], preferred_element_type=F32)


def _head_gain(gain, width, scale):
    return jnp.tile(gain.astype(F32), width // HEAD_DIM) * scale


def _proj(x2, gain, w_in, a_q, a_k, c_q, c_k, d_q, d_k):
    n = x2.shape[0]
    cf_off = ABC_W
    d_off = ABC_W + C_HEADS
    w_abc = w_in[:, :ABC_W].astype(BF16)
    w_d = w_in[:, d_off:d_off + 3 * D_W].astype(BF16)
    w_cf = jnp.pad(w_in[:, cf_off:cf_off + C_HEADS], ((0, 0), (0, LANES - C_HEADS))).astype(BF16)
    ones = lambda w, s: jnp.full((w,), s, F32)
    g_abc = jnp.concatenate([
        _head_gain(a_q, A_W, SCALE), _head_gain(a_k, A_W, 1.0), ones(A_W, 1.0),
        ones(B_W, SCALE), ones(B_W, 1.0), ones(B_W, 1.0),
        _head_gain(c_q, C_W, SCALE), _head_gain(c_k, C_W, 1.0), ones(C_W, 1.0)]).reshape(1, ABC_W)
    g_d = jnp.concatenate([_head_gain(d_q, D_W, SCALE), _head_gain(d_k, D_W, 1.0),
                           ones(D_W, 1.0)]).reshape(1, 3 * D_W)
    lane = np.arange(LANES)
    gmat = jnp.asarray((lane[:, None] // HEAD_DIM == lane[None, :] // HEAD_DIM) / HEAD_DIM, BF16)
    const = lambda i: (0, 0)
    row = lambda i: (i, 0)
    return pl.pallas_call(
        _proj_kernel,
        grid=(n // TOK_TILE,),
        in_specs=[
            pl.BlockSpec((TOK_TILE, D_MODEL), row),
            pl.BlockSpec((1, D_MODEL), const),
            pl.BlockSpec((D_MODEL, ABC_W), const, pipeline_mode=pl.Buffered(1)),
            pl.BlockSpec((D_MODEL, 3 * D_W), const, pipeline_mode=pl.Buffered(1)),
            pl.BlockSpec((D_MODEL, LANES), const),
            pl.BlockSpec((1, ABC_W), const),
            pl.BlockSpec((1, 3 * D_W), const),
            pl.BlockSpec((LANES, LANES), const),
        ],
        out_specs=[
            pl.BlockSpec((TOK_TILE, ABC_W), row),
            pl.BlockSpec((TOK_TILE, 3 * D_W), row),
            pl.BlockSpec((TOK_TILE, LANES), row),
        ],
        out_shape=[
            jax.ShapeDtypeStruct((n, ABC_W), BF16),
            jax.ShapeDtypeStruct((n, 3 * D_W), F32),
            jax.ShapeDtypeStruct((n, LANES), F32),
        ],
        compiler_params=_cparams(1),
        name="proj",
    )(x2, gain.reshape(1, D_MODEL), w_abc, w_d, w_cf, g_abc, g_d, gmat)


_MASKED_BUCKET = REL_BUCKETS


def _rel_bucket_np(n):
    n = np.asarray(n, np.int64)
    max_exact = REL_BUCKETS // 2
    nf = np.maximum(n, 1).astype(np.float32)
    large = max_exact + (np.log(nf / np.float32(max_exact)) / np.float32(math.log(REL_MAX_DIST / max_exact))
                         * np.float32(REL_BUCKETS - max_exact)).astype(np.int32)
    large = np.minimum(large, REL_BUCKETS - 1)
    return np.where(n < max_exact, n, large).astype(np.int32)


def _bucket_maps():
    t = ATT_T
    x, y = np.arange(t)[:, None], np.arange(t)[None, :]
    diag = np.where(x >= y, _rel_bucket_np(np.maximum(x - y, 0)), _MASKED_BUCKET)
    prev = _rel_bucket_np(t + x - y)
    idx_a = np.stack([diag, prev]).astype(np.int32)
    x, y = np.arange(D_T)[:, None], np.arange(D_T)[None, :]
    idx_d = np.empty((D_GROUPS, 2, D_T, 2 * D_T), np.int32)
    for g, (_, dil) in enumerate(D_PAIRS):
        prev = np.where(y >= x, _rel_bucket_np(dil * (D_T + x - y)), _MASKED_BUCKET)
        diag = np.where(y <= x, _rel_bucket_np(dil * np.maximum(x - y, 0)), _MASKED_BUCKET)
        idx_d[g, 0] = np.concatenate([prev, diag], axis=1)
        idx_d[g, 1] = np.concatenate([np.full_like(prev, _MASKED_BUCKET), diag], axis=1)
    return idx_a, idx_d


def _bias_kernel(tbl_ref, idxa_ref, idxd_ref, ba_ref, bd_ref):
    for h in range(A_HEADS):
        for kind in range(2):
            ba_ref[h, kind] = jnp.where(idxa_ref[kind] == _MASKED_BUCKET, NEG_INF, 0.0)
    for g in range(D_GROUPS):
        for h in range(2):
            for var in range(2):
                bd_ref[g, h, var] = jnp.where(idxd_ref[g, var] == _MASKED_BUCKET, NEG_INF, 0.0)

    def body(b, carry):
        for h in range(A_HEADS):
            val = tbl_ref[b, h] - tbl_ref[REL_BUCKETS - 1, h]
            for kind in range(2):
                ba_ref[h, kind] = jnp.where(idxa_ref[kind] == b, val, ba_ref[h, kind])
        for g in range(D_GROUPS):
            for h in range(2):
                val = tbl_ref[b, A_HEADS + 2 * g + h]
                for var in range(2):
                    bd_ref[g, h, var] = jnp.where(idxd_ref[g, var] == b, val, bd_ref[g, h, var])
        return carry

    lax.fori_loop(0, REL_BUCKETS, body, 0)


def _bias_tiles(rel_table):
    idx_a, idx_d = _bucket_maps()
    return pl.pallas_call(
        _bias_kernel,
        in_specs=[pl.BlockSpec(memory_space=pltpu.SMEM),
                  pl.BlockSpec(memory_space=pltpu.VMEM),
                  pl.BlockSpec(memory_space=pltpu.VMEM)],
        out_specs=[pl.BlockSpec(memory_space=pltpu.VMEM),
                   pl.BlockSpec(memory_space=pltpu.VMEM)],
        out_shape=[jax.ShapeDtypeStruct((A_HEADS, 2, ATT_T, ATT_T), F32),
                   jax.ShapeDtypeStruct((D_GROUPS, 2, 2, D_T, 2 * D_T), F32)],
        compiler_params=pltpu.CompilerParams(vmem_limit_bytes=VMEM_LIMIT),
        name="rel_bias",
    )(rel_table.astype(F32), jnp.asarray(idx_a), jnp.asarray(idx_d))


def _kv_tile(ref, kj):
    return ref[pl.ds(pl.multiple_of(kj * ATT_T, ATT_T), ATT_T), :]


def _softmax_step(s, m, l, acc_ref, idx, v):
    m_new = jnp.maximum(m, jnp.max(s, axis=-1, keepdims=True))
    alpha = jnp.exp(m - m_new)
    p = jnp.exp(s - m_new)
    l_new = alpha * l + jnp.sum(p, axis=-1, keepdims=True)
    acc_ref[idx] = alpha * acc_ref[idx] + jnp.dot(p.astype(BF16), v, preferred_element_type=F32)
    return m_new, l_new


def _split_pair(q):
    first = _lane_half(q.shape)
    zero = jnp.zeros_like(q)
    return jnp.where(first, q, zero), jnp.where(first, zero, q)


def _attn_specs(col0, n_pairs, s):
    nq = s // ATT_T
    q_spec = pl.BlockSpec((ATT_T, LANES), lambda b, p, i: (b * nq + i, col0 + p))
    k_spec = pl.BlockSpec((s, LANES), lambda b, p, i: (b, col0 + n_pairs + p))
    v_spec = pl.BlockSpec((s, LANES), lambda b, p, i: (b, col0 + 2 * n_pairs + p))
    o_spec = pl.BlockSpec((ATT_T, LANES), lambda b, p, i: (b * nq + i, p))
    return q_spec, k_spec, v_spec, o_spec


def _attn_a_kernel(q_ref, k_ref, v_ref, bias_ref, lam_ref, subln_ref, o_ref, acc_ref, *, lam_init):
    qi = pl.program_id(2)
    qs = _split_pair(q_ref[...])
    acc_ref[...] = jnp.zeros(acc_ref.shape, F32)
    col = lambda val: jnp.full((ATT_T, 1), val, F32)
    init = (col(NEG_INF), col(0.0), col(NEG_INF), col(0.0))

    def step(kj, carry, bias):
        k, v = _kv_tile(k_ref, kj), _kv_tile(v_ref, kj)
        out = []
        for half in range(2):
            s = _qk(qs[half], k)
            if bias is not None:
                s = s + bias
            out += _softmax_step(s, carry[2 * half], carry[2 * half + 1], acc_ref, half, v)
        return tuple(out)

    n_far = jnp.maximum(qi - 1, 0)
    carry = lax.fori_loop(0, n_far, lambda kj, c: step(kj, c, None), init)
    carry = lax.fori_loop(n_far, qi + 1, lambda kj, c: step(kj, c, bias_ref[0, qi - kj]), carry)
    m0, l0, m1, l1 = carry

    lv = lam_ref[...]
    lam = (jnp.exp(jnp.sum(lv[0:1] * lv[1:2], axis=-1, keepdims=True))
           - jnp.exp(jnp.sum(lv[2:3] * lv[3:4], axis=-1, keepdims=True)) + lam_init)
    o = acc_ref[0] / l0 - lam * (acc_ref[1] / l1)
    o_ref[...] = (_rms(o, subln_ref[...]) * (1.0 - lam_init)).astype(o_ref.dtype)


def _attn_a(p_abc, bias_a, lam_vecs, subln, lam_init, batch, s):
    n = p_abc.shape[0]
    q_spec, k_spec, v_spec, o_spec = _attn_specs(0, A_HEADS, s)
    return pl.pallas_call(
        functools.partial(_attn_a_kernel, lam_init=lam_init),
        grid=(batch, A_HEADS, s // ATT_T),
        in_specs=[q_spec, k_spec, v_spec,
                  pl.BlockSpec((1, 2, ATT_T, ATT_T), lambda b, p, i: (p, 0, 0, 0)),
                  pl.BlockSpec((4, HEAD_DIM), lambda b, p, i: (0, 0)),
                  pl.BlockSpec((1, 2 * HEAD_DIM), lambda b, p, i: (0, 0))],
        out_specs=o_spec,
        out_shape=jax.ShapeDtypeStruct((n, A_W), BF16),
        scratch_shapes=[pltpu.VMEM((2, ATT_T, LANES), F32)],
        compiler_params=_cparams(3),
        name="attn_diff",
    )(p_abc, p_abc, p_abc, bias_a, lam_vecs.astype(F32), subln.reshape(1, 2 * HEAD_DIM).astype(F32))


def _attn_b_kernel(q_ref, k_ref, v_ref, o_ref, acc_ref):
    qi = pl.program_id(2)
    qs = _split_pair(q_ref[...])
    acc_ref[...] = jnp.zeros(acc_ref.shape, F32)
    row = lax.broadcasted_iota(jnp.int32, (ATT_T, ATT_T), 0)
    colm = lax.broadcasted_iota(jnp.int32, (ATT_T, ATT_T), 1)
    later = jnp.where(row > colm, 1.0, 0.0).astype(BF16)
    strict = colm < row

    def step(kj, rsum, diag):
        k, v = _kv_tile(k_ref, kj), _kv_tile(v_ref, kj)
        out = []
        for h in range(2):
            z = _qk(qs[h], k)
            ls = _log_sigmoid(z)
            u = ls - z
            if diag:
                u = jnp.where(strict, u, 0.0)
            u_hi = u.astype(BF16)
            u_lo = (u - u_hi.astype(F32)).astype(BF16)
            tail = (jnp.dot(u_hi, later, preferred_element_type=F32)
                    + jnp.dot(u_lo, later, preferred_element_type=F32) + rsum[h])
            a = jnp.exp(ls + tail)
            if diag:
                a = jnp.where(strict, a, 0.0)
            acc_ref[h] = acc_ref[h] + jnp.dot(a.astype(BF16), v, preferred_element_type=F32)
            out.append(rsum[h] + jnp.sum(u, axis=-1, keepdims=True))
        return tuple(out)

    zero = jnp.zeros((ATT_T, 1), F32)
    rsum = step(qi, (zero, zero), True)
    lax.fori_loop(0, qi, lambda t, c: step(qi - 1 - t, c, False), rsum)
    o_ref[...] = jnp.where(_lane_half((ATT_T, LANES)), acc_ref[0], acc_ref[1]).astype(o_ref.dtype)


def _attn_b(p_abc, batch, s):
    n = p_abc.shape[0]
    q_spec, k_spec, v_spec, o_spec = _attn_specs(3 * A_W // LANES, B_HEADS // 2, s)
    return pl.pallas_call(
        _attn_b_kernel,
        grid=(batch, B_HEADS // 2, s // ATT_T),
        in_specs=[q_spec, k_spec, v_spec],
        out_specs=o_spec,
        out_shape=jax.ShapeDtypeStruct((n, B_W), BF16),
        scratch_shapes=[pltpu.VMEM((2, ATT_T, LANES), F32)],
        compiler_params=_cparams(3),
        name="attn_stick",
    )(p_abc, p_abc, p_abc)


def _attn_c_kernel(q_ref, k_ref, v_ref, cf_ref, fb_ref, o_ref, acc_ref, cumc_ref, cumr_ref):
    pair, qi = pl.program_id(1), pl.program_id(2)
    s_len = cf_ref.shape[0]

    @pl.when(qi == 0)
    def _():
        r = lax.broadcasted_iota(jnp.int32, (LANES, LANES), 0)
        c = lax.broadcasted_iota(jnp.int32, (LANES, LANES), 1)
        lower = jnp.where(c <= r, 1.0, 0.0).astype(BF16)
        carry = jnp.zeros((1, LANES), F32)
        for blk in range(s_len // LANES):
            rows = slice(blk * LANES, (blk + 1) * LANES)
            log_f = _log_sigmoid(cf_ref[rows, :] + fb_ref[...])
            cum = _dot_exact_lhs(lower, log_f) + carry
            cumc_ref[rows, :] = cum
            cumr_ref[:, rows] = cum.T
            carry = cum[LANES - 1:LANES, :]

    qs = _split_pair(q_ref[...])
    acc_ref[...] = jnp.zeros(acc_ref.shape, F32)
    q_rows = pl.ds(pl.multiple_of(qi * ATT_T, ATT_T), ATT_T)
    cum_q_tile = cumc_ref[q_rows, :]
    lane = lax.broadcasted_iota(jnp.int32, (ATT_T, LANES), 1)
    cum_q = [jnp.sum(jnp.where(lane == 2 * pair + h, cum_q_tile, 0.0), axis=-1, keepdims=True)
             for h in range(2)]
    causal = (lax.broadcasted_iota(jnp.int32, (ATT_T, ATT_T), 1)
              <= lax.broadcasted_iota(jnp.int32, (ATT_T, ATT_T), 0))
    col = lambda val: jnp.full((ATT_T, 1), val, F32)

    def step(kj, carry, diag):
        k, v = _kv_tile(k_ref, kj), _kv_tile(v_ref, kj)
        k_cols = pl.ds(pl.multiple_of(kj * ATT_T, ATT_T), ATT_T)
        out = []
        for h in range(2):
            cum_k = cumr_ref[pl.ds(2 * pair + h, 1), k_cols]
            s = _qk(qs[h], k) + cum_q[h] - cum_k
            if diag:
                s = jnp.where(causal, s, NEG_INF)
            out += _softmax_step(s, carry[2 * h], carry[2 * h + 1], acc_ref, h, v)
        return tuple(out)

    carry = lax.fori_loop(0, qi, lambda kj, c: step(kj, c, False),
                          (col(NEG_INF), col(0.0), col(NEG_INF), col(0.0)))
    m0, l0, m1, l1 = step(qi, carry, True)
    o_ref[...] = jnp.where(_lane_half((ATT_T, LANES)), acc_ref[0] / l0,
                           acc_ref[1] / l1).astype(o_ref.dtype)


def _attn_c(p_abc, cf, forget_bias, batch, s):
    n = p_abc.shape[0]
    q_spec, k_spec, v_spec, o_spec = _attn_specs((3 * A_W + 3 * B_W) // LANES, C_HEADS // 2, s)
    fb = jnp.pad(forget_bias.astype(F32), (0, LANES - C_HEADS)).reshape(1, LANES)
    return pl.pallas_call(
        _attn_c_kernel,
        grid=(batch, C_HEADS // 2, s // ATT_T),
        in_specs=[q_spec, k_spec, v_spec,
                  pl.BlockSpec((s, LANES), lambda b, p, i: (b, 0)),
                  pl.BlockSpec((1, LANES), lambda b, p, i: (0, 0))],
        out_specs=o_spec,
        out_shape=jax.ShapeDtypeStruct((n, C_W), BF16),
        scratch_shapes=[pltpu.VMEM((2, ATT_T, LANES), F32),
                        pltpu.VMEM((s, LANES), F32),
                        pltpu.VMEM((LANES, s), F32)],
        compiler_params=_cparams(3),
        name="attn_forget",
    )(p_abc, p_abc, p_abc, cf, fb)


def _attn_d_kernel(*refs):
    qkv_refs, (bias_ref, o_ref, qc_ref, kc_ref, vc_ref, oc_ref, lc_ref, og_ref, lse_ref) = (
        refs[:3 * D_GROUPS], refs[3 * D_GROUPS:])
    s_len = o_ref.shape[0]
    n_tiles = s_len // D_T
    first = _lane_half((D_T, LANES))

    for g, (_, dil) in enumerate(D_PAIRS):
        q_ref, k_ref, v_ref = qkv_refs[g], qkv_refs[D_GROUPS + g], qkv_refs[2 * D_GROUPS + g]
        class_len = s_len // dil
        tiles_per_class = class_len // D_T

        for r in range(dil):
            src = pl.ds(r, class_len, stride=dil) if dil > 1 else slice(None)
            dst = slice(r * class_len, (r + 1) * class_len)
            qc_ref[dst, :] = q_ref[src, :].astype(BF16)
            kc_ref[dst, :] = k_ref[src, :].astype(BF16)
            vc_ref[dst, :] = v_ref[src, :].astype(BF16)

        def rows(t):
            return pl.ds(pl.multiple_of(t * D_T, D_T), D_T)

        def tile(t, carry):
            q = qc_ref[rows(t), :]
            k_diag = kc_ref[rows(t), :]
            v_diag = vc_ref[rows(t), :]
            if tiles_per_class > 1:
                t_prev = jnp.maximum(t - 1, 0)
                k_prev = kc_ref[rows(t_prev), :]
                v_prev = vc_ref[rows(t_prev), :]
                variant = jnp.where(t % tiles_per_class == 0, 1, 0)
            o_heads, lse_heads = [], []
            for h, qh in enumerate(_split_pair(q)):
                if tiles_per_class > 1:
                    bias = bias_ref[g, h, variant]
                    s_prev = _qk(qh, k_prev) + bias[:, :D_T]
                    s_diag = _qk(qh, k_diag) + bias[:, D_T:]
                    m = jnp.maximum(jnp.max(s_prev, axis=-1, keepdims=True),
                                    jnp.max(s_diag, axis=-1, keepdims=True))
                    e_prev, e_diag = jnp.exp(s_prev - m), jnp.exp(s_diag - m)
                    l = (jnp.sum(e_prev, axis=-1, keepdims=True)
                         + jnp.sum(e_diag, axis=-1, keepdims=True))
                    o = (jnp.dot(e_prev.astype(BF16), v_prev, preferred_element_type=F32)
                         + jnp.dot(e_diag.astype(BF16), v_diag, preferred_element_type=F32))
                else:
                    s_diag = _qk(qh, k_diag) + bias_ref[g, h, 1][:, D_T:]
                    m = jnp.max(s_diag, axis=-1, keepdims=True)
                    e_diag = jnp.exp(s_diag - m)
                    l = jnp.sum(e_diag, axis=-1, keepdims=True)
                    o = jnp.dot(e_diag.astype(BF16), v_diag, preferred_element_type=F32)
                o_heads.append(o / l)
                lse_heads.append(jnp.broadcast_to(m + jnp.log(l), (D_T, LANES)))
            oc_ref[rows(t), :] = jnp.where(first, o_heads[0], o_heads[1])
            lc_ref[rows(t), :] = jnp.where(first, lse_heads[0], lse_heads[1])
            return carry

        lax.fori_loop(0, n_tiles, tile, 0)

        for r in range(dil):
            dst = pl.ds(r, class_len, stride=dil) if dil > 1 else slice(None)
            src = slice(r * class_len, (r + 1) * class_len)
            og_ref[g, dst, :] = oc_ref[src, :]
            lse_ref[g, dst, :] = lc_ref[src, :]

    lse = [lse_ref[g] for g in range(D_GROUPS)]
    top = functools.reduce(jnp.maximum, lse)
    w = [jnp.exp(x - top) for x in lse]
    num = sum(w[g] * og_ref[g] for g in range(D_GROUPS))
    o_ref[...] = (num / sum(w)).astype(o_ref.dtype)


def _attn_d(p_d, bias_d, batch, s):
    n = p_d.shape[0]
    return pl.pallas_call(
        _attn_d_kernel,
        grid=(batch,),
        in_specs=([pl.BlockSpec((s, LANES), functools.partial(lambda b, c: (b, c), c=c))
                   for c in range(3 * D_GROUPS)]
                  + [pl.BlockSpec(bias_d.shape, lambda b: (0, 0, 0, 0, 0))]),
        out_specs=pl.BlockSpec((s, LANES), lambda b: (b, 0)),
        out_shape=jax.ShapeDtypeStruct((n, LANES), BF16),
        scratch_shapes=[pltpu.VMEM((s, LANES), BF16), pltpu.VMEM((s, LANES), BF16),
                        pltpu.VMEM((s, LANES), BF16),
                        pltpu.VMEM((s, LANES), F32), pltpu.VMEM((s, LANES), F32),
                        pltpu.VMEM((D_GROUPS, s, LANES), F32),
                        pltpu.VMEM((D_GROUPS, s, LANES), F32)],
        compiler_params=_cparams(1),
        name="attn_dilated",
    )(*([p_d] * (3 * D_GROUPS)), bias_d)


def _merge_kernel(x_ref, g_ref, wg_ref, gb_ref, oa_ref, ob_ref, oc_ref, od_ref,
                  wa_ref, wb_ref, wc_ref, wd_ref, wout_ref, o_ref):
    x = x_ref[...]
    h = _rms(x, g_ref[...]).astype(BF16)
    merged = jnp.zeros(x.shape, F32)
    branches = ((oa_ref, wa_ref), (ob_ref, wb_ref), (oc_ref, wc_ref), (od_ref, wd_ref))
    for i, (br_ref, w_ref) in enumerate(branches):
        cols = slice(i * D_MODEL, (i + 1) * D_MODEL)
        logits = jnp.dot(h, wg_ref[:, cols], preferred_element_type=F32) + gb_ref[:, cols]
        branch = jnp.dot(br_ref[...], w_ref[...], preferred_element_type=F32)
        merged = merged + jax.nn.sigmoid(logits) * branch
    o_ref[...] = x + jnp.dot(merged.astype(BF16), wout_ref[...], preferred_element_type=F32)


def _merge(x2, gain, w_gate, gate_bias, outs, w_branch, w_out):
    n = x2.shape[0]
    widths = (A_W, B_W, C_W, 2 * HEAD_DIM)
    offs = np.concatenate([[0], np.cumsum(widths)])
    w_br = [w_branch[offs[i]:offs[i + 1]].astype(BF16) for i in range(N_BRANCH)]
    const = lambda i: (0, 0)
    row = lambda i: (i, 0)
    return pl.pallas_call(
        _merge_kernel,
        grid=(n // TOK_TILE,),
        in_specs=([pl.BlockSpec((TOK_TILE, D_MODEL), row),
                   pl.BlockSpec((1, D_MODEL), const),
                   pl.BlockSpec((D_MODEL, N_BRANCH * D_MODEL), const, pipeline_mode=pl.Buffered(1)),
                   pl.BlockSpec((1, N_BRANCH * D_MODEL), const)]
                  + [pl.BlockSpec((TOK_TILE, w), row) for w in widths]
                  + [pl.BlockSpec((w, D_MODEL), const) for w in widths]
                  + [pl.BlockSpec((D_MODEL, D_MODEL), const)]),
        out_specs=pl.BlockSpec((TOK_TILE, D_MODEL), row),
        out_shape=jax.ShapeDtypeStruct((n, D_MODEL), F32),
        compiler_params=_cparams(1),
        name="merge",
    )(x2, gain.reshape(1, D_MODEL), w_gate.astype(BF16), gate_bias.reshape(1, -1).astype(F32),
      *outs, *w_br, w_out.astype(BF16))


def kernel(x, rel_table, ffn1_norm, ffn1_w_in, ffn1_w_out, mix_norm, w_in, gate_bias, forget_bias,
           a_q_norm, a_k_norm, a_lambda, a_subln, c_q_norm, c_k_norm, d_q_norm, d_k_norm,
           w_branch, w_out, ffn2_norm, ffn2_w_in, ffn2_w_out):
    batch, s, _ = x.shape
    depth = w_in.shape[0]
    assert s % (D_T * D_PAIRS[-1][1]) == 0 and (batch * s) % TOK_TILE == 0
    gate_off = ABC_W + C_HEADS + 3 * D_W
    bias_a, bias_d = _bias_tiles(rel_table)
    x2 = x.reshape(batch * s, D_MODEL)
    for l in range(depth):
        x2 = _ffn(x2, ffn1_norm[l], ffn1_w_in[l], ffn1_w_out[l])
        p_abc, p_d, cf = _proj(x2, mix_norm[l], w_in[l], a_q_norm[l], a_k_norm[l],
                               c_q_norm[l], c_k_norm[l], d_q_norm[l], d_k_norm[l])
        lam_init = 0.8 - 0.6 * math.exp(-0.3 * l)
        oa = _attn_a(p_abc, bias_a, a_lambda[l], a_subln[l], lam_init, batch, s)
        ob = _attn_b(p_abc, batch, s)
        oc = _attn_c(p_abc, cf, forget_bias[l], batch, s)
        od = _attn_d(p_d, bias_d, batch, s)
        x2 = _merge(x2, mix_norm[l], w_in[l][:, gate_off:], gate_bias[l], (oa, ob, oc, od),
                    w_branch[l], w_out[l])
        x2 = _ffn(x2, ffn2_norm[l], ffn2_w_in[l], ffn2_w_out[l])
    return x2.reshape(batch, s, D_MODEL)
```

```python
import functools
import math

import numpy as np
import jax
import jax.numpy as jnp
from jax import lax
from jax.experimental import pallas as pl
from jax.experimental.pallas import tpu as pltpu

F32 = jnp.float32
BF16 = jnp.bfloat16

D_MODEL = 1024
HEAD_DIM = 64
SCALE = HEAD_DIM ** -0.5
A_HEADS = 4
B_HEADS = 6
C_HEADS = 6
D_PAIRS = ((128, 1), (512, 4), (2048, 16))
D_GROUPS = len(D_PAIRS)
N_BRANCH = 4
FFN_HIDDEN = 2816
REL_BUCKETS = 32
REL_MAX_DIST = 128
RMS_EPS = 1e-6
NEG_INF = -1e30

LANES = 128
A_W = A_HEADS * 2 * HEAD_DIM
B_W = B_HEADS * HEAD_DIM
C_W = C_HEADS * HEAD_DIM
D_W = D_GROUPS * 2 * HEAD_DIM
ABC_W = 3 * (A_W + B_W + C_W)

TOK_TILE = 512
FFN_CHUNK = 256
ATT_T = 256
ATT_WIN = 2 * ATT_T
LOG2E = math.log2(math.e)
D_T = 128
VMEM_LIMIT = 56 * 1024 * 1024


def _cparams(n_axes):
    return pltpu.CompilerParams(dimension_semantics=("arbitrary",) * n_axes,
                                vmem_limit_bytes=VMEM_LIMIT)


def _rms(x, g):
    ms = jnp.mean(x * x, axis=-1, keepdims=True)
    return x * lax.rsqrt(ms + RMS_EPS) * g


def _split3(a):
    hi = a.astype(BF16)
    r1 = a - hi.astype(F32)
    mid = r1.astype(BF16)
    lo = (r1 - mid.astype(F32)).astype(BF16)
    return hi, mid, lo


def _dot_exact_rhs(a, b_bf16):
    hi, mid, lo = _split3(a)
    d = functools.partial(jnp.dot, preferred_element_type=F32)
    return d(hi, b_bf16) + d(mid, b_bf16) + d(lo, b_bf16)


def _dot_exact_lhs(a_bf16, b):
    hi, mid, lo = _split3(b)
    d = functools.partial(jnp.dot, preferred_element_type=F32)
    return d(a_bf16, hi) + d(a_bf16, mid) + d(a_bf16, lo)


def _qk(q, k):
    return lax.dot_general(q, k, (((1,), (1,)), ((), ())), preferred_element_type=F32)


def _log_sigmoid(x):
    return jnp.minimum(x, 0.0) - jnp.log(1.0 + jnp.exp(-jnp.abs(x)))


def _lane_half(shape):
    return lax.broadcasted_iota(jnp.int32, shape, len(shape) - 1) < HEAD_DIM


def _ffn_kernel(x_ref, g_ref, wi_ref, wo_ref, o_ref):
    x = x_ref[...]
    h = _rms(x, g_ref[...]).astype(BF16)
    acc = jnp.zeros(x.shape, F32)
    for c in range(FFN_HIDDEN // FFN_CHUNK):
        lo = c * FFN_CHUNK
        gate = jnp.dot(h, wi_ref[:, lo:lo + FFN_CHUNK], preferred_element_type=F32)
        up = jnp.dot(h, wi_ref[:, FFN_HIDDEN + lo:FFN_HIDDEN + lo + FFN_CHUNK],
                     preferred_element_type=F32)
        a = (gate * jax.nn.sigmoid(gate) * up).astype(BF16)
        acc = acc + jnp.dot(a, wo_ref[c * FFN_CHUNK:(c + 1) * FFN_CHUNK, :],
                            preferred_element_type=F32)
    o_ref[...] = x + 0.5 * acc


def _ffn(x2, gain, w_i, w_o):
    n = x2.shape[0]
    const = lambda i: (0, 0)
    return pl.pallas_call(
        _ffn_kernel,
        grid=(n // TOK_TILE,),
        in_specs=[
            pl.BlockSpec((TOK_TILE, D_MODEL), lambda i: (i, 0)),
            pl.BlockSpec((1, D_MODEL), const),
            pl.BlockSpec((D_MODEL, 2 * FFN_HIDDEN), const, pipeline_mode=pl.Buffered(1)),
            pl.BlockSpec((FFN_HIDDEN, D_MODEL), const, pipeline_mode=pl.Buffered(1)),
        ],
        out_specs=pl.BlockSpec((TOK_TILE, D_MODEL), lambda i: (i, 0)),
        out_shape=jax.ShapeDtypeStruct((n, D_MODEL), F32),
        compiler_params=_cparams(1),
        name="ffn",
    )(x2, gain.reshape(1, D_MODEL), w_i.astype(BF16), w_o.astype(BF16))


_ABC_SECTIONS = ((A_W, True), (A_W, True), (A_W, False),
                 (B_W, False), (B_W, False), (B_W, False),
                 (C_W, True), (C_W, True), (C_W, False))
_D_SECTIONS = ((D_W, True), (D_W, True), (D_W, False))


def _proj_sections(h, w_ref, gain_ref, gmat, out_ref, sections):
    off = 0
    for width, normed in sections:
        y = jnp.dot(h, w_ref[:, off:off + width], preferred_element_type=F32)
        for j in range(width // LANES):
            yb = y[:, j * LANES:(j + 1) * LANES]
            gain = gain_ref[:, off + j * LANES:off + (j + 1) * LANES]
            if normed:
                ms = _dot_exact_rhs(yb * yb, gmat)
                yb = yb * lax.rsqrt(ms + RMS_EPS)
            out_ref[:, off + j * LANES:off + (j + 1) * LANES] = (yb * gain).astype(out_ref.dtype)
        off += width


def _proj_kernel(x_ref, g_ref, wabc_ref, wd_ref, wcf_ref, gabc_ref, gd_ref, gmat_ref,
                 pabc_ref, pd_ref, cf_ref):
    h = _rms(x_ref[...], g_ref[...]).astype(BF16)
    gmat = gmat_ref[...]
    _proj_sections(h, wabc_ref, gabc_ref, gmat, pabc_ref, _ABC_SECTIONS)
    _proj_sections(h, wd_ref, gd_ref, gmat, pd_ref, _D_SECTIONS)
    cf_ref[...] = jnp.dot(h, wcf_ref[...], preferred_element_type=F32)


def _head_gain(gain, width, scale):
    return jnp.tile(gain.astype(F32), width // HEAD_DIM) * scale


def _proj(x2, gain, w_in, a_q, a_k, c_q, c_k, d_q, d_k):
    n = x2.shape[0]
    cf_off = ABC_W
    d_off = ABC_W + C_HEADS
    w_abc = w_in[:, :ABC_W].astype(BF16)
    w_d = w_in[:, d_off:d_off + 3 * D_W].astype(BF16)
    w_cf = jnp.pad(w_in[:, cf_off:cf_off + C_HEADS], ((0, 0), (0, LANES - C_HEADS))).astype(BF16)
    ones = lambda w, s: jnp.full((w,), s, F32)
    g_abc = jnp.concatenate([
        _head_gain(a_q, A_W, SCALE * LOG2E), _head_gain(a_k, A_W, 1.0), ones(A_W, 1.0),
        ones(B_W, SCALE), ones(B_W, 1.0), ones(B_W, 1.0),
        _head_gain(c_q, C_W, SCALE * LOG2E), _head_gain(c_k, C_W, 1.0),
        ones(C_W, 1.0)]).reshape(1, ABC_W)
    g_d = jnp.concatenate([_head_gain(d_q, D_W, SCALE), _head_gain(d_k, D_W, 1.0),
                           ones(D_W, 1.0)]).reshape(1, 3 * D_W)
    lane = np.arange(LANES)
    gmat = jnp.asarray((lane[:, None] // HEAD_DIM == lane[None, :] // HEAD_DIM) / HEAD_DIM, BF16)
    const = lambda i: (0, 0)
    row = lambda i: (i, 0)
    return pl.pallas_call(
        _proj_kernel,
        grid=(n // TOK_TILE,),
        in_specs=[
            pl.BlockSpec((TOK_TILE, D_MODEL), row),
            pl.BlockSpec((1, D_MODEL), const),
            pl.BlockSpec((D_MODEL, ABC_W), const, pipeline_mode=pl.Buffered(1)),
            pl.BlockSpec((D_MODEL, 3 * D_W), const, pipeline_mode=pl.Buffered(1)),
            pl.BlockSpec((D_MODEL, LANES), const),
            pl.BlockSpec((1, ABC_W), const),
            pl.BlockSpec((1, 3 * D_W), const),
            pl.BlockSpec((LANES, LANES), const),
        ],
        out_specs=[
            pl.BlockSpec((TOK_TILE, ABC_W), row),
            pl.BlockSpec((TOK_TILE, 3 * D_W), row),
            pl.BlockSpec((TOK_TILE, LANES), row),
        ],
        out_shape=[
            jax.ShapeDtypeStruct((n, ABC_W), BF16),
            jax.ShapeDtypeStruct((n, 3 * D_W), F32),
            jax.ShapeDtypeStruct((n, LANES), F32),
        ],
        compiler_params=_cparams(1),
        name="proj",
    )(x2, gain.reshape(1, D_MODEL), w_abc, w_d, w_cf, g_abc, g_d, gmat)


_MASKED_BUCKET = REL_BUCKETS


def _rel_bucket_np(n):
    n = np.asarray(n, np.int64)
    max_exact = REL_BUCKETS // 2
    nf = np.maximum(n, 1).astype(np.float32)
    large = max_exact + (np.log(nf / np.float32(max_exact)) / np.float32(math.log(REL_MAX_DIST / max_exact))
                         * np.float32(REL_BUCKETS - max_exact)).astype(np.int32)
    large = np.minimum(large, REL_BUCKETS - 1)
    return np.where(n < max_exact, n, large).astype(np.int32)


def _bucket_maps():
    t = ATT_T
    x, y = np.arange(t)[:, None], np.arange(t)[None, :]
    diag = np.where(x >= y, _rel_bucket_np(np.maximum(x - y, 0)), _MASKED_BUCKET)
    prev = _rel_bucket_np(t + x - y)
    idx_a = np.stack([prev, diag]).astype(np.int32)
    x, y = np.arange(D_T)[:, None], np.arange(D_T)[None, :]
    idx_d = np.empty((D_GROUPS, 2, D_T, 2 * D_T), np.int32)
    for g, (_, dil) in enumerate(D_PAIRS):
        prev = np.where(y >= x, _rel_bucket_np(dil * (D_T + x - y)), _MASKED_BUCKET)
        diag = np.where(y <= x, _rel_bucket_np(dil * np.maximum(x - y, 0)), _MASKED_BUCKET)
        idx_d[g, 0] = np.concatenate([prev, diag], axis=1)
        idx_d[g, 1] = np.concatenate([np.full_like(prev, _MASKED_BUCKET), diag], axis=1)
    return idx_a, idx_d


def _bias_kernel(tbl_ref, idxa_ref, idxd_ref, ba_ref, bd_ref):
    for h in range(A_HEADS):
        for kind in range(2):
            ba_ref[h, kind] = jnp.where(idxa_ref[kind] == _MASKED_BUCKET, NEG_INF, 0.0)
        ba_ref[h, 2] = jnp.zeros((ATT_T, ATT_T), F32)
        ba_ref[h, 3] = jnp.full((ATT_T, ATT_T), NEG_INF, F32)
    for g in range(D_GROUPS):
        for h in range(2):
            for var in range(2):
                bd_ref[g, h, var] = jnp.where(idxd_ref[g, var] == _MASKED_BUCKET, NEG_INF, 0.0)

    def body(b, carry):
        for h in range(A_HEADS):
            val = (tbl_ref[b, h] - tbl_ref[REL_BUCKETS - 1, h]) * LOG2E
            for kind in range(2):
                ba_ref[h, kind] = jnp.where(idxa_ref[kind] == b, val, ba_ref[h, kind])
        for g in range(D_GROUPS):
            for h in range(2):
                val = tbl_ref[b, A_HEADS + 2 * g + h]
                for var in range(2):
                    bd_ref[g, h, var] = jnp.where(idxd_ref[g, var] == b, val, bd_ref[g, h, var])
        return carry

    lax.fori_loop(0, REL_BUCKETS, body, 0)


def _bias_tiles(rel_table):
    idx_a, idx_d = _bucket_maps()
    return pl.pallas_call(
        _bias_kernel,
        in_specs=[pl.BlockSpec(memory_space=pltpu.SMEM),
                  pl.BlockSpec(memory_space=pltpu.VMEM),
                  pl.BlockSpec(memory_space=pltpu.VMEM)],
        out_specs=[pl.BlockSpec(memory_space=pltpu.VMEM),
                   pl.BlockSpec(memory_space=pltpu.VMEM)],
        out_shape=[jax.ShapeDtypeStruct((A_HEADS, 4, ATT_T, ATT_T), F32),
                   jax.ShapeDtypeStruct((D_GROUPS, 2, 2, D_T, 2 * D_T), F32)],
        compiler_params=pltpu.CompilerParams(vmem_limit_bytes=VMEM_LIMIT),
        name="rel_bias",
    )(rel_table.astype(F32), jnp.asarray(idx_a), jnp.asarray(idx_d))


def _rows(idx, size):
    return pl.ds(pl.multiple_of(idx * size, size), size)


def _softmax_stages(scores, ms, acc_ref, vs, ls=None):
    n = len(scores)
    m_new = [jnp.maximum(ms[c], jnp.max(scores[c], axis=-1, keepdims=True)) for c in range(n)]
    alpha = [jnp.exp2(ms[c] - m_new[c]) for c in range(n)]
    p = [jnp.exp2(scores[c] - m_new[c]) for c in range(n)]
    pv = [jnp.dot(p[c].astype(BF16), vs[c], preferred_element_type=F32) for c in range(n)]
    for c in range(n):
        acc_ref[c] = alpha[c] * acc_ref[c] + pv[c]
    if ls is None:
        return m_new, None
    return m_new, [alpha[c] * ls[c] + jnp.sum(p[c], axis=-1, keepdims=True) for c in range(n)]


def _split_heads(q):
    first = _lane_half((q.shape[0], LANES))
    zero = jnp.zeros((q.shape[0], LANES), q.dtype)
    out = []
    for p in range(q.shape[1] // LANES):
        qp = q[:, p * LANES:(p + 1) * LANES]
        out += [jnp.where(first, qp, zero), jnp.where(first, zero, qp)]
    return out


def _attn_specs(width, col0, s):
    nq = s // ATT_T
    q_spec = pl.BlockSpec((ATT_T, width), lambda b, i: (b * nq + i, col0))
    k_spec = pl.BlockSpec((s, width), lambda b, i: (b, col0 + 1))
    v_spec = pl.BlockSpec((s, width), lambda b, i: (b, col0 + 2))
    o_spec = pl.BlockSpec((ATT_T, width), lambda b, i: (b * nq + i, 0))
    return q_spec, k_spec, v_spec, o_spec


def _window_split(qi):
    return lax.shift_right_logical(qi, 1), jnp.bitwise_and(qi, 1)


def _attn_a_kernel(q_ref, k_ref, v_ref, bias_ref, lam_ref, subln_ref, o_ref, acc_ref, *, lam_init):
    qi = pl.program_id(1)
    qs = _split_heads(q_ref[...])
    n_chain = len(qs)
    acc_ref[...] = jnp.zeros(acc_ref.shape, F32)
    col = lambda val: jnp.full((ATT_T, 1), val, F32)
    init = (col(NEG_INF), col(0.0)) * n_chain
    last, odd = _window_split(qi)

    def window(j, carry, kinds):
        rows = _rows(j, ATT_WIN)
        scores, vs = [], []
        for h in range(A_HEADS):
            cols = slice(h * LANES, (h + 1) * LANES)
            k, v = k_ref[rows, cols], v_ref[rows, cols]
            if kinds is not None:
                bias = jnp.concatenate([bias_ref[h, kinds[0]], bias_ref[h, kinds[1]]], axis=1)
            for c in (2 * h, 2 * h + 1):
                s = _qk(qs[c], k)
                scores.append(s if kinds is None else s + bias)
                vs.append(v)
        ms, ls = _softmax_stages(scores, carry[0::2], acc_ref, vs, ls=carry[1::2])
        return tuple(x for pair in zip(ms, ls) for x in pair)

    n_plain = jnp.maximum(last - 1 + odd, 0)
    carry = lax.fori_loop(0, n_plain, lambda j, c: window(j, c, None), init)

    def near(j, c):
        left = jnp.where(odd == 1, 0, jnp.where(j == last, 1, 2))
        right = jnp.where(odd == 1, 1, jnp.where(j == last, 3, 0))
        return window(j, c, (left, right))

    carry = lax.fori_loop(n_plain, last + 1, near, carry)

    lv = lam_ref[...]
    lam = (jnp.exp(jnp.sum(lv[0:1] * lv[1:2], axis=-1, keepdims=True))
           - jnp.exp(jnp.sum(lv[2:3] * lv[3:4], axis=-1, keepdims=True)) + lam_init)
    for h in range(A_HEADS):
        l0, l1 = carry[4 * h + 1], carry[4 * h + 3]
        o = acc_ref[2 * h] / l0 - lam * (acc_ref[2 * h + 1] / l1)
        o_ref[:, h * LANES:(h + 1) * LANES] = (
            _rms(o, subln_ref[...]) * (1.0 - lam_init)).astype(o_ref.dtype)


def _attn_a(p_abc, bias_a, lam_vecs, subln, lam_init, batch, s):
    n = p_abc.shape[0]
    q_spec, k_spec, v_spec, o_spec = _attn_specs(A_W, 0, s)
    return pl.pallas_call(
        functools.partial(_attn_a_kernel, lam_init=lam_init),
        grid=(batch, s // ATT_T),
        in_specs=[q_spec, k_spec, v_spec,
                  pl.BlockSpec(bias_a.shape, lambda b, i: (0, 0, 0, 0)),
                  pl.BlockSpec((4, HEAD_DIM), lambda b, i: (0, 0)),
                  pl.BlockSpec((1, 2 * HEAD_DIM), lambda b, i: (0, 0))],
        out_specs=o_spec,
        out_shape=jax.ShapeDtypeStruct((n, A_W), BF16),
        scratch_shapes=[pltpu.VMEM((2 * A_HEADS, ATT_T, LANES), F32)],
        compiler_params=_cparams(2),
        name="attn_diff",
    )(p_abc, p_abc, p_abc, bias_a, lam_vecs.astype(F32), subln.reshape(1, 2 * HEAD_DIM).astype(F32))


def _attn_b_kernel(q_ref, k_ref, v_ref, o_ref, acc_ref):
    qi = pl.program_id(1)
    qs = _split_heads(q_ref[...])
    acc_ref[...] = jnp.zeros(acc_ref.shape, F32)
    row = lax.broadcasted_iota(jnp.int32, (ATT_T, ATT_T), 0)
    colm = lax.broadcasted_iota(jnp.int32, (ATT_T, ATT_T), 1)
    later = jnp.where(row > colm, 1.0, 0.0).astype(BF16)
    strict = colm < row

    def step(kj, rsum, diag):
        rows = _rows(kj, ATT_T)
        heads = range(B_HEADS)
        cols = [slice((h // 2) * LANES, (h // 2 + 1) * LANES) for h in heads]
        z = [_qk(qs[h], k_ref[rows, cols[h]]) for h in heads]
        ls = [_log_sigmoid(z[h]) for h in heads]
        u = [ls[h] - z[h] for h in heads]
        if diag:
            u = [jnp.where(strict, u[h], 0.0) for h in heads]
        u_hi = [u[h].astype(BF16) for h in heads]
        u_lo = [(u[h] - u_hi[h].astype(F32)).astype(BF16) for h in heads]
        tail = [jnp.dot(u_hi[h], later, preferred_element_type=F32)
                + jnp.dot(u_lo[h], later, preferred_element_type=F32) + rsum[h] for h in heads]
        a = [jnp.exp(ls[h] + tail[h]) for h in heads]
        if diag:
            a = [jnp.where(strict, a[h], 0.0) for h in heads]
        pv = [jnp.dot(a[h].astype(BF16), v_ref[rows, cols[h]], preferred_element_type=F32)
              for h in heads]
        for h in heads:
            acc_ref[h] = acc_ref[h] + pv[h]
        return tuple(rsum[h] + jnp.sum(u[h], axis=-1, keepdims=True) for h in heads)

    zero = jnp.zeros((ATT_T, 1), F32)
    rsum = step(qi, (zero,) * B_HEADS, True)
    lax.fori_loop(0, qi, lambda t, c: step(qi - 1 - t, c, False), rsum)
    first = _lane_half((ATT_T, LANES))
    for p in range(B_HEADS // 2):
        o_ref[:, p * LANES:(p + 1) * LANES] = jnp.where(
            first, acc_ref[2 * p], acc_ref[2 * p + 1]).astype(o_ref.dtype)


def _attn_b(p_abc, batch, s):
    n = p_abc.shape[0]
    q_spec, k_spec, v_spec, o_spec = _attn_specs(B_W, 3 * A_W // B_W, s)
    return pl.pallas_call(
        _attn_b_kernel,
        grid=(batch, s // ATT_T),
        in_specs=[q_spec, k_spec, v_spec],
        out_specs=o_spec,
        out_shape=jax.ShapeDtypeStruct((n, B_W), BF16),
        scratch_shapes=[pltpu.VMEM((B_HEADS, ATT_T, LANES), F32)],
        compiler_params=_cparams(2),
        name="attn_stick",
    )(p_abc, p_abc, p_abc)


def _attn_c_kernel(q_ref, k_ref, v_ref, cf_ref, fb_ref, o_ref, acc_ref, cumc_ref, cumr_ref, vaug_ref):
    qi = pl.program_id(1)
    s_len = cf_ref.shape[0]

    @pl.when(qi == 0)
    def _():
        r = lax.broadcasted_iota(jnp.int32, (LANES, LANES), 0)
        c = lax.broadcasted_iota(jnp.int32, (LANES, LANES), 1)
        lower = jnp.where(c <= r, 1.0, 0.0).astype(BF16)
        carry = jnp.zeros((1, LANES), F32)
        for blk in range(s_len // LANES):
            rows = slice(blk * LANES, (blk + 1) * LANES)
            log_f = _log_sigmoid(cf_ref[rows, :] + fb_ref[...])
            cum = _dot_exact_lhs(lower, log_f) + carry
            cumc_ref[rows, :] = cum * LOG2E
            cumr_ref[:, rows] = (cum * LOG2E).T
            carry = cum[LANES - 1:LANES, :]
        v = v_ref[...]
        first = jnp.bitwise_and(lax.broadcasted_iota(jnp.int32, v.shape, 1), LANES - 1) < HEAD_DIM
        one = jnp.ones(v.shape, v.dtype)
        vaug_ref[0] = jnp.where(first, v, one)
        vaug_ref[1] = jnp.where(first, one, v)

    qs = _split_heads(q_ref[...])
    acc_ref[...] = jnp.zeros(acc_ref.shape, F32)
    cum_q_tile = cumc_ref[_rows(qi, ATT_T), :]
    lane = lax.broadcasted_iota(jnp.int32, (ATT_T, LANES), 1)
    cum_q = [jnp.sum(jnp.where(lane == h, cum_q_tile, 0.0), axis=-1, keepdims=True)
             for h in range(C_HEADS)]
    last, odd = _window_split(qi)
    shape = (ATT_T, ATT_WIN)
    causal = jnp.where(lax.broadcasted_iota(jnp.int32, shape, 1) - lax.broadcasted_iota(jnp.int32, shape, 0)
                       <= odd * ATT_T, 0.0, NEG_INF)

    def window(j, carry, masked):
        rows = _rows(j, ATT_WIN)
        scores, vs = [], []
        for h in range(C_HEADS):
            cols = slice((h // 2) * LANES, (h // 2 + 1) * LANES)
            s = _qk(qs[h], k_ref[rows, cols]) + cum_q[h] - cumr_ref[h:h + 1, rows]
            scores.append(s + causal if masked else s)
            vs.append(vaug_ref[h % 2, rows, cols])
        return tuple(_softmax_stages(scores, carry, acc_ref, vs)[0])

    init = (jnp.full((ATT_T, 1), NEG_INF, F32),) * C_HEADS
    carry = lax.fori_loop(0, last, lambda j, c: window(j, c, False), init)
    window(last, carry, True)
    first = _lane_half((ATT_T, LANES))
    for p in range(C_HEADS // 2):
        a0, a1 = acc_ref[2 * p], acc_ref[2 * p + 1]
        num = jnp.where(first, a0, a1)
        den = jnp.where(first, pltpu.roll(a0, HEAD_DIM, 1), pltpu.roll(a1, HEAD_DIM, 1))
        o_ref[:, p * LANES:(p + 1) * LANES] = (num / den).astype(o_ref.dtype)


def _attn_c(p_abc, cf, forget_bias, batch, s):
    n = p_abc.shape[0]
    q_spec, k_spec, v_spec, o_spec = _attn_specs(C_W, 3 * (A_W + B_W) // C_W, s)
    fb = jnp.pad(forget_bias.astype(F32), (0, LANES - C_HEADS)).reshape(1, LANES)
    return pl.pallas_call(
        _attn_c_kernel,
        grid=(batch, s // ATT_T),
        in_specs=[q_spec, k_spec, v_spec,
                  pl.BlockSpec((s, LANES), lambda b, i: (b, 0)),
                  pl.BlockSpec((1, LANES), lambda b, i: (0, 0))],
        out_specs=o_spec,
        out_shape=jax.ShapeDtypeStruct((n, C_W), BF16),
        scratch_shapes=[pltpu.VMEM((C_HEADS, ATT_T, LANES), F32),
                        pltpu.VMEM((s, LANES), F32),
                        pltpu.VMEM((LANES, s), F32),
                        pltpu.VMEM((2, s, C_W), BF16)],
        compiler_params=_cparams(2),
        name="attn_forget",
    )(p_abc, p_abc, p_abc, cf, fb)


def _attn_d_kernel(*refs):
    qkv_refs, (bias_ref, o_ref, qc_ref, kc_ref, vc_ref, oc_ref, lc_ref, og_ref, lse_ref) = (
        refs[:3 * D_GROUPS], refs[3 * D_GROUPS:])
    s_len = o_ref.shape[0]
    n_tiles = s_len // D_T
    first = _lane_half((D_T, LANES))

    for g, (_, dil) in enumerate(D_PAIRS):
        q_ref, k_ref, v_ref = qkv_refs[g], qkv_refs[D_GROUPS + g], qkv_refs[2 * D_GROUPS + g]
        class_len = s_len // dil
        tiles_per_class = class_len // D_T

        for r in range(dil):
            src = pl.ds(r, class_len, stride=dil) if dil > 1 else slice(None)
            dst = slice(r * class_len, (r + 1) * class_len)
            qc_ref[dst, :] = q_ref[src, :].astype(BF16)
            kc_ref[dst, :] = k_ref[src, :].astype(BF16)
            vc_ref[dst, :] = v_ref[src, :].astype(BF16)

        def rows(t):
            return pl.ds(pl.multiple_of(t * D_T, D_T), D_T)

        def tile(t, carry):
            q = qc_ref[rows(t), :]
            k_diag = kc_ref[rows(t), :]
            v_diag = vc_ref[rows(t), :]
            if tiles_per_class > 1:
                t_prev = jnp.maximum(t - 1, 0)
                k_prev = kc_ref[rows(t_prev), :]
                v_prev = vc_ref[rows(t_prev), :]
                variant = jnp.where(t % tiles_per_class == 0, 1, 0)
            o_heads, lse_heads = [], []
            for h, qh in enumerate(_split_heads(q)):
                if tiles_per_class > 1:
                    bias = bias_ref[g, h, variant]
                    s_prev = _qk(qh, k_prev) + bias[:, :D_T]
                    s_diag = _qk(qh, k_diag) + bias[:, D_T:]
                    m = jnp.maximum(jnp.max(s_prev, axis=-1, keepdims=True),
                                    jnp.max(s_diag, axis=-1, keepdims=True))
                    e_prev, e_diag = jnp.exp(s_prev - m), jnp.exp(s_diag - m)
                    l = (jnp.sum(e_prev, axis=-1, keepdims=True)
                         + jnp.sum(e_diag, axis=-1, keepdims=True))
                    o = (jnp.dot(e_prev.astype(BF16), v_prev, preferred_element_type=F32)
                         + jnp.dot(e_diag.astype(BF16), v_diag, preferred_element_type=F32))
                else:
                    s_diag = _qk(qh, k_diag) + bias_ref[g, h, 1][:, D_T:]
                    m = jnp.max(s_diag, axis=-1, keepdims=True)
                    e_diag = jnp.exp(s_diag - m)
                    l = jnp.sum(e_diag, axis=-1, keepdims=True)
                    o = jnp.dot(e_diag.astype(BF16), v_diag, preferred_element_type=F32)
                o_heads.append(o / l)
                lse_heads.append(jnp.broadcast_to(m + jnp.log(l), (D_T, LANES)))
            oc_ref[rows(t), :] = jnp.where(first, o_heads[0], o_heads[1])
            lc_ref[rows(t), :] = jnp.where(first, lse_heads[0], lse_heads[1])
            return carry

        lax.fori_loop(0, n_tiles, tile, 0)

        for r in range(dil):
            dst = pl.ds(r, class_len, stride=dil) if dil > 1 else slice(None)
            src = slice(r * class_len, (r + 1) * class_len)
            og_ref[g, dst, :] = oc_ref[src, :]
            lse_ref[g, dst, :] = lc_ref[src, :]

    lse = [lse_ref[g] for g in range(D_GROUPS)]
    top = functools.reduce(jnp.maximum, lse)
    w = [jnp.exp(x - top) for x in lse]
    num = sum(w[g] * og_ref[g] for g in range(D_GROUPS))
    o_ref[...] = (num / sum(w)).astype(o_ref.dtype)


def _attn_d(p_d, bias_d, batch, s):
    n = p_d.shape[0]
    return pl.pallas_call(
        _attn_d_kernel,
        grid=(batch,),
        in_specs=([pl.BlockSpec((s, LANES), functools.partial(lambda b, c: (b, c), c=c))
                   for c in range(3 * D_GROUPS)]
                  + [pl.BlockSpec(bias_d.shape, lambda b: (0, 0, 0, 0, 0))]),
        out_specs=pl.BlockSpec((s, LANES), lambda b: (b, 0)),
        out_shape=jax.ShapeDtypeStruct((n, LANES), BF16),
        scratch_shapes=[pltpu.VMEM((s, LANES), BF16), pltpu.VMEM((s, LANES), BF16),
                        pltpu.VMEM((s, LANES), BF16),
                        pltpu.VMEM((s, LANES), F32), pltpu.VMEM((s, LANES), F32),
                        pltpu.VMEM((D_GROUPS, s, LANES), F32),
                        pltpu.VMEM((D_GROUPS, s, LANES), F32)],
        compiler_params=_cparams(1),
        name="attn_dilated",
    )(*([p_d] * (3 * D_GROUPS)), bias_d)


def _merge_kernel(x_ref, g_ref, wg_ref, gb_ref, oa_ref, ob_ref, oc_ref, od_ref,
                  wa_ref, wb_ref, wc_ref, wd_ref, wout_ref, o_ref):
    x = x_ref[...]
    h = _rms(x, g_ref[...]).astype(BF16)
    merged = jnp.zeros(x.shape, F32)
    branches = ((oa_ref, wa_ref), (ob_ref, wb_ref), (oc_ref, wc_ref), (od_ref, wd_ref))
    for i, (br_ref, w_ref) in enumerate(branches):
        cols = slice(i * D_MODEL, (i + 1) * D_MODEL)
        logits = jnp.dot(h, wg_ref[:, cols], preferred_element_type=F32) + gb_ref[:, cols]
        branch = jnp.dot(br_ref[...], w_ref[...], preferred_element_type=F32)
        merged = merged + jax.nn.sigmoid(logits) * branch
    o_ref[...] = x + jnp.dot(merged.astype(BF16), wout_ref[...], preferred_element_type=F32)


def _merge(x2, gain, w_gate, gate_bias, outs, w_branch, w_out):
    n = x2.shape[0]
    widths = (A_W, B_W, C_W, 2 * HEAD_DIM)
    offs = np.concatenate([[0], np.cumsum(widths)])
    w_br = [w_branch[offs[i]:offs[i + 1]].astype(BF16) for i in range(N_BRANCH)]
    const = lambda i: (0, 0)
    row = lambda i: (i, 0)
    return pl.pallas_call(
        _merge_kernel,
        grid=(n // TOK_TILE,),
        in_specs=([pl.BlockSpec((TOK_TILE, D_MODEL), row),
                   pl.BlockSpec((1, D_MODEL), const),
                   pl.BlockSpec((D_MODEL, N_BRANCH * D_MODEL), const, pipeline_mode=pl.Buffered(1)),
                   pl.BlockSpec((1, N_BRANCH * D_MODEL), const)]
                  + [pl.BlockSpec((TOK_TILE, w), row) for w in widths]
                  + [pl.BlockSpec((w, D_MODEL), const) for w in widths]
                  + [pl.BlockSpec((D_MODEL, D_MODEL), const)]),
        out_specs=pl.BlockSpec((TOK_TILE, D_MODEL), row),
        out_shape=jax.ShapeDtypeStruct((n, D_MODEL), F32),
        compiler_params=_cparams(1),
        name="merge",
    )(x2, gain.reshape(1, D_MODEL), w_gate.astype(BF16), gate_bias.reshape(1, -1).astype(F32),
      *outs, *w_br, w_out.astype(BF16))


def kernel(x, rel_table, ffn1_norm, ffn1_w_in, ffn1_w_out, mix_norm, w_in, gate_bias, forget_bias,
           a_q_norm, a_k_norm, a_lambda, a_subln, c_q_norm, c_k_norm, d_q_norm, d_k_norm,
           w_branch, w_out, ffn2_norm, ffn2_w_in, ffn2_w_out):
    batch, s, _ = x.shape
    depth = w_in.shape[0]
    assert s % (D_T * D_PAIRS[-1][1]) == 0 and (batch * s) % TOK_TILE == 0
    gate_off = ABC_W + C_HEADS + 3 * D_W
    bias_a, bias_d = _bias_tiles(rel_table)
    x2 = x.reshape(batch * s, D_MODEL)
    for l in range(depth):
        x2 = _ffn(x2, ffn1_norm[l], ffn1_w_in[l], ffn1_w_out[l])
        p_abc, p_d, cf = _proj(x2, mix_norm[l], w_in[l], a_q_norm[l], a_k_norm[l],
                               c_q_norm[l], c_k_norm[l], d_q_norm[l], d_k_norm[l])
        lam_init = 0.8 - 0.6 * math.exp(-0.3 * l)
        oa = _attn_a(p_abc, bias_a, a_lambda[l], a_subln[l], lam_init, batch, s)
        ob = _attn_b(p_abc, batch, s)
        oc = _attn_c(p_abc, cf, forget_bias[l], batch, s)
        od = _attn_d(p_d, bias_d, batch, s)
        x2 = _merge(x2, mix_norm[l], w_in[l][:, gate_off:], gate_bias[l], (oa, ob, oc, od),
                    w_branch[l], w_out[l])
        x2 = _ffn(x2, ffn2_norm[l], ffn2_w_in[l], ffn2_w_out[l])
    return x2.reshape(batch, s, D_MODEL)
```

```python
import functools
import math

import numpy as np
import jax
import jax.numpy as jnp
from jax import lax
from jax.experimental import pallas as pl
from jax.experimental.pallas import tpu as pltpu

F32 = jnp.float32
BF16 = jnp.bfloat16

D_MODEL = 1024
HEAD_DIM = 64
SCALE = HEAD_DIM ** -0.5
A_HEADS = 4
B_HEADS = 6
C_HEADS = 6
D_PAIRS = ((128, 1), (512, 4), (2048, 16))
D_GROUPS = len(D_PAIRS)
N_BRANCH = 4
FFN_HIDDEN = 2816
REL_BUCKETS = 32
REL_MAX_DIST = 128
RMS_EPS = 1e-6
NEG_INF = -1e30

LANES = 128
A_W = A_HEADS * 2 * HEAD_DIM
B_W = B_HEADS * HEAD_DIM
C_W = C_HEADS * HEAD_DIM
D_W = D_GROUPS * 2 * HEAD_DIM
ABC_W = 3 * (A_W + B_W + C_W)

W_SHIFT = (ABC_W + C_HEADS) % LANES
W_ALIGNED_BLOCKS = ABC_W // LANES + 1
W_D_OFF = W_ALIGNED_BLOCKS * LANES
W_GATE_OFF = W_D_OFF + 3 * D_W
GROUP_W = 256

TOK_TILE = 512
FFN_CHUNK = 256
ATT_T = 256
ATT_WIN = 2 * ATT_T
LOG2E = math.log2(math.e)
D_UNROLL = 4
D_T = 128
VMEM_LIMIT = 56 * 1024 * 1024


def _cparams(n_axes):
    return pltpu.CompilerParams(dimension_semantics=("arbitrary",) * n_axes,
                                vmem_limit_bytes=VMEM_LIMIT)


def _rms(x, g):
    ms = jnp.mean(x * x, axis=-1, keepdims=True)
    return x * lax.rsqrt(ms + RMS_EPS) * g


def _split3(a):
    hi = a.astype(BF16)
    r1 = a - hi.astype(F32)
    mid = r1.astype(BF16)
    lo = (r1 - mid.astype(F32)).astype(BF16)
    return hi, mid, lo


def _dot_split2_rhs(a, b_bf16):
    hi = a.astype(BF16)
    lo = (a - hi.astype(F32)).astype(BF16)
    d = functools.partial(jnp.dot, preferred_element_type=F32)
    return d(hi, b_bf16) + d(lo, b_bf16)


def _dot_exact_lhs(a_bf16, b):
    hi, mid, lo = _split3(b)
    d = functools.partial(jnp.dot, preferred_element_type=F32)
    return d(a_bf16, hi) + d(a_bf16, mid) + d(a_bf16, lo)


def _qk(q, k):
    return lax.dot_general(q, k, (((1,), (1,)), ((), ())), preferred_element_type=F32)


def _log_sigmoid(x):
    return jnp.minimum(x, 0.0) - jnp.log(1.0 + jnp.exp(-jnp.abs(x)))


def _lane_half(shape):
    return lax.broadcasted_iota(jnp.int32, shape, len(shape) - 1) < HEAD_DIM


def _ffn_kernel(x_ref, g_ref, wi_ref, wo_ref, o_ref):
    x = x_ref[...]
    h = _rms(x, g_ref[...]).astype(BF16)
    acc = jnp.zeros(x.shape, F32)
    for c in range(FFN_HIDDEN // FFN_CHUNK):
        lo = c * FFN_CHUNK
        gate = jnp.dot(h, wi_ref[:, lo:lo + FFN_CHUNK], preferred_element_type=F32)
        up = jnp.dot(h, wi_ref[:, FFN_HIDDEN + lo:FFN_HIDDEN + lo + FFN_CHUNK],
                     preferred_element_type=F32)
        a = (gate * jax.nn.sigmoid(gate) * up).astype(BF16)
        acc = acc + jnp.dot(a, wo_ref[c * FFN_CHUNK:(c + 1) * FFN_CHUNK, :],
                            preferred_element_type=F32)
    o_ref[...] = x + 0.5 * acc


def _ffn(x2, gain, w_i, w_o):
    n = x2.shape[0]
    const = lambda i: (0, 0)
    return pl.pallas_call(
        _ffn_kernel,
        grid=(n // TOK_TILE,),
        in_specs=[
            pl.BlockSpec((TOK_TILE, D_MODEL), lambda i: (i, 0)),
            pl.BlockSpec((1, D_MODEL), const),
            pl.BlockSpec((D_MODEL, 2 * FFN_HIDDEN), const, pipeline_mode=pl.Buffered(1)),
            pl.BlockSpec((FFN_HIDDEN, D_MODEL), const, pipeline_mode=pl.Buffered(1)),
        ],
        out_specs=pl.BlockSpec((TOK_TILE, D_MODEL), lambda i: (i, 0)),
        out_shape=jax.ShapeDtypeStruct((n, D_MODEL), F32),
        compiler_params=_cparams(1),
        name="ffn",
    )(x2, gain.reshape(1, D_MODEL), w_i.astype(BF16), w_o.astype(BF16))


_ABC_SECTIONS = ((A_W, True), (A_W, True), (A_W, False),
                 (B_W, False), (B_W, False), (B_W, False),
                 (C_W, True), (C_W, True), (C_W, False))
_D_SECTIONS = ((D_W, True), (D_W, True), (D_W, False))


def _proj_sections(h, w_ref, w_off, gain_ref, gmat, out_ref, sections):
    off = 0
    for width, normed in sections:
        y = jnp.dot(h, w_ref[:, w_off + off:w_off + off + width], preferred_element_type=F32)
        lo = 0
        while lo < width:
            blk = min(GROUP_W, width - lo)
            yb = y[:, lo:lo + blk]
            if normed:
                ms = _dot_split2_rhs(yb * yb, gmat[:blk, :blk])
                yb = yb * lax.rsqrt(ms + RMS_EPS)
            gain = gain_ref[:, off + lo:off + lo + blk]
            out_ref[:, off + lo:off + lo + blk] = (yb * gain).astype(out_ref.dtype)
            lo += blk
        off += width


def _proj_kernel(x_ref, g_ref, w_ref, gabc_ref, gd_ref, gmat_ref, pabc_ref, pd_ref, cf_ref):
    h = _rms(x_ref[...], g_ref[...]).astype(BF16)
    gmat = gmat_ref[...]
    _proj_sections(h, w_ref, 0, gabc_ref, gmat, pabc_ref, _ABC_SECTIONS)
    _proj_sections(h, w_ref, W_D_OFF, gd_ref, gmat, pd_ref, _D_SECTIONS)
    cf_ref[...] = jnp.dot(h, w_ref[:, ABC_W:ABC_W + LANES], preferred_element_type=F32)


def _head_gain(gain, width, scale):
    return jnp.tile(gain.astype(F32), width // HEAD_DIM) * scale


def _win_kernel(a_ref, b_ref, o_ref):
    c = pl.program_id(1)

    @pl.when(c < W_ALIGNED_BLOCKS)
    def _():
        o_ref[...] = b_ref[...].astype(BF16)

    @pl.when(c >= W_ALIGNED_BLOCKS)
    def _():
        both = jnp.concatenate([a_ref[...], b_ref[...]], axis=1)
        o_ref[...] = both[:, W_SHIFT:W_SHIFT + LANES].astype(BF16)


def _prep_w_in(w_in):
    depth = w_in.shape[0]
    n_blocks = pl.cdiv(w_in.shape[2], LANES)
    return pl.pallas_call(
        _win_kernel,
        grid=(depth, n_blocks),
        in_specs=[pl.BlockSpec((None, D_MODEL, LANES), lambda l, c: (l, 0, jnp.maximum(c - 1, 0))),
                  pl.BlockSpec((None, D_MODEL, LANES), lambda l, c: (l, 0, c))],
        out_specs=pl.BlockSpec((None, D_MODEL, LANES), lambda l, c: (l, 0, c)),
        out_shape=jax.ShapeDtypeStruct((depth, D_MODEL, n_blocks * LANES), BF16),
        compiler_params=_cparams(2),
        name="prep_w_in",
    )(w_in, w_in)


def _proj(x2, gain, w_all, layer, a_q, a_k, c_q, c_k, d_q, d_k):
    n = x2.shape[0]
    ones = lambda w, s: jnp.full((w,), s, F32)
    g_abc = jnp.concatenate([
        _head_gain(a_q, A_W, SCALE * LOG2E), _head_gain(a_k, A_W, 1.0), ones(A_W, 1.0),
        ones(B_W, SCALE * LOG2E), ones(B_W, 1.0), ones(B_W, 1.0),
        _head_gain(c_q, C_W, SCALE * LOG2E), _head_gain(c_k, C_W, 1.0),
        ones(C_W, 1.0)]).reshape(1, ABC_W)
    g_d = jnp.concatenate([_head_gain(d_q, D_W, SCALE), _head_gain(d_k, D_W, 1.0),
                           ones(D_W, 1.0)]).reshape(1, 3 * D_W)
    lane = np.arange(GROUP_W)
    gmat = jnp.asarray((lane[:, None] // HEAD_DIM == lane[None, :] // HEAD_DIM) / HEAD_DIM, BF16)
    const = lambda i: (0, 0)
    row = lambda i: (i, 0)
    return pl.pallas_call(
        _proj_kernel,
        grid=(n // TOK_TILE,),
        in_specs=[
            pl.BlockSpec((TOK_TILE, D_MODEL), row),
            pl.BlockSpec((1, D_MODEL), const),
            pl.BlockSpec((None, D_MODEL, W_GATE_OFF), lambda i: (layer, 0, 0),
                         pipeline_mode=pl.Buffered(1)),
            pl.BlockSpec((1, ABC_W), const),
            pl.BlockSpec((1, 3 * D_W), const),
            pl.BlockSpec((GROUP_W, GROUP_W), const),
        ],
        out_specs=[
            pl.BlockSpec((TOK_TILE, ABC_W), row),
            pl.BlockSpec((TOK_TILE, 3 * D_W), row),
            pl.BlockSpec((TOK_TILE, LANES), row),
        ],
        out_shape=[
            jax.ShapeDtypeStruct((n, ABC_W), BF16),
            jax.ShapeDtypeStruct((n, 3 * D_W), F32),
            jax.ShapeDtypeStruct((n, LANES), F32),
        ],
        compiler_params=_cparams(1),
        name="proj",
    )(x2, gain.reshape(1, D_MODEL), w_all, g_abc, g_d, gmat)


_MASKED_BUCKET = REL_BUCKETS


def _rel_bucket_np(n):
    n = np.asarray(n, np.int64)
    max_exact = REL_BUCKETS // 2
    nf = np.maximum(n, 1).astype(np.float32)
    large = max_exact + (np.log(nf / np.float32(max_exact)) / np.float32(math.log(REL_MAX_DIST / max_exact))
                         * np.float32(REL_BUCKETS - max_exact)).astype(np.int32)
    large = np.minimum(large, REL_BUCKETS - 1)
    return np.where(n < max_exact, n, large).astype(np.int32)


def _bucket_maps():
    t = ATT_T
    x, y = np.arange(t)[:, None], np.arange(t)[None, :]
    diag = np.where(x >= y, _rel_bucket_np(np.maximum(x - y, 0)), _MASKED_BUCKET)
    prev = _rel_bucket_np(t + x - y)
    idx_a = np.stack([prev, diag]).astype(np.int32)
    x, y = np.arange(D_T)[:, None], np.arange(D_T)[None, :]
    idx_d = np.empty((D_GROUPS, 2, D_T, 2 * D_T), np.int32)
    for g, (_, dil) in enumerate(D_PAIRS):
        prev = np.where(y >= x, _rel_bucket_np(dil * (D_T + x - y)), _MASKED_BUCKET)
        diag = np.where(y <= x, _rel_bucket_np(dil * np.maximum(x - y, 0)), _MASKED_BUCKET)
        idx_d[g, 0] = np.concatenate([prev, diag], axis=1)
        idx_d[g, 1] = np.concatenate([np.full_like(prev, _MASKED_BUCKET), diag], axis=1)
    return idx_a, idx_d


def _bias_kernel(tbl_ref, idxa_ref, idxd_ref, ba_ref, bd_ref):
    for h in range(A_HEADS):
        for kind in range(2):
            ba_ref[h, kind] = jnp.where(idxa_ref[kind] == _MASKED_BUCKET, NEG_INF, 0.0)
        ba_ref[h, 2] = jnp.zeros((ATT_T, ATT_T), F32)
        ba_ref[h, 3] = jnp.full((ATT_T, ATT_T), NEG_INF, F32)
    for g in range(D_GROUPS):
        for h in range(2):
            for var in range(2):
                bd_ref[g, h, var] = jnp.where(idxd_ref[g, var] == _MASKED_BUCKET, NEG_INF, 0.0)

    def body(b, carry):
        for h in range(A_HEADS):
            val = (tbl_ref[b, h] - tbl_ref[REL_BUCKETS - 1, h]) * LOG2E
            for kind in range(2):
                ba_ref[h, kind] = jnp.where(idxa_ref[kind] == b, val, ba_ref[h, kind])
        for g in range(D_GROUPS):
            for h in range(2):
                val = tbl_ref[b, A_HEADS + 2 * g + h]
                for var in range(2):
                    bd_ref[g, h, var] = jnp.where(idxd_ref[g, var] == b, val, bd_ref[g, h, var])
        return carry

    lax.fori_loop(0, REL_BUCKETS, body, 0)


def _bias_tiles(rel_table):
    idx_a, idx_d = _bucket_maps()
    return pl.pallas_call(
        _bias_kernel,
        in_specs=[pl.BlockSpec(memory_space=pltpu.SMEM),
                  pl.BlockSpec(memory_space=pltpu.VMEM),
                  pl.BlockSpec(memory_space=pltpu.VMEM)],
        out_specs=[pl.BlockSpec(memory_space=pltpu.VMEM),
                   pl.BlockSpec(memory_space=pltpu.VMEM)],
        out_shape=[jax.ShapeDtypeStruct((A_HEADS, 4, ATT_T, ATT_T), F32),
                   jax.ShapeDtypeStruct((D_GROUPS, 2, 2, D_T, 2 * D_T), F32)],
        compiler_params=pltpu.CompilerParams(vmem_limit_bytes=VMEM_LIMIT),
        name="rel_bias",
    )(rel_table.astype(F32), jnp.asarray(idx_a), jnp.asarray(idx_d))


def _rows(idx, size):
    return pl.ds(pl.multiple_of(idx * size, size), size)


def _softmax_stages(scores, ms, acc_ref, vs, ls=None):
    n = len(scores)
    m_new = [jnp.maximum(ms[c], jnp.max(scores[c], axis=-1, keepdims=True)) for c in range(n)]
    alpha = [jnp.exp2(ms[c] - m_new[c]) for c in range(n)]
    p_bf16, l_new = [], []
    for c in range(n):
        p = jnp.exp2(scores[c] - m_new[c])
        p_bf16.append(p.astype(BF16))
        if ls is not None:
            l_new.append(alpha[c] * ls[c] + jnp.sum(p, axis=-1, keepdims=True))
    pv = [jnp.dot(p_bf16[c], vs[c], preferred_element_type=F32) for c in range(n)]
    for c in range(n):
        acc_ref[c] = alpha[c] * acc_ref[c] + pv[c]
    return m_new, (l_new if ls is not None else None)


def _split_heads(q):
    first = _lane_half((q.shape[0], LANES))
    zero = jnp.zeros((q.shape[0], LANES), q.dtype)
    out = []
    for p in range(q.shape[1] // LANES):
        qp = q[:, p * LANES:(p + 1) * LANES]
        out += [jnp.where(first, qp, zero), jnp.where(first, zero, qp)]
    return out


def _attn_specs(width, col0, s):
    nq = s // ATT_T
    q_spec = pl.BlockSpec((ATT_T, width), lambda b, i: (b * nq + i, col0))
    k_spec = pl.BlockSpec((s, width), lambda b, i: (b, col0 + 1))
    v_spec = pl.BlockSpec((s, width), lambda b, i: (b, col0 + 2))
    o_spec = pl.BlockSpec((ATT_T, width), lambda b, i: (b * nq + i, 0))
    return q_spec, k_spec, v_spec, o_spec


def _window_split(qi):
    return lax.shift_right_logical(qi, 1), jnp.bitwise_and(qi, 1)


def _attn_a_kernel(q_ref, k_ref, v_ref, bias_ref, lam_ref, subln_ref, o_ref, acc_ref, *, lam_init):
    qi = pl.program_id(1)
    qs = _split_heads(q_ref[...])
    n_chain = len(qs)
    acc_ref[...] = jnp.zeros(acc_ref.shape, F32)
    col = lambda val: jnp.full((ATT_T, 1), val, F32)
    init = (col(NEG_INF), col(0.0)) * n_chain
    last, odd = _window_split(qi)

    def window(j, carry, kinds):
        rows = _rows(j, ATT_WIN)
        scores, vs = [], []
        for h in range(A_HEADS):
            cols = slice(h * LANES, (h + 1) * LANES)
            k, v = k_ref[rows, cols], v_ref[rows, cols]
            if kinds is not None:
                bias = jnp.concatenate([bias_ref[h, kinds[0]], bias_ref[h, kinds[1]]], axis=1)
            for c in (2 * h, 2 * h + 1):
                s = _qk(qs[c], k)
                scores.append(s if kinds is None else s + bias)
                vs.append(v)
        ms, ls = _softmax_stages(scores, carry[0::2], acc_ref, vs, ls=carry[1::2])
        return tuple(x for pair in zip(ms, ls) for x in pair)

    n_plain = jnp.maximum(last - 1 + odd, 0)
    carry = lax.fori_loop(0, n_plain, lambda j, c: window(j, c, None), init)

    def near(j, c):
        left = jnp.where(odd == 1, 0, jnp.where(j == last, 1, 2))
        right = jnp.where(odd == 1, 1, jnp.where(j == last, 3, 0))
        return window(j, c, (left, right))

    carry = lax.fori_loop(n_plain, last + 1, near, carry)

    lv = lam_ref[...]
    lam = (jnp.exp(jnp.sum(lv[0:1] * lv[1:2], axis=-1, keepdims=True))
           - jnp.exp(jnp.sum(lv[2:3] * lv[3:4], axis=-1, keepdims=True)) + lam_init)
    for h in range(A_HEADS):
        l0, l1 = carry[4 * h + 1], carry[4 * h + 3]
        o = acc_ref[2 * h] / l0 - lam * (acc_ref[2 * h + 1] / l1)
        o_ref[:, h * LANES:(h + 1) * LANES] = (
            _rms(o, subln_ref[...]) * (1.0 - lam_init)).astype(o_ref.dtype)


def _attn_a(p_abc, bias_a, lam_vecs, subln, lam_init, batch, s):
    n = p_abc.shape[0]
    q_spec, k_spec, v_spec, o_spec = _attn_specs(A_W, 0, s)
    return pl.pallas_call(
        functools.partial(_attn_a_kernel, lam_init=lam_init),
        grid=(batch, s // ATT_T),
        in_specs=[q_spec, k_spec, v_spec,
                  pl.BlockSpec(bias_a.shape, lambda b, i: (0, 0, 0, 0)),
                  pl.BlockSpec((4, HEAD_DIM), lambda b, i: (0, 0)),
                  pl.BlockSpec((1, 2 * HEAD_DIM), lambda b, i: (0, 0))],
        out_specs=o_spec,
        out_shape=jax.ShapeDtypeStruct((n, A_W), BF16),
        scratch_shapes=[pltpu.VMEM((2 * A_HEADS, ATT_T, LANES), F32)],
        compiler_params=_cparams(2),
        name="attn_diff",
    )(p_abc, p_abc, p_abc, bias_a, lam_vecs.astype(F32), subln.reshape(1, 2 * HEAD_DIM).astype(F32))


def _attn_b_kernel(q_ref, k_ref, v_ref, o_ref, acc_ref):
    qi = pl.program_id(1)
    qs = _split_heads(q_ref[...])
    acc_ref[...] = jnp.zeros(acc_ref.shape, F32)
    row = lax.broadcasted_iota(jnp.int32, (ATT_T, ATT_T), 0)
    colm = lax.broadcasted_iota(jnp.int32, (ATT_T, ATT_T), 1)
    later = jnp.where(row > colm, 1.0, 0.0).astype(BF16)
    strict = colm < row

    def step(kj, rsum, diag):
        rows = _rows(kj, ATT_T)
        heads = range(B_HEADS)
        cols = [slice((h // 2) * LANES, (h // 2 + 1) * LANES) for h in heads]
        z = [_qk(qs[h], k_ref[rows, cols[h]]) for h in heads]
        ls = [jnp.minimum(z[h], 0.0) - jnp.log2(1.0 + jnp.exp2(-jnp.abs(z[h]))) for h in heads]
        u = [ls[h] - z[h] for h in heads]
        if diag:
            u = [jnp.where(strict, u[h], 0.0) for h in heads]
        u_hi = [u[h].astype(BF16) for h in heads]
        u_lo = [(u[h] - u_hi[h].astype(F32)).astype(BF16) for h in heads]
        tail = [jnp.dot(u_hi[h], later, preferred_element_type=F32)
                + jnp.dot(u_lo[h], later, preferred_element_type=F32) + rsum[h] for h in heads]
        a = [jnp.exp2(ls[h] + tail[h]) for h in heads]
        if diag:
            a = [jnp.where(strict, a[h], 0.0) for h in heads]
        pv = [jnp.dot(a[h].astype(BF16), v_ref[rows, cols[h]], preferred_element_type=F32)
              for h in heads]
        for h in heads:
            acc_ref[h] = acc_ref[h] + pv[h]
        return tuple(rsum[h] + jnp.sum(u[h], axis=-1, keepdims=True) for h in heads)

    zero = jnp.zeros((ATT_T, 1), F32)
    rsum = step(qi, (zero,) * B_HEADS, True)
    lax.fori_loop(0, qi, lambda t, c: step(qi - 1 - t, c, False), rsum)
    first = _lane_half((ATT_T, LANES))
    for p in range(B_HEADS // 2):
        o_ref[:, p * LANES:(p + 1) * LANES] = jnp.where(
            first, acc_ref[2 * p], acc_ref[2 * p + 1]).astype(o_ref.dtype)


def _attn_b(p_abc, batch, s):
    n = p_abc.shape[0]
    q_spec, k_spec, v_spec, o_spec = _attn_specs(B_W, 3 * A_W // B_W, s)
    return pl.pallas_call(
        _attn_b_kernel,
        grid=(batch, s // ATT_T),
        in_specs=[q_spec, k_spec, v_spec],
        out_specs=o_spec,
        out_shape=jax.ShapeDtypeStruct((n, B_W), BF16),
        scratch_shapes=[pltpu.VMEM((B_HEADS, ATT_T, LANES), F32)],
        compiler_params=_cparams(2),
        name="attn_stick",
    )(p_abc, p_abc, p_abc)


def _attn_c_kernel(q_ref, k_ref, v_ref, cf_ref, fb_ref, o_ref, acc_ref, cumc_ref, cumr_ref, vaug_ref):
    qi = pl.program_id(1)
    s_len = cf_ref.shape[0]

    @pl.when(qi == 0)
    def _():
        r = lax.broadcasted_iota(jnp.int32, (LANES, LANES), 0)
        c = lax.broadcasted_iota(jnp.int32, (LANES, LANES), 1)
        lower = jnp.where(c <= r, 1.0, 0.0).astype(BF16)
        carry = jnp.zeros((1, LANES), F32)
        for blk in range(s_len // LANES):
            rows = slice(blk * LANES, (blk + 1) * LANES)
            log_f = _log_sigmoid(cf_ref[rows, :] + fb_ref[...])
            cum = _dot_exact_lhs(lower, log_f) + carry
            cumc_ref[rows, :] = cum * LOG2E
            cumr_ref[:, rows] = (cum * LOG2E).T
            carry = cum[LANES - 1:LANES, :]
        v = v_ref[...]
        first = jnp.bitwise_and(lax.broadcasted_iota(jnp.int32, v.shape, 1), LANES - 1) < HEAD_DIM
        one = jnp.ones(v.shape, v.dtype)
        vaug_ref[0] = jnp.where(first, v, one)
        vaug_ref[1] = jnp.where(first, one, v)

    qs = _split_heads(q_ref[...])
    acc_ref[...] = jnp.zeros(acc_ref.shape, F32)
    cum_q_tile = cumc_ref[_rows(qi, ATT_T), :]
    lane = lax.broadcasted_iota(jnp.int32, (ATT_T, LANES), 1)
    cum_q = [jnp.sum(jnp.where(lane == h, cum_q_tile, 0.0), axis=-1, keepdims=True)
             for h in range(C_HEADS)]
    last, odd = _window_split(qi)
    shape = (ATT_T, ATT_WIN)
    causal = jnp.where(lax.broadcasted_iota(jnp.int32, shape, 1) - lax.broadcasted_iota(jnp.int32, shape, 0)
                       <= odd * ATT_T, 0.0, NEG_INF)

    def window(j, carry, masked):
        rows = _rows(j, ATT_WIN)
        scores, vs = [], []
        for h in range(C_HEADS):
            cols = slice((h // 2) * LANES, (h // 2 + 1) * LANES)
            s = _qk(qs[h], k_ref[rows, cols]) + cum_q[h] - cumr_ref[h:h + 1, rows]
            scores.append(s + causal if masked else s)
            vs.append(vaug_ref[h % 2, rows, cols])
        return tuple(_softmax_stages(scores, carry, acc_ref, vs)[0])

    init = (jnp.full((ATT_T, 1), NEG_INF, F32),) * C_HEADS
    carry = lax.fori_loop(0, last, lambda j, c: window(j, c, False), init)
    window(last, carry, True)
    first = _lane_half((ATT_T, LANES))
    for p in range(C_HEADS // 2):
        a0, a1 = acc_ref[2 * p], acc_ref[2 * p + 1]
        num = jnp.where(first, a0, a1)
        den = jnp.where(first, pltpu.roll(a0, HEAD_DIM, 1), pltpu.roll(a1, HEAD_DIM, 1))
        o_ref[:, p * LANES:(p + 1) * LANES] = (num / den).astype(o_ref.dtype)


def _attn_c(p_abc, cf, forget_bias, batch, s):
    n = p_abc.shape[0]
    q_spec, k_spec, v_spec, o_spec = _attn_specs(C_W, 3 * (A_W + B_W) // C_W, s)
    fb = jnp.pad(forget_bias.astype(F32), (0, LANES - C_HEADS)).reshape(1, LANES)
    return pl.pallas_call(
        _attn_c_kernel,
        grid=(batch, s // ATT_T),
        in_specs=[q_spec, k_spec, v_spec,
                  pl.BlockSpec((s, LANES), lambda b, i: (b, 0)),
                  pl.BlockSpec((1, LANES), lambda b, i: (0, 0))],
        out_specs=o_spec,
        out_shape=jax.ShapeDtypeStruct((n, C_W), BF16),
        scratch_shapes=[pltpu.VMEM((C_HEADS, ATT_T, LANES), F32),
                        pltpu.VMEM((s, LANES), F32),
                        pltpu.VMEM((LANES, s), F32),
                        pltpu.VMEM((2, s, C_W), BF16)],
        compiler_params=_cparams(2),
        name="attn_forget",
    )(p_abc, p_abc, p_abc, cf, fb)


def _attn_d_kernel(*refs):
    qkv_refs, (bias_ref, o_ref, qc_ref, kc_ref, vc_ref, oc_ref, lc_ref, og_ref, lse_ref) = (
        refs[:3 * D_GROUPS], refs[3 * D_GROUPS:])
    s_len = o_ref.shape[0]
    n_tiles = s_len // D_T
    first = _lane_half((D_T, LANES))
    kc_ref[0:D_T, :] = jnp.zeros((D_T, LANES), BF16)
    vc_ref[0:D_T, :] = jnp.zeros((D_T, LANES), BF16)

    for g, (_, dil) in enumerate(D_PAIRS):
        q_ref, k_ref, v_ref = qkv_refs[g], qkv_refs[D_GROUPS + g], qkv_refs[2 * D_GROUPS + g]
        class_len = s_len // dil
        tiles_per_class = class_len // D_T

        for r in range(dil):
            src = pl.ds(r, class_len, stride=dil) if dil > 1 else slice(None)
            dst = slice(r * class_len, (r + 1) * class_len)
            dst_kv = slice(D_T + r * class_len, D_T + (r + 1) * class_len)
            qc_ref[dst, :] = q_ref[src, :].astype(BF16)
            kc_ref[dst_kv, :] = k_ref[src, :].astype(BF16)
            vc_ref[dst_kv, :] = v_ref[src, :].astype(BF16)

        def tile_group(i, carry):
            tiles = [i * D_UNROLL + u for u in range(D_UNROLL)]
            rows = [pl.ds(pl.multiple_of(t * D_T, D_T), D_T) for t in tiles]
            if tiles_per_class > 1:
                keys = [pl.ds(pl.multiple_of(t * D_T, D_T), 2 * D_T) for t in tiles]
                variant = [jnp.where(t % tiles_per_class == 0, 1, 0) for t in tiles]
                bias = [[bias_ref[g, h, variant[u]] for h in range(2)] for u in range(D_UNROLL)]
            else:
                keys = [pl.ds(pl.multiple_of(t * D_T + D_T, D_T), D_T) for t in tiles]
                bias = [[bias_ref[g, h, 1][:, D_T:] for h in range(2)]] * D_UNROLL
            chains = [(u, h) for u in range(D_UNROLL) for h in range(2)]
            qh = [_split_heads(qc_ref[rows[u], :]) for u in range(D_UNROLL)]
            sc = [_qk(qh[u][h], kc_ref[keys[u], :]) + bias[u][h] for u, h in chains]
            m = [jnp.max(s, axis=-1, keepdims=True) for s in sc]
            e = [jnp.exp(s - mm) for s, mm in zip(sc, m)]
            l = [jnp.sum(x, axis=-1, keepdims=True) for x in e]
            o = [jnp.dot(x.astype(BF16), vc_ref[keys[u], :], preferred_element_type=F32)
                 for x, (u, h) in zip(e, chains)]
            o = [x / ll for x, ll in zip(o, l)]
            lse = [jnp.broadcast_to(mm + jnp.log(ll), (D_T, LANES)) for mm, ll in zip(m, l)]
            for u in range(D_UNROLL):
                oc_ref[rows[u], :] = jnp.where(first, o[2 * u], o[2 * u + 1])
                lc_ref[rows[u], :] = jnp.where(first, lse[2 * u], lse[2 * u + 1])
            return carry

        lax.fori_loop(0, n_tiles // D_UNROLL, tile_group, 0)

        for r in range(dil):
            dst = pl.ds(r, class_len, stride=dil) if dil > 1 else slice(None)
            src = slice(r * class_len, (r + 1) * class_len)
            og_ref[g, dst, :] = oc_ref[src, :]
            lse_ref[g, dst, :] = lc_ref[src, :]

    lse = [lse_ref[g] for g in range(D_GROUPS)]
    top = functools.reduce(jnp.maximum, lse)
    w = [jnp.exp(x - top) for x in lse]
    num = sum(w[g] * og_ref[g] for g in range(D_GROUPS))
    o_ref[...] = (num / sum(w)).astype(o_ref.dtype)


def _attn_d(p_d, bias_d, batch, s):
    n = p_d.shape[0]
    return pl.pallas_call(
        _attn_d_kernel,
        grid=(batch,),
        in_specs=([pl.BlockSpec((s, LANES), functools.partial(lambda b, c: (b, c), c=c))
                   for c in range(3 * D_GROUPS)]
                  + [pl.BlockSpec(bias_d.shape, lambda b: (0, 0, 0, 0, 0))]),
        out_specs=pl.BlockSpec((s, LANES), lambda b: (b, 0)),
        out_shape=jax.ShapeDtypeStruct((n, LANES), BF16),
        scratch_shapes=[pltpu.VMEM((s, LANES), BF16), pltpu.VMEM((s + D_T, LANES), BF16),
                        pltpu.VMEM((s + D_T, LANES), BF16),
                        pltpu.VMEM((s, LANES), F32), pltpu.VMEM((s, LANES), F32),
                        pltpu.VMEM((D_GROUPS, s, LANES), F32),
                        pltpu.VMEM((D_GROUPS, s, LANES), F32)],
        compiler_params=_cparams(1),
        name="attn_dilated",
    )(*([p_d] * (3 * D_GROUPS)), bias_d)


def _merge_kernel(x_ref, g_ref, wg_ref, gb_ref, oa_ref, ob_ref, oc_ref, od_ref,
                  wa_ref, wb_ref, wc_ref, wd_ref, wout_ref, o_ref):
    x = x_ref[...]
    h = _rms(x, g_ref[...]).astype(BF16)
    merged = jnp.zeros(x.shape, F32)
    branches = ((oa_ref, wa_ref), (ob_ref, wb_ref), (oc_ref, wc_ref), (od_ref, wd_ref))
    for i, (br_ref, w_ref) in enumerate(branches):
        cols = slice(i * D_MODEL, (i + 1) * D_MODEL)
        logits = jnp.dot(h, wg_ref[:, cols], preferred_element_type=F32) + gb_ref[:, cols]
        branch = jnp.dot(br_ref[...], w_ref[...], preferred_element_type=F32)
        merged = merged + jax.nn.sigmoid(logits) * branch
    o_ref[...] = x + jnp.dot(merged.astype(BF16), wout_ref[...], preferred_element_type=F32)


def _merge(x2, gain, w_gate, gate_bias, outs, w_branch, w_out):
    n = x2.shape[0]
    widths = (A_W, B_W, C_W, 2 * HEAD_DIM)
    offs = np.concatenate([[0], np.cumsum(widths)])
    w_br = [w_branch[offs[i]:offs[i + 1]].astype(BF16) for i in range(N_BRANCH)]
    const = lambda i: (0, 0)
    row = lambda i: (i, 0)
    return pl.pallas_call(
        _merge_kernel,
        grid=(n // TOK_TILE,),
        in_specs=([pl.BlockSpec((TOK_TILE, D_MODEL), row),
                   pl.BlockSpec((1, D_MODEL), const),
                   pl.BlockSpec((D_MODEL, N_BRANCH * D_MODEL), const, pipeline_mode=pl.Buffered(1)),
                   pl.BlockSpec((1, N_BRANCH * D_MODEL), const)]
                  + [pl.BlockSpec((TOK_TILE, w), row) for w in widths]
                  + [pl.BlockSpec((w, D_MODEL), const) for w in widths]
                  + [pl.BlockSpec((D_MODEL, D_MODEL), const)]),
        out_specs=pl.BlockSpec((TOK_TILE, D_MODEL), row),
        out_shape=jax.ShapeDtypeStruct((n, D_MODEL), F32),
        compiler_params=_cparams(1),
        name="merge",
    )(x2, gain.reshape(1, D_MODEL), w_gate.astype(BF16), gate_bias.reshape(1, -1).astype(F32),
      *outs, *w_br, w_out.astype(BF16))


def kernel(x, rel_table, ffn1_norm, ffn1_w_in, ffn1_w_out, mix_norm, w_in, gate_bias, forget_bias,
           a_q_norm, a_k_norm, a_lambda, a_subln, c_q_norm, c_k_norm, d_q_norm, d_k_norm,
           w_branch, w_out, ffn2_norm, ffn2_w_in, ffn2_w_out):
    batch, s, _ = x.shape
    depth = w_in.shape[0]
    assert s % (D_T * D_PAIRS[-1][1]) == 0 and (batch * s) % TOK_TILE == 0
    bias_a, bias_d = _bias_tiles(rel_table)
    w_all = _prep_w_in(w_in)
    x2 = x.reshape(batch * s, D_MODEL)
    for l in range(depth):
        x2 = _ffn(x2, ffn1_norm[l], ffn1_w_in[l], ffn1_w_out[l])
        p_abc, p_d, cf = _proj(x2, mix_norm[l], w_all, l, a_q_norm[l], a_k_norm[l],
                               c_q_norm[l], c_k_norm[l], d_q_norm[l], d_k_norm[l])
        lam_init = 0.8 - 0.6 * math.exp(-0.3 * l)
        oa = _attn_a(p_abc, bias_a, a_lambda[l], a_subln[l], lam_init, batch, s)
        ob = _attn_b(p_abc, batch, s)
        oc = _attn_c(p_abc, cf, forget_bias[l], batch, s)
        od = _attn_d(p_d, bias_d, batch, s)
        x2 = _merge(x2, mix_norm[l], w_all[l, :, W_GATE_OFF:], gate_bias[l], (oa, ob, oc, od),
                    w_branch[l], w_out[l])
        x2 = _ffn(x2, ffn2_norm[l], ffn2_w_in[l], ffn2_w_out[l])
    return x2.reshape(batch, s, D_MODEL)
```

```python
import functools
import math

import numpy as np
import jax
import jax.numpy as jnp
from jax import lax
from jax.experimental import pallas as pl
from jax.experimental.pallas import tpu as pltpu

F32 = jnp.float32
BF16 = jnp.bfloat16

D_MODEL = 1024
HEAD_DIM = 64
SCALE = HEAD_DIM ** -0.5
A_HEADS = 4
B_HEADS = 6
C_HEADS = 6
D_PAIRS = ((128, 1), (512, 4), (2048, 16))
D_GROUPS = len(D_PAIRS)
N_BRANCH = 4
FFN_HIDDEN = 2816
REL_BUCKETS = 32
REL_MAX_DIST = 128
RMS_EPS = 1e-6
NEG_INF = -1e30

LANES = 128
A_W = A_HEADS * 2 * HEAD_DIM
B_W = B_HEADS * HEAD_DIM
C_W = C_HEADS * HEAD_DIM
D_W = D_GROUPS * 2 * HEAD_DIM
ABC_W = 3 * (A_W + B_W + C_W)

W_SHIFT = (ABC_W + C_HEADS) % LANES
W_ALIGNED_BLOCKS = ABC_W // LANES + 1
W_D_OFF = W_ALIGNED_BLOCKS * LANES
W_D_END = W_D_OFF + 3 * D_W
W_GATE_OFF = 2 * N_BRANCH * D_MODEL
GROUP_W = 256

TOK_TILE = 512
FFN_CHUNK = 256
ATT_T = 256
ATT_WIN = 2 * ATT_T
LOG2E = math.log2(math.e)
D_UNROLL = 4
D_T = 128
VMEM_LIMIT = 56 * 1024 * 1024


def _cparams(n_axes):
    return pltpu.CompilerParams(dimension_semantics=("arbitrary",) * n_axes,
                                vmem_limit_bytes=VMEM_LIMIT)


def _rms(x, g):
    ms = jnp.mean(x * x, axis=-1, keepdims=True)
    return x * lax.rsqrt(ms + RMS_EPS) * g


def _split3(a):
    hi = a.astype(BF16)
    r1 = a - hi.astype(F32)
    mid = r1.astype(BF16)
    lo = (r1 - mid.astype(F32)).astype(BF16)
    return hi, mid, lo


def _dot_exact_lhs(a_bf16, b):
    hi, mid, lo = _split3(b)
    d = functools.partial(jnp.dot, preferred_element_type=F32)
    return d(a_bf16, hi) + d(a_bf16, mid) + d(a_bf16, lo)


def _qk(q, k):
    return lax.dot_general(q, k, (((1,), (1,)), ((), ())), preferred_element_type=F32)


def _log_sigmoid(x):
    return jnp.minimum(x, 0.0) - jnp.log(1.0 + jnp.exp(-jnp.abs(x)))


def _lane_half(shape):
    return lax.broadcasted_iota(jnp.int32, shape, len(shape) - 1) < HEAD_DIM


def _ffn_kernel(x_ref, g_ref, wi_ref, wo_ref, o_ref):
    x = x_ref[...]
    h = _rms(x, g_ref[...]).astype(BF16)
    acc = jnp.zeros(x.shape, F32)
    for c in range(FFN_HIDDEN // FFN_CHUNK):
        lo = c * FFN_CHUNK
        gate = jnp.dot(h, wi_ref[:, lo:lo + FFN_CHUNK], preferred_element_type=F32)
        up = jnp.dot(h, wi_ref[:, FFN_HIDDEN + lo:FFN_HIDDEN + lo + FFN_CHUNK],
                     preferred_element_type=F32)
        a = (gate * jax.nn.sigmoid(gate) * up).astype(BF16)
        acc = acc + jnp.dot(a, wo_ref[c * FFN_CHUNK:(c + 1) * FFN_CHUNK, :],
                            preferred_element_type=F32)
    o_ref[...] = x + 0.5 * acc


def _ffn(x2, gain, w_i, w_o):
    n = x2.shape[0]
    const = lambda i: (0, 0)
    return pl.pallas_call(
        _ffn_kernel,
        grid=(n // TOK_TILE,),
        in_specs=[
            pl.BlockSpec((TOK_TILE, D_MODEL), lambda i: (i, 0)),
            pl.BlockSpec((1, D_MODEL), const),
            pl.BlockSpec((D_MODEL, 2 * FFN_HIDDEN), const, pipeline_mode=pl.Buffered(1)),
            pl.BlockSpec((FFN_HIDDEN, D_MODEL), const, pipeline_mode=pl.Buffered(1)),
        ],
        out_specs=pl.BlockSpec((TOK_TILE, D_MODEL), lambda i: (i, 0)),
        out_shape=jax.ShapeDtypeStruct((n, D_MODEL), F32),
        compiler_params=_cparams(1),
        name="ffn",
    )(x2, gain.reshape(1, D_MODEL), w_i.astype(BF16), w_o.astype(BF16))


_ABC_SECTIONS = ((A_W, True), (A_W, True), (A_W, False),
                 (B_W, False), (B_W, False), (B_W, False),
                 (C_W, True), (C_W, True), (C_W, False))
_D_SECTIONS = ((D_W, True), (D_W, True), (D_W, False))


def _proj_sections(h, w_ref, w_off, gain_ref, gmat, out_ref, sections):
    off = 0
    for width, normed in sections:
        y = jnp.dot(h, w_ref[:, w_off + off:w_off + off + width], preferred_element_type=F32)
        lo = 0
        while lo < width:
            blk = min(GROUP_W, width - lo)
            yb = y[:, lo:lo + blk]
            if normed:
                ms = jnp.dot((yb * yb).astype(BF16), gmat[:blk, :blk], preferred_element_type=F32)
                yb = yb * lax.rsqrt(ms + RMS_EPS)
            gain = gain_ref[:, off + lo:off + lo + blk]
            out_ref[:, off + lo:off + lo + blk] = (yb * gain).astype(out_ref.dtype)
            lo += blk
        off += width


def _proj_kernel(x_ref, g_ref, w_ref, gabc_ref, gd_ref, gmat_ref, pabc_ref, pd_ref, cf_ref):
    h = _rms(x_ref[...], g_ref[...]).astype(BF16)
    gmat = gmat_ref[...]
    _proj_sections(h, w_ref, 0, gabc_ref, gmat, pabc_ref, _ABC_SECTIONS)
    _proj_sections(h, w_ref, W_D_OFF, gd_ref, gmat, pd_ref, _D_SECTIONS)
    cf_ref[...] = jnp.dot(h, w_ref[:, ABC_W:ABC_W + LANES], preferred_element_type=F32)


def _head_gain(gain, width, scale):
    return jnp.tile(gain.astype(F32), width // HEAD_DIM) * scale


def _win_kernel(a_ref, b_ref, o_ref):
    c = pl.program_id(1)

    @pl.when(c < W_ALIGNED_BLOCKS)
    def _():
        o_ref[...] = b_ref[...].astype(BF16)

    @pl.when(c >= W_ALIGNED_BLOCKS)
    def _():
        both = jnp.concatenate([a_ref[...], b_ref[...]], axis=1)
        o_ref[...] = both[:, W_SHIFT:W_SHIFT + LANES].astype(BF16)


def _prep_w_in(w_in):
    depth = w_in.shape[0]
    n_blocks = pl.cdiv(w_in.shape[2], LANES)
    gate_block0, gate_skip = W_D_END // LANES, (W_GATE_OFF - W_D_END) // LANES
    return pl.pallas_call(
        _win_kernel,
        grid=(depth, n_blocks),
        in_specs=[pl.BlockSpec((None, D_MODEL, LANES), lambda l, c: (l, 0, jnp.maximum(c - 1, 0))),
                  pl.BlockSpec((None, D_MODEL, LANES), lambda l, c: (l, 0, c))],
        out_specs=pl.BlockSpec((None, D_MODEL, LANES),
                               lambda l, c: (l, 0, jnp.where(c < gate_block0, c, c + gate_skip))),
        out_shape=jax.ShapeDtypeStruct((depth, D_MODEL, W_GATE_OFF + N_BRANCH * D_MODEL), BF16),
        compiler_params=_cparams(2),
        name="prep_w_in",
    )(w_in, w_in)


def _proj(x2, gain, w_all, layer, a_q, a_k, c_q, c_k, d_q, d_k):
    n = x2.shape[0]
    ones = lambda w, s: jnp.full((w,), s, F32)
    g_abc = jnp.concatenate([
        _head_gain(a_q, A_W, SCALE * LOG2E), _head_gain(a_k, A_W, 1.0), ones(A_W, 1.0),
        ones(B_W, SCALE * LOG2E), ones(B_W, 1.0), ones(B_W, 1.0),
        _head_gain(c_q, C_W, SCALE * LOG2E), _head_gain(c_k, C_W, 1.0),
        ones(C_W, 1.0)]).reshape(1, ABC_W)
    g_d = jnp.concatenate([_head_gain(d_q, D_W, SCALE), _head_gain(d_k, D_W, 1.0),
                           ones(D_W, 1.0)]).reshape(1, 3 * D_W)
    lane = np.arange(GROUP_W)
    gmat = jnp.asarray((lane[:, None] // HEAD_DIM == lane[None, :] // HEAD_DIM) / HEAD_DIM, BF16)
    const = lambda i: (0, 0)
    row = lambda i: (i, 0)
    return pl.pallas_call(
        _proj_kernel,
        grid=(n // TOK_TILE,),
        in_specs=[
            pl.BlockSpec((TOK_TILE, D_MODEL), row),
            pl.BlockSpec((1, D_MODEL), const),
            pl.BlockSpec((None, D_MODEL, W_D_END), lambda i: (layer, 0, 0),
                         pipeline_mode=pl.Buffered(1)),
            pl.BlockSpec((1, ABC_W), const),
            pl.BlockSpec((1, 3 * D_W), const),
            pl.BlockSpec((GROUP_W, GROUP_W), const),
        ],
        out_specs=[
            pl.BlockSpec((TOK_TILE, ABC_W), row),
            pl.BlockSpec((TOK_TILE, 3 * D_W), row),
            pl.BlockSpec((TOK_TILE, LANES), row),
        ],
        out_shape=[
            jax.ShapeDtypeStruct((n, ABC_W), BF16),
            jax.ShapeDtypeStruct((n, 3 * D_W), F32),
            jax.ShapeDtypeStruct((n, LANES), F32),
        ],
        compiler_params=_cparams(1),
        name="proj",
    )(x2, gain.reshape(1, D_MODEL), w_all, g_abc, g_d, gmat)


_MASKED_BUCKET = REL_BUCKETS


def _rel_bucket_np(n):
    n = np.asarray(n, np.int64)
    max_exact = REL_BUCKETS // 2
    nf = np.maximum(n, 1).astype(np.float32)
    large = max_exact + (np.log(nf / np.float32(max_exact)) / np.float32(math.log(REL_MAX_DIST / max_exact))
                         * np.float32(REL_BUCKETS - max_exact)).astype(np.int32)
    large = np.minimum(large, REL_BUCKETS - 1)
    return np.where(n < max_exact, n, large).astype(np.int32)


def _bucket_maps():
    t = ATT_T
    x, y = np.arange(t)[:, None], np.arange(t)[None, :]
    diag = np.where(x >= y, _rel_bucket_np(np.maximum(x - y, 0)), _MASKED_BUCKET)
    prev = _rel_bucket_np(t + x - y)
    idx_a = np.stack([prev, diag]).astype(np.int32)
    x, y = np.arange(D_T)[:, None], np.arange(D_T)[None, :]
    idx_d = np.empty((D_GROUPS, 2, D_T, 2 * D_T), np.int32)
    for g, (_, dil) in enumerate(D_PAIRS):
        prev = np.where(y >= x, _rel_bucket_np(dil * (D_T + x - y)), _MASKED_BUCKET)
        diag = np.where(y <= x, _rel_bucket_np(dil * np.maximum(x - y, 0)), _MASKED_BUCKET)
        idx_d[g, 0] = np.concatenate([prev, diag], axis=1)
        idx_d[g, 1] = np.concatenate([np.full_like(prev, _MASKED_BUCKET), diag], axis=1)
    return idx_a, idx_d


def _bias_kernel(tbl_ref, idxa_ref, idxd_ref, ba_ref, bd_ref):
    for h in range(A_HEADS):
        for kind in range(2):
            ba_ref[h, kind] = jnp.where(idxa_ref[kind] == _MASKED_BUCKET, NEG_INF, 0.0)
        ba_ref[h, 2] = jnp.zeros((ATT_T, ATT_T), F32)
        ba_ref[h, 3] = jnp.full((ATT_T, ATT_T), NEG_INF, F32)
    for g in range(D_GROUPS):
        for h in range(2):
            for var in range(2):
                bd_ref[g, h, var] = jnp.where(idxd_ref[g, var] == _MASKED_BUCKET, NEG_INF, 0.0)

    def body(b, carry):
        for h in range(A_HEADS):
            val = (tbl_ref[b, h] - tbl_ref[REL_BUCKETS - 1, h]) * LOG2E
            for kind in range(2):
                ba_ref[h, kind] = jnp.where(idxa_ref[kind] == b, val, ba_ref[h, kind])
        for g in range(D_GROUPS):
            for h in range(2):
                val = tbl_ref[b, A_HEADS + 2 * g + h]
                for var in range(2):
                    bd_ref[g, h, var] = jnp.where(idxd_ref[g, var] == b, val, bd_ref[g, h, var])
        return carry

    lax.fori_loop(0, REL_BUCKETS, body, 0)


def _bias_tiles(rel_table):
    idx_a, idx_d = _bucket_maps()
    return pl.pallas_call(
        _bias_kernel,
        in_specs=[pl.BlockSpec(memory_space=pltpu.SMEM),
                  pl.BlockSpec(memory_space=pltpu.VMEM),
                  pl.BlockSpec(memory_space=pltpu.VMEM)],
        out_specs=[pl.BlockSpec(memory_space=pltpu.VMEM),
                   pl.BlockSpec(memory_space=pltpu.VMEM)],
        out_shape=[jax.ShapeDtypeStruct((A_HEADS, 4, ATT_T, ATT_T), F32),
                   jax.ShapeDtypeStruct((D_GROUPS, 2, 2, D_T, 2 * D_T), F32)],
        compiler_params=pltpu.CompilerParams(vmem_limit_bytes=VMEM_LIMIT),
        name="rel_bias",
    )(rel_table.astype(F32), jnp.asarray(idx_a), jnp.asarray(idx_d))


def _rows(idx, size):
    return pl.ds(pl.multiple_of(idx * size, size), size)


def _softmax_stages(scores, ms, acc_ref, vs, ls=None):
    n = len(scores)
    m_new = [jnp.maximum(ms[c], jnp.max(scores[c], axis=-1, keepdims=True)) for c in range(n)]
    alpha = [jnp.exp2(ms[c] - m_new[c]) for c in range(n)]
    p_bf16, l_new = [], []
    for c in range(n):
        p = jnp.exp2(scores[c] - m_new[c])
        p_bf16.append(p.astype(BF16))
        if ls is not None:
            l_new.append(alpha[c] * ls[c] + jnp.sum(p, axis=-1, keepdims=True))
    pv = [jnp.dot(p_bf16[c], vs[c], preferred_element_type=F32) for c in range(n)]
    for c in range(n):
        acc_ref[c] = alpha[c] * acc_ref[c] + pv[c]
    return m_new, (l_new if ls is not None else None)


def _split_heads(q):
    first = _lane_half((q.shape[0], LANES))
    zero = jnp.zeros((q.shape[0], LANES), q.dtype)
    out = []
    for p in range(q.shape[1] // LANES):
        qp = q[:, p * LANES:(p + 1) * LANES]
        out += [jnp.where(first, qp, zero), jnp.where(first, zero, qp)]
    return out


def _attn_specs(width, col0, s):
    nq = s // ATT_T
    q_spec = pl.BlockSpec((ATT_T, width), lambda b, i: (b * nq + i, col0))
    k_spec = pl.BlockSpec((s, width), lambda b, i: (b, col0 + 1))
    v_spec = pl.BlockSpec((s, width), lambda b, i: (b, col0 + 2))
    o_spec = pl.BlockSpec((ATT_T, width), lambda b, i: (b * nq + i, 0))
    return q_spec, k_spec, v_spec, o_spec


def _window_split(qi):
    return lax.shift_right_logical(qi + 1, 1), jnp.bitwise_and(qi, 1)


def _attn_a_kernel(q_ref, k_ref, v_ref, bias_ref, lam_ref, subln_ref, o_ref, acc_ref, *, lam_init):
    qi = pl.program_id(1)
    qs = _split_heads(q_ref[...])
    n_chain = len(qs)
    acc_ref[...] = jnp.zeros(acc_ref.shape, F32)
    col = lambda val: jnp.full((ATT_T, 1), val, F32)
    init = (col(NEG_INF), col(0.0)) * n_chain
    n_full, odd = _window_split(qi)

    def window(rows, carry, kinds):
        scores, vs = [], []
        for h in range(A_HEADS):
            cols = slice(h * LANES, (h + 1) * LANES)
            k, v = k_ref[rows, cols], v_ref[rows, cols]
            if kinds is not None:
                bias = jnp.concatenate([bias_ref[h, kind] for kind in kinds], axis=1)
            for c in (2 * h, 2 * h + 1):
                s = _qk(qs[c], k)
                scores.append(s if kinds is None else s + bias)
                vs.append(v)
        ms, ls = _softmax_stages(scores, carry[0::2], acc_ref, vs, ls=carry[1::2])
        return tuple(x for pair in zip(ms, ls) for x in pair)

    n_plain = jnp.maximum(n_full - 1, 0)
    carry = lax.fori_loop(0, n_plain, lambda j, c: window(_rows(j, ATT_WIN), c, None), init)
    near = (jnp.where(odd == 1, 0, 2), jnp.where(odd == 1, 1, 0))
    carry = lax.fori_loop(n_plain, n_full, lambda j, c: window(_rows(j, ATT_WIN), c, near), carry)
    carry = lax.fori_loop(0, 1 - odd, lambda _, c: window(_rows(qi, ATT_T), c, (1,)), carry)

    lv = lam_ref[...]
    lam = (jnp.exp(jnp.sum(lv[0:1] * lv[1:2], axis=-1, keepdims=True))
           - jnp.exp(jnp.sum(lv[2:3] * lv[3:4], axis=-1, keepdims=True)) + lam_init)
    for h in range(A_HEADS):
        l0, l1 = carry[4 * h + 1], carry[4 * h + 3]
        o = acc_ref[2 * h] / l0 - lam * (acc_ref[2 * h + 1] / l1)
        o_ref[:, h * LANES:(h + 1) * LANES] = (
            _rms(o, subln_ref[...]) * (1.0 - lam_init)).astype(o_ref.dtype)


def _attn_a(p_abc, bias_a, lam_vecs, subln, lam_init, batch, s):
    n = p_abc.shape[0]
    q_spec, k_spec, v_spec, o_spec = _attn_specs(A_W, 0, s)
    return pl.pallas_call(
        functools.partial(_attn_a_kernel, lam_init=lam_init),
        grid=(batch, s // ATT_T),
        in_specs=[q_spec, k_spec, v_spec,
                  pl.BlockSpec(bias_a.shape, lambda b, i: (0, 0, 0, 0)),
                  pl.BlockSpec((4, HEAD_DIM), lambda b, i: (0, 0)),
                  pl.BlockSpec((1, 2 * HEAD_DIM), lambda b, i: (0, 0))],
        out_specs=o_spec,
        out_shape=jax.ShapeDtypeStruct((n, A_W), BF16),
        scratch_shapes=[pltpu.VMEM((2 * A_HEADS, ATT_T, LANES), F32)],
        compiler_params=_cparams(2),
        name="attn_diff",
    )(p_abc, p_abc, p_abc, bias_a, lam_vecs.astype(F32), subln.reshape(1, 2 * HEAD_DIM).astype(F32))


def _attn_b_kernel(q_ref, k_ref, v_ref, o_ref, acc_ref):
    qi = pl.program_id(1)
    qs = _split_heads(q_ref[...])
    acc_ref[...] = jnp.zeros(acc_ref.shape, F32)
    row = lax.broadcasted_iota(jnp.int32, (ATT_T, ATT_T), 0)
    colm = lax.broadcasted_iota(jnp.int32, (ATT_T, ATT_T), 1)
    later = jnp.where(row > colm, 1.0, 0.0).astype(BF16)
    strict = colm < row

    def step(kj, rsum, diag):
        rows = _rows(kj, ATT_T)
        heads = range(B_HEADS)
        cols = [slice((h // 2) * LANES, (h // 2 + 1) * LANES) for h in heads]
        z = [_qk(qs[h], k_ref[rows, cols[h]]) for h in heads]
        ls = [jnp.minimum(z[h], 0.0) - jnp.log2(1.0 + jnp.exp2(-jnp.abs(z[h]))) for h in heads]
        u = [ls[h] - z[h] for h in heads]
        if diag:
            u = [jnp.where(strict, u[h], 0.0) for h in heads]
        tail = [jnp.dot(u[h].astype(BF16), later, preferred_element_type=F32) + rsum[h] for h in heads]
        a = [jnp.exp2(ls[h] + tail[h]) for h in heads]
        if diag:
            a = [jnp.where(strict, a[h], 0.0) for h in heads]
        pv = [jnp.dot(a[h].astype(BF16), v_ref[rows, cols[h]], preferred_element_type=F32)
              for h in heads]
        for h in heads:
            acc_ref[h] = acc_ref[h] + pv[h]
        return tuple(rsum[h] + jnp.sum(u[h], axis=-1, keepdims=True) for h in heads)

    zero = jnp.zeros((ATT_T, 1), F32)
    rsum = step(qi, (zero,) * B_HEADS, True)
    lax.fori_loop(0, qi, lambda t, c: step(qi - 1 - t, c, False), rsum)
    first = _lane_half((ATT_T, LANES))
    for p in range(B_HEADS // 2):
        o_ref[:, p * LANES:(p + 1) * LANES] = jnp.where(
            first, acc_ref[2 * p], acc_ref[2 * p + 1]).astype(o_ref.dtype)


def _attn_b(p_abc, batch, s):
    n = p_abc.shape[0]
    q_spec, k_spec, v_spec, o_spec = _attn_specs(B_W, 3 * A_W // B_W, s)
    return pl.pallas_call(
        _attn_b_kernel,
        grid=(batch, s // ATT_T),
        in_specs=[q_spec, k_spec, v_spec],
        out_specs=o_spec,
        out_shape=jax.ShapeDtypeStruct((n, B_W), BF16),
        scratch_shapes=[pltpu.VMEM((B_HEADS, ATT_T, LANES), F32)],
        compiler_params=_cparams(2),
        name="attn_stick",
    )(p_abc, p_abc, p_abc)


def _attn_c_kernel(q_ref, k_ref, v_ref, cf_ref, fb_ref, o_ref, acc_ref, cumc_ref, cumr_ref, vaug_ref):
    qi = pl.program_id(1)
    s_len = cf_ref.shape[0]

    @pl.when(qi == 0)
    def _():
        r = lax.broadcasted_iota(jnp.int32, (LANES, LANES), 0)
        c = lax.broadcasted_iota(jnp.int32, (LANES, LANES), 1)
        lower = jnp.where(c <= r, 1.0, 0.0).astype(BF16)
        carry = jnp.zeros((1, LANES), F32)
        for blk in range(s_len // LANES):
            rows = slice(blk * LANES, (blk + 1) * LANES)
            log_f = _log_sigmoid(cf_ref[rows, :] + fb_ref[...])
            cum = _dot_exact_lhs(lower, log_f) + carry
            cumc_ref[rows, :] = cum * LOG2E
            cumr_ref[:, rows] = (cum * LOG2E).T
            carry = cum[LANES - 1:LANES, :]
        v = v_ref[...]
        first = jnp.bitwise_and(lax.broadcasted_iota(jnp.int32, v.shape, 1), LANES - 1) < HEAD_DIM
        one = jnp.ones(v.shape, v.dtype)
        vaug_ref[0] = jnp.where(first, v, one)
        vaug_ref[1] = jnp.where(first, one, v)

    qs = _split_heads(q_ref[...])
    acc_ref[...] = jnp.zeros(acc_ref.shape, F32)
    cum_q_tile = cumc_ref[_rows(qi, ATT_T), :]
    lane = lax.broadcasted_iota(jnp.int32, (ATT_T, LANES), 1)
    cum_q = [jnp.sum(jnp.where(lane == h, cum_q_tile, 0.0), axis=-1, keepdims=True)
             for h in range(C_HEADS)]
    n_full, odd = _window_split(qi)

    def causal(width):
        col = lax.broadcasted_iota(jnp.int32, (ATT_T, width), 1) - (width - ATT_T)
        return jnp.where(col <= lax.broadcasted_iota(jnp.int32, (ATT_T, width), 0), 0.0, NEG_INF)

    def window(rows, carry, mask):
        scores, vs = [], []
        for h in range(C_HEADS):
            cols = slice((h // 2) * LANES, (h // 2 + 1) * LANES)
            s = _qk(qs[h], k_ref[rows, cols]) + cum_q[h] - cumr_ref[h:h + 1, rows]
            scores.append(s if mask is None else s + mask)
            vs.append(vaug_ref[h % 2, rows, cols])
        return tuple(_softmax_stages(scores, carry, acc_ref, vs)[0])

    init = (jnp.full((ATT_T, 1), NEG_INF, F32),) * C_HEADS
    carry = lax.fori_loop(0, n_full - odd, lambda j, c: window(_rows(j, ATT_WIN), c, None), init)
    carry = lax.fori_loop(n_full - odd, n_full,
                          lambda j, c: window(_rows(j, ATT_WIN), c, causal(ATT_WIN)), carry)
    lax.fori_loop(0, 1 - odd, lambda _, c: window(_rows(qi, ATT_T), c, causal(ATT_T)), carry)
    first = _lane_half((ATT_T, LANES))
    for p in range(C_HEADS // 2):
        a0, a1 = acc_ref[2 * p], acc_ref[2 * p + 1]
        num = jnp.where(first, a0, a1)
        den = jnp.where(first, pltpu.roll(a0, HEAD_DIM, 1), pltpu.roll(a1, HEAD_DIM, 1))
        o_ref[:, p * LANES:(p + 1) * LANES] = (num / den).astype(o_ref.dtype)


def _attn_c(p_abc, cf, forget_bias, batch, s):
    n = p_abc.shape[0]
    q_spec, k_spec, v_spec, o_spec = _attn_specs(C_W, 3 * (A_W + B_W) // C_W, s)
    fb = jnp.pad(forget_bias.astype(F32), (0, LANES - C_HEADS)).reshape(1, LANES)
    return pl.pallas_call(
        _attn_c_kernel,
        grid=(batch, s // ATT_T),
        in_specs=[q_spec, k_spec, v_spec,
                  pl.BlockSpec((s, LANES), lambda b, i: (b, 0)),
                  pl.BlockSpec((1, LANES), lambda b, i: (0, 0))],
        out_specs=o_spec,
        out_shape=jax.ShapeDtypeStruct((n, C_W), BF16),
        scratch_shapes=[pltpu.VMEM((C_HEADS, ATT_T, LANES), F32),
                        pltpu.VMEM((s, LANES), F32),
                        pltpu.VMEM((LANES, s), F32),
                        pltpu.VMEM((2, s, C_W), BF16)],
        compiler_params=_cparams(2),
        name="attn_forget",
    )(p_abc, p_abc, p_abc, cf, fb)


def _attn_d_kernel(*refs):
    qkv_refs, (bias_ref, o_ref, qc_ref, kc_ref, vc_ref, oc_ref, lc_ref, og_ref, lse_ref) = (
        refs[:3 * D_GROUPS], refs[3 * D_GROUPS:])
    s_len = o_ref.shape[0]
    n_tiles = s_len // D_T
    first = _lane_half((D_T, LANES))
    kc_ref[0:D_T, :] = jnp.zeros((D_T, LANES), BF16)
    vc_ref[0:D_T, :] = jnp.zeros((D_T, LANES), BF16)

    for g, (_, dil) in enumerate(D_PAIRS):
        q_ref, k_ref, v_ref = qkv_refs[g], qkv_refs[D_GROUPS + g], qkv_refs[2 * D_GROUPS + g]
        class_len = s_len // dil
        tiles_per_class = class_len // D_T

        for r in range(dil):
            src = pl.ds(r, class_len, stride=dil) if dil > 1 else slice(None)
            dst = slice(r * class_len, (r + 1) * class_len)
            dst_kv = slice(D_T + r * class_len, D_T + (r + 1) * class_len)
            qc_ref[dst, :] = q_ref[src, :].astype(BF16)
            kc_ref[dst_kv, :] = k_ref[src, :].astype(BF16)
            vc_ref[dst_kv, :] = v_ref[src, :].astype(BF16)

        def tile_group(i, carry):
            tiles = [i * D_UNROLL + u for u in range(D_UNROLL)]
            rows = [pl.ds(pl.multiple_of(t * D_T, D_T), D_T) for t in tiles]
            if tiles_per_class > 1:
                keys = [pl.ds(pl.multiple_of(t * D_T, D_T), 2 * D_T) for t in tiles]
                variant = [jnp.where(t % tiles_per_class == 0, 1, 0) for t in tiles]
                bias = [[bias_ref[g, h, variant[u]] for h in range(2)] for u in range(D_UNROLL)]
            else:
                keys = [pl.ds(pl.multiple_of(t * D_T + D_T, D_T), D_T) for t in tiles]
                bias = [[bias_ref[g, h, 1][:, D_T:] for h in range(2)]] * D_UNROLL
            chains = [(u, h) for u in range(D_UNROLL) for h in range(2)]
            qh = [_split_heads(qc_ref[rows[u], :]) for u in range(D_UNROLL)]
            sc = [_qk(qh[u][h], kc_ref[keys[u], :]) + bias[u][h] for u, h in chains]
            m = [jnp.max(s, axis=-1, keepdims=True) for s in sc]
            e = [jnp.exp(s - mm) for s, mm in zip(sc, m)]
            l = [jnp.sum(x, axis=-1, keepdims=True) for x in e]
            o = [jnp.dot(x.astype(BF16), vc_ref[keys[u], :], preferred_element_type=F32)
                 for x, (u, h) in zip(e, chains)]
            o = [x / ll for x, ll in zip(o, l)]
            lse = [jnp.broadcast_to(mm + jnp.log(ll), (D_T, LANES)) for mm, ll in zip(m, l)]
            for u in range(D_UNROLL):
                oc_ref[rows[u], :] = jnp.where(first, o[2 * u], o[2 * u + 1])
                lc_ref[rows[u], :] = jnp.where(first, lse[2 * u], lse[2 * u + 1])
            return carry

        lax.fori_loop(0, n_tiles // D_UNROLL, tile_group, 0)

        for r in range(dil):
            dst = pl.ds(r, class_len, stride=dil) if dil > 1 else slice(None)
            src = slice(r * class_len, (r + 1) * class_len)
            og_ref[g, dst, :] = oc_ref[src, :]
            lse_ref[g, dst, :] = lc_ref[src, :]

    lse = [lse_ref[g] for g in range(D_GROUPS)]
    top = functools.reduce(jnp.maximum, lse)
    w = [jnp.exp(x - top) for x in lse]
    num = sum(w[g] * og_ref[g] for g in range(D_GROUPS))
    o_ref[...] = (num / sum(w)).astype(o_ref.dtype)


def _attn_d(p_d, bias_d, batch, s):
    n = p_d.shape[0]
    return pl.pallas_call(
        _attn_d_kernel,
        grid=(batch,),
        in_specs=([pl.BlockSpec((s, LANES), functools.partial(lambda b, c: (b, c), c=c))
                   for c in range(3 * D_GROUPS)]
                  + [pl.BlockSpec(bias_d.shape, lambda b: (0, 0, 0, 0, 0))]),
        out_specs=pl.BlockSpec((s, LANES), lambda b: (b, 0)),
        out_shape=jax.ShapeDtypeStruct((n, LANES), BF16),
        scratch_shapes=[pltpu.VMEM((s, LANES), BF16), pltpu.VMEM((s + D_T, LANES), BF16),
                        pltpu.VMEM((s + D_T, LANES), BF16),
                        pltpu.VMEM((s, LANES), F32), pltpu.VMEM((s, LANES), F32),
                        pltpu.VMEM((D_GROUPS, s, LANES), F32),
                        pltpu.VMEM((D_GROUPS, s, LANES), F32)],
        compiler_params=_cparams(1),
        name="attn_dilated",
    )(*([p_d] * (3 * D_GROUPS)), bias_d)


def _merge_kernel(x_ref, g_ref, wg_ref, gb_ref, oa_ref, ob_ref, oc_ref, od_ref,
                  wa_ref, wb_ref, wc_ref, wd_ref, wout_ref, o_ref):
    x = x_ref[...]
    h = _rms(x, g_ref[...]).astype(BF16)
    merged = jnp.zeros(x.shape, F32)
    branches = ((oa_ref, wa_ref), (ob_ref, wb_ref), (oc_ref, wc_ref), (od_ref, wd_ref))
    for i, (br_ref, w_ref) in enumerate(branches):
        cols = slice(i * D_MODEL, (i + 1) * D_MODEL)
        logits = jnp.dot(h, wg_ref[:, cols], preferred_element_type=F32) + gb_ref[:, cols]
        branch = jnp.dot(br_ref[...], w_ref[...], preferred_element_type=F32)
        merged = merged + jax.nn.sigmoid(logits) * branch
    o_ref[...] = x + jnp.dot(merged.astype(BF16), wout_ref[...], preferred_element_type=F32)


def _merge(x2, gain, w_all, layer, gate_bias, outs, w_branch, w_out):
    n = x2.shape[0]
    gate_w = N_BRANCH * D_MODEL
    widths = (A_W, B_W, C_W, 2 * HEAD_DIM)
    offs = np.concatenate([[0], np.cumsum(widths)])
    w_br = [w_branch[offs[i]:offs[i + 1]].astype(BF16) for i in range(N_BRANCH)]
    const = lambda i: (0, 0)
    row = lambda i: (i, 0)
    return pl.pallas_call(
        _merge_kernel,
        grid=(n // TOK_TILE,),
        in_specs=([pl.BlockSpec((TOK_TILE, D_MODEL), row),
                   pl.BlockSpec((1, D_MODEL), const),
                   pl.BlockSpec((None, D_MODEL, gate_w), lambda i: (layer, 0, W_GATE_OFF // gate_w),
                                pipeline_mode=pl.Buffered(1)),
                   pl.BlockSpec((1, gate_w), const)]
                  + [pl.BlockSpec((TOK_TILE, w), row) for w in widths]
                  + [pl.BlockSpec((w, D_MODEL), const) for w in widths]
                  + [pl.BlockSpec((D_MODEL, D_MODEL), const)]),
        out_specs=pl.BlockSpec((TOK_TILE, D_MODEL), row),
        out_shape=jax.ShapeDtypeStruct((n, D_MODEL), F32),
        compiler_params=_cparams(1),
        name="merge",
    )(x2, gain.reshape(1, D_MODEL), w_all, gate_bias.reshape(1, -1).astype(F32),
      *outs, *w_br, w_out.astype(BF16))


def kernel(x, rel_table, ffn1_norm, ffn1_w_in, ffn1_w_out, mix_norm, w_in, gate_bias, forget_bias,
           a_q_norm, a_k_norm, a_lambda, a_subln, c_q_norm, c_k_norm, d_q_norm, d_k_norm,
           w_branch, w_out, ffn2_norm, ffn2_w_in, ffn2_w_out):
    batch, s, _ = x.shape
    depth = w_in.shape[0]
    assert s % (D_T * D_PAIRS[-1][1]) == 0 and (batch * s) % TOK_TILE == 0
    bias_a, bias_d = _bias_tiles(rel_table)
    w_all = _prep_w_in(w_in)
    x2 = x.reshape(batch * s, D_MODEL)
    for l in range(depth):
        x2 = _ffn(x2, ffn1_norm[l], ffn1_w_in[l], ffn1_w_out[l])
        p_abc, p_d, cf = _proj(x2, mix_norm[l], w_all, l, a_q_norm[l], a_k_norm[l],
                               c_q_norm[l], c_k_norm[l], d_q_norm[l], d_k_norm[l])
        lam_init = 0.8 - 0.6 * math.exp(-0.3 * l)
        oa = _attn_a(p_abc, bias_a, a_lambda[l], a_subln[l], lam_init, batch, s)
        ob = _attn_b(p_abc, batch, s)
        oc = _attn_c(p_abc, cf, forget_bias[l], batch, s)
        od = _attn_d(p_d, bias_d, batch, s)
        x2 = _merge(x2, mix_norm[l], w_all, l, gate_bias[l], (oa, ob, oc, od),
                    w_branch[l], w_out[l])
        x2 = _ffn(x2, ffn2_norm[l], ffn2_w_in[l], ffn2_w_out[l])
    return x2.reshape(batch, s, D_MODEL)
```

```python
import functools
import math

import numpy as np
import jax
import jax.numpy as jnp
from jax import lax
from jax.experimental import pallas as pl
from jax.experimental.pallas import tpu as pltpu

F32 = jnp.float32
BF16 = jnp.bfloat16

D_MODEL = 1024
HEAD_DIM = 64
SCALE = HEAD_DIM ** -0.5
A_HEADS = 4
B_HEADS = 6
C_HEADS = 6
D_PAIRS = ((128, 1), (512, 4), (2048, 16))
D_GROUPS = len(D_PAIRS)
N_BRANCH = 4
FFN_HIDDEN = 2816
REL_BUCKETS = 32
REL_MAX_DIST = 128
RMS_EPS = 1e-6
NEG_INF = -1e30

LANES = 128
A_W = A_HEADS * 2 * HEAD_DIM
B_W = B_HEADS * HEAD_DIM
C_W = C_HEADS * HEAD_DIM
D_W = D_GROUPS * 2 * HEAD_DIM
ABC_W = 3 * (A_W + B_W + C_W)

W_SHIFT = (ABC_W + C_HEADS) % LANES
W_ALIGNED_BLOCKS = ABC_W // LANES + 1
W_D_OFF = W_ALIGNED_BLOCKS * LANES
W_D_END = W_D_OFF + 3 * D_W
W_GATE_OFF = 2 * N_BRANCH * D_MODEL
GROUP_W = 256

TOK_TILE = 512
FFN_CHUNK = 256
ATT_T = 256
ATT_WIN = 2 * ATT_T
LOG2E = math.log2(math.e)
D_UNROLL = 4
D_T = 128
VMEM_LIMIT = 56 * 1024 * 1024


def _cparams(n_axes):
    return pltpu.CompilerParams(dimension_semantics=("arbitrary",) * n_axes,
                                vmem_limit_bytes=VMEM_LIMIT)


def _rms(x, g):
    ms = jnp.mean(x * x, axis=-1, keepdims=True)
    return x * lax.rsqrt(ms + RMS_EPS) * g


def _split3(a):
    hi = a.astype(BF16)
    r1 = a - hi.astype(F32)
    mid = r1.astype(BF16)
    lo = (r1 - mid.astype(F32)).astype(BF16)
    return hi, mid, lo


def _dot_exact_lhs(a_bf16, b):
    hi, mid, lo = _split3(b)
    d = functools.partial(jnp.dot, preferred_element_type=F32)
    return d(a_bf16, hi) + d(a_bf16, mid) + d(a_bf16, lo)


def _qk(q, k):
    return lax.dot_general(q, k, (((1,), (1,)), ((), ())), preferred_element_type=F32)


def _log_sigmoid(x):
    return jnp.minimum(x, 0.0) - jnp.log(1.0 + jnp.exp(-jnp.abs(x)))


def _lane_half(shape):
    return lax.broadcasted_iota(jnp.int32, shape, len(shape) - 1) < HEAD_DIM


def _ffn_kernel(x_ref, g_ref, wi_ref, wo_ref, o_ref):
    x = x_ref[...]
    h = _rms(x, g_ref[...]).astype(BF16)
    acc = jnp.zeros(x.shape, F32)
    for c in range(FFN_HIDDEN // FFN_CHUNK):
        lo = c * FFN_CHUNK
        gate = jnp.dot(h, wi_ref[:, lo:lo + FFN_CHUNK], preferred_element_type=F32)
        up = jnp.dot(h, wi_ref[:, FFN_HIDDEN + lo:FFN_HIDDEN + lo + FFN_CHUNK],
                     preferred_element_type=F32)
        a = (gate * jax.nn.sigmoid(gate) * up).astype(BF16)
        acc = acc + jnp.dot(a, wo_ref[c * FFN_CHUNK:(c + 1) * FFN_CHUNK, :],
                            preferred_element_type=F32)
    o_ref[...] = x + 0.5 * acc


def _ffn(x2, gain, w_i, w_o):
    n = x2.shape[0]
    const = lambda i: (0, 0)
    return pl.pallas_call(
        _ffn_kernel,
        grid=(n // TOK_TILE,),
        in_specs=[
            pl.BlockSpec((TOK_TILE, D_MODEL), lambda i: (i, 0)),
            pl.BlockSpec((1, D_MODEL), const),
            pl.BlockSpec((D_MODEL, 2 * FFN_HIDDEN), const, pipeline_mode=pl.Buffered(1)),
            pl.BlockSpec((FFN_HIDDEN, D_MODEL), const, pipeline_mode=pl.Buffered(1)),
        ],
        out_specs=pl.BlockSpec((TOK_TILE, D_MODEL), lambda i: (i, 0)),
        out_shape=jax.ShapeDtypeStruct((n, D_MODEL), F32),
        compiler_params=_cparams(1),
        name="ffn",
    )(x2, gain.reshape(1, D_MODEL), w_i.astype(BF16), w_o.astype(BF16))


_ABC_SECTIONS = ((A_W, True), (A_W, True), (A_W, False),
                 (B_W, False), (B_W, False), (B_W, False),
                 (C_W, True), (C_W, True), (C_W, False))
_D_SECTIONS = ((D_W, True), (D_W, True), (D_W, False))


def _proj_sections(h, w_ref, w_off, gain_ref, gmat, out_ref, sections):
    off = 0
    for width, normed in sections:
        y = jnp.dot(h, w_ref[:, w_off + off:w_off + off + width], preferred_element_type=F32)
        lo = 0
        while lo < width:
            blk = min(GROUP_W, width - lo)
            yb = y[:, lo:lo + blk]
            if normed:
                ms = jnp.dot((yb * yb).astype(BF16), gmat[:blk, :blk], preferred_element_type=F32)
                yb = yb * lax.rsqrt(ms + RMS_EPS)
            gain = gain_ref[:, off + lo:off + lo + blk]
            out_ref[:, off + lo:off + lo + blk] = (yb * gain).astype(out_ref.dtype)
            lo += blk
        off += width


def _proj_kernel(x_ref, g_ref, w_ref, gabc_ref, gd_ref, gmat_ref, pabc_ref, pd_ref, cf_ref):
    h = _rms(x_ref[...], g_ref[...]).astype(BF16)
    gmat = gmat_ref[...]
    _proj_sections(h, w_ref, 0, gabc_ref, gmat, pabc_ref, _ABC_SECTIONS)
    _proj_sections(h, w_ref, W_D_OFF, gd_ref, gmat, pd_ref, _D_SECTIONS)
    cf_ref[...] = jnp.dot(h, w_ref[:, ABC_W:ABC_W + LANES], preferred_element_type=F32)


def _head_gain(gain, width, scale):
    return jnp.tile(gain.astype(F32), width // HEAD_DIM) * scale


def _win_kernel(a_ref, b_ref, o_ref):
    c = pl.program_id(1)

    @pl.when(c < W_ALIGNED_BLOCKS)
    def _():
        o_ref[...] = b_ref[...].astype(BF16)

    @pl.when(c >= W_ALIGNED_BLOCKS)
    def _():
        both = jnp.concatenate([a_ref[...], b_ref[...]], axis=1)
        o_ref[...] = both[:, W_SHIFT:W_SHIFT + LANES].astype(BF16)


def _prep_w_in(w_in):
    depth = w_in.shape[0]
    n_blocks = pl.cdiv(w_in.shape[2], LANES)
    gate_block0, gate_skip = W_D_END // LANES, (W_GATE_OFF - W_D_END) // LANES
    return pl.pallas_call(
        _win_kernel,
        grid=(depth, n_blocks),
        in_specs=[pl.BlockSpec((None, D_MODEL, LANES), lambda l, c: (l, 0, jnp.maximum(c - 1, 0))),
                  pl.BlockSpec((None, D_MODEL, LANES), lambda l, c: (l, 0, c))],
        out_specs=pl.BlockSpec((None, D_MODEL, LANES),
                               lambda l, c: (l, 0, jnp.where(c < gate_block0, c, c + gate_skip))),
        out_shape=jax.ShapeDtypeStruct((depth, D_MODEL, W_GATE_OFF + N_BRANCH * D_MODEL), BF16),
        compiler_params=_cparams(2),
        name="prep_w_in",
    )(w_in, w_in)


def _proj(x2, gain, w_all, layer, a_q, a_k, c_q, c_k, d_q, d_k):
    n = x2.shape[0]
    ones = lambda w, s: jnp.full((w,), s, F32)
    g_abc = jnp.concatenate([
        _head_gain(a_q, A_W, SCALE * LOG2E), _head_gain(a_k, A_W, 1.0), ones(A_W, 1.0),
        ones(B_W, SCALE * LOG2E), ones(B_W, 1.0), ones(B_W, 1.0),
        _head_gain(c_q, C_W, SCALE * LOG2E), _head_gain(c_k, C_W, 1.0),
        ones(C_W, 1.0)]).reshape(1, ABC_W)
    g_d = jnp.concatenate([_head_gain(d_q, D_W, SCALE), _head_gain(d_k, D_W, 1.0),
                           ones(D_W, 1.0)]).reshape(1, 3 * D_W)
    lane = np.arange(GROUP_W)
    gmat = jnp.asarray((lane[:, None] // HEAD_DIM == lane[None, :] // HEAD_DIM) / HEAD_DIM, BF16)
    const = lambda i: (0, 0)
    row = lambda i: (i, 0)
    return pl.pallas_call(
        _proj_kernel,
        grid=(n // TOK_TILE,),
        in_specs=[
            pl.BlockSpec((TOK_TILE, D_MODEL), row),
            pl.BlockSpec((1, D_MODEL), const),
            pl.BlockSpec((None, D_MODEL, W_D_END), lambda i: (layer, 0, 0),
                         pipeline_mode=pl.Buffered(1)),
            pl.BlockSpec((1, ABC_W), const),
            pl.BlockSpec((1, 3 * D_W), const),
            pl.BlockSpec((GROUP_W, GROUP_W), const),
        ],
        out_specs=[
            pl.BlockSpec((TOK_TILE, ABC_W), row),
            pl.BlockSpec((TOK_TILE, 3 * D_W), row),
            pl.BlockSpec((TOK_TILE, LANES), row),
        ],
        out_shape=[
            jax.ShapeDtypeStruct((n, ABC_W), BF16),
            jax.ShapeDtypeStruct((n, 3 * D_W), F32),
            jax.ShapeDtypeStruct((n, LANES), F32),
        ],
        compiler_params=_cparams(1),
        name="proj",
    )(x2, gain.reshape(1, D_MODEL), w_all, g_abc, g_d, gmat)


_MASKED_BUCKET = REL_BUCKETS


def _rel_bucket_np(n):
    n = np.asarray(n, np.int64)
    max_exact = REL_BUCKETS // 2
    nf = np.maximum(n, 1).astype(np.float32)
    large = max_exact + (np.log(nf / np.float32(max_exact)) / np.float32(math.log(REL_MAX_DIST / max_exact))
                         * np.float32(REL_BUCKETS - max_exact)).astype(np.int32)
    large = np.minimum(large, REL_BUCKETS - 1)
    return np.where(n < max_exact, n, large).astype(np.int32)


def _bucket_maps():
    t = ATT_T
    x, y = np.arange(t)[:, None], np.arange(t)[None, :]
    diag = np.where(x >= y, _rel_bucket_np(np.maximum(x - y, 0)), _MASKED_BUCKET)
    prev = _rel_bucket_np(t + x - y)
    idx_a = np.stack([prev.T, diag.T]).astype(np.int32)
    x, y = np.arange(D_T)[:, None], np.arange(D_T)[None, :]
    idx_d = np.empty((D_GROUPS, 2, D_T, 2 * D_T), np.int32)
    for g, (_, dil) in enumerate(D_PAIRS):
        prev = np.where(y >= x, _rel_bucket_np(dil * (D_T + x - y)), _MASKED_BUCKET)
        diag = np.where(y <= x, _rel_bucket_np(dil * np.maximum(x - y, 0)), _MASKED_BUCKET)
        idx_d[g, 0] = np.concatenate([prev, diag], axis=1)
        idx_d[g, 1] = np.concatenate([np.full_like(prev, _MASKED_BUCKET), diag], axis=1)
    return idx_a, idx_d


def _bias_kernel(tbl_ref, idxa_ref, idxd_ref, ba_ref, bd_ref):
    for h in range(A_HEADS):
        for kind in range(2):
            ba_ref[h, kind] = jnp.where(idxa_ref[kind] == _MASKED_BUCKET, NEG_INF, 0.0)
        ba_ref[h, 2] = jnp.zeros((ATT_T, ATT_T), F32)
        ba_ref[h, 3] = jnp.full((ATT_T, ATT_T), NEG_INF, F32)
    for g in range(D_GROUPS):
        for h in range(2):
            for var in range(2):
                bd_ref[g, h, var] = jnp.where(idxd_ref[g, var] == _MASKED_BUCKET, NEG_INF, 0.0)

    def body(b, carry):
        for h in range(A_HEADS):
            val = (tbl_ref[b, h] - tbl_ref[REL_BUCKETS - 1, h]) * LOG2E
            for kind in range(2):
                ba_ref[h, kind] = jnp.where(idxa_ref[kind] == b, val, ba_ref[h, kind])
        for g in range(D_GROUPS):
            for h in range(2):
                val = tbl_ref[b, A_HEADS + 2 * g + h]
                for var in range(2):
                    bd_ref[g, h, var] = jnp.where(idxd_ref[g, var] == b, val, bd_ref[g, h, var])
        return carry

    lax.fori_loop(0, REL_BUCKETS, body, 0)


def _bias_tiles(rel_table):
    idx_a, idx_d = _bucket_maps()
    return pl.pallas_call(
        _bias_kernel,
        in_specs=[pl.BlockSpec(memory_space=pltpu.SMEM),
                  pl.BlockSpec(memory_space=pltpu.VMEM),
                  pl.BlockSpec(memory_space=pltpu.VMEM)],
        out_specs=[pl.BlockSpec(memory_space=pltpu.VMEM),
                   pl.BlockSpec(memory_space=pltpu.VMEM)],
        out_shape=[jax.ShapeDtypeStruct((A_HEADS, 4, ATT_T, ATT_T), F32),
                   jax.ShapeDtypeStruct((D_GROUPS, 2, 2, D_T, 2 * D_T), F32)],
        compiler_params=pltpu.CompilerParams(vmem_limit_bytes=VMEM_LIMIT),
        name="rel_bias",
    )(rel_table.astype(F32), jnp.asarray(idx_a), jnp.asarray(idx_d))


def _rows(idx, size):
    return pl.ds(pl.multiple_of(idx * size, size), size)


def _split_heads(q):
    first = _lane_half((q.shape[0], LANES))
    zero = jnp.zeros((q.shape[0], LANES), q.dtype)
    out = []
    for p in range(q.shape[1] // LANES):
        qp = q[:, p * LANES:(p + 1) * LANES]
        out += [jnp.where(first, qp, zero), jnp.where(first, zero, qp)]
    return out


def _transpose_into(vt_ref, v_ref):
    for p in range(v_ref.shape[1] // LANES):
        for blk in range(v_ref.shape[0] // ATT_T):
            tile = v_ref[blk * ATT_T:(blk + 1) * ATT_T, p * LANES:(p + 1) * LANES]
            vt_ref[p * LANES:(p + 1) * LANES, blk * ATT_T:(blk + 1) * ATT_T] = (
                tile.astype(F32).T.astype(vt_ref.dtype))


def _attn_specs(width, col0, s):
    nq = s // ATT_T
    q_spec = pl.BlockSpec((ATT_T, width), lambda b, i: (b * nq + i, col0))
    k_spec = pl.BlockSpec((s, width), lambda b, i: (b, col0 + 1))
    v_spec = pl.BlockSpec((s, width), lambda b, i: (b, col0 + 2))
    o_spec = pl.BlockSpec((ATT_T, width), lambda b, i: (b * nq + i, 0))
    return q_spec, k_spec, v_spec, o_spec


def _window_split(qi):
    return lax.shift_right_logical(qi + 1, 1), jnp.bitwise_and(qi, 1)


def _softmax_stages_t(scores, ms, acc_ref, vts, ls=None):
    n = len(scores)
    m_new = [jnp.maximum(ms[c], jnp.max(scores[c], axis=0, keepdims=True)) for c in range(n)]
    alpha = [jnp.exp2(ms[c] - m_new[c]) for c in range(n)]
    p_bf16, l_new = [], []
    for c in range(n):
        p = jnp.exp2(scores[c] - m_new[c])
        p_bf16.append(p.astype(BF16))
        if ls is not None:
            l_new.append(alpha[c] * ls[c] + jnp.sum(p, axis=0, keepdims=True))
    pv = [jnp.dot(vts[c], p_bf16[c], preferred_element_type=F32) for c in range(n)]
    for c in range(n):
        acc_ref[c] = alpha[c] * acc_ref[c] + pv[c]
    return m_new, (l_new if ls is not None else None)


def _attn_a_kernel(q_ref, k_ref, v_ref, bias_ref, lam_ref, subln_ref, o_ref, acc_ref, vt_ref, *,
                   lam_init):
    qi = pl.program_id(1)
    s_len = k_ref.shape[0]

    @pl.when(qi == 0)
    def _():
        _transpose_into(vt_ref, v_ref)

    qs = _split_heads(q_ref[...])
    n_chain = len(qs)
    acc_ref[...] = jnp.zeros(acc_ref.shape, F32)
    stat = lambda val: jnp.full((1, ATT_T), val, F32)
    init = (stat(NEG_INF), stat(0.0)) * n_chain
    n_full, odd = _window_split(qi)

    def window(rows, carry, kinds):
        scores, vts = [], []
        for h in range(A_HEADS):
            cols = slice(h * LANES, (h + 1) * LANES)
            k, vt = k_ref[rows, cols], vt_ref[cols, rows]
            if kinds is not None:
                bias = jnp.concatenate([bias_ref[h, kind] for kind in kinds], axis=0)
            for c in (2 * h, 2 * h + 1):
                s = _qk(k, qs[c])
                scores.append(s if kinds is None else s + bias)
                vts.append(vt)
        ms, ls = _softmax_stages_t(scores, carry[0::2], acc_ref, vts, ls=carry[1::2])
        return tuple(x for pair in zip(ms, ls) for x in pair)

    n_plain = jnp.maximum(n_full - 1, 0)
    carry = lax.fori_loop(0, n_plain, lambda j, c: window(_rows(j, ATT_WIN), c, None), init)
    near = (jnp.where(odd == 1, 0, 2), jnp.where(odd == 1, 1, 0))
    carry = lax.fori_loop(n_plain, n_full, lambda j, c: window(_rows(j, ATT_WIN), c, near), carry)
    carry = lax.fori_loop(0, 1 - odd, lambda _, c: window(_rows(qi, ATT_T), c, (1,)), carry)

    lv = lam_ref[...]
    lam = (jnp.exp(jnp.sum(lv[0:1] * lv[1:2], axis=-1, keepdims=True))
           - jnp.exp(jnp.sum(lv[2:3] * lv[3:4], axis=-1, keepdims=True)) + lam_init)
    for h in range(A_HEADS):
        l0, l1 = carry[4 * h + 1], carry[4 * h + 3]
        o = acc_ref[2 * h] / l0 - lam * (acc_ref[2 * h + 1] / l1)
        ms = jnp.mean(o * o, axis=0, keepdims=True)
        o = o * lax.rsqrt(ms + RMS_EPS) * subln_ref[...] * (1.0 - lam_init)
        o_ref[:, h * LANES:(h + 1) * LANES] = o.T.astype(o_ref.dtype)


def _attn_a(p_abc, bias_a, lam_vecs, subln, lam_init, batch, s):
    n = p_abc.shape[0]
    q_spec, k_spec, v_spec, o_spec = _attn_specs(A_W, 0, s)
    return pl.pallas_call(
        functools.partial(_attn_a_kernel, lam_init=lam_init),
        grid=(batch, s // ATT_T),
        in_specs=[q_spec, k_spec, v_spec,
                  pl.BlockSpec(bias_a.shape, lambda b, i: (0, 0, 0, 0)),
                  pl.BlockSpec((4, HEAD_DIM), lambda b, i: (0, 0)),
                  pl.BlockSpec((2 * HEAD_DIM, ATT_T), lambda b, i: (0, 0))],
        out_specs=o_spec,
        out_shape=jax.ShapeDtypeStruct((n, A_W), BF16),
        scratch_shapes=[pltpu.VMEM((2 * A_HEADS, LANES, ATT_T), F32),
                        pltpu.VMEM((A_W, s), BF16)],
        compiler_params=_cparams(2),
        name="attn_diff",
    )(p_abc, p_abc, p_abc, bias_a, lam_vecs.astype(F32),
      jnp.broadcast_to(subln.astype(F32)[:, None], (2 * HEAD_DIM, ATT_T)))


def _attn_b_kernel(q_ref, k_ref, v_ref, o_ref, acc_ref, vt_ref):
    qi = pl.program_id(1)

    @pl.when(qi == 0)
    def _():
        _transpose_into(vt_ref, v_ref)

    qs = _split_heads(q_ref[...])
    acc_ref[...] = jnp.zeros(acc_ref.shape, F32)
    key = lax.broadcasted_iota(jnp.int32, (ATT_T, ATT_T), 0)
    qry = lax.broadcasted_iota(jnp.int32, (ATT_T, ATT_T), 1)
    later = jnp.where(qry > key, 1.0, 0.0).astype(BF16)
    strict = key < qry

    def step(kj, rsum, diag):
        rows = _rows(kj, ATT_T)
        heads = range(B_HEADS)
        cols = [slice((h // 2) * LANES, (h // 2 + 1) * LANES) for h in heads]
        z = [_qk(k_ref[rows, cols[h]], qs[h]) for h in heads]
        ls = [jnp.minimum(z[h], 0.0) - jnp.log2(1.0 + jnp.exp2(-jnp.abs(z[h]))) for h in heads]
        u = [ls[h] - z[h] for h in heads]
        if diag:
            u = [jnp.where(strict, u[h], 0.0) for h in heads]
        tail = [jnp.dot(later, u[h].astype(BF16), preferred_element_type=F32) + rsum[h] for h in heads]
        a = [jnp.exp2(ls[h] + tail[h]) for h in heads]
        if diag:
            a = [jnp.where(strict, a[h], 0.0) for h in heads]
        pv = [jnp.dot(vt_ref[cols[h], rows], a[h].astype(BF16), preferred_element_type=F32)
              for h in heads]
        for h in heads:
            acc_ref[h] = acc_ref[h] + pv[h]
        return tuple(rsum[h] + jnp.sum(u[h], axis=0, keepdims=True) for h in heads)

    zero = jnp.zeros((1, ATT_T), F32)
    rsum = step(qi, (zero,) * B_HEADS, True)
    lax.fori_loop(0, qi, lambda t, c: step(qi - 1 - t, c, False), rsum)
    for p in range(B_HEADS // 2):
        o_t = jnp.concatenate([acc_ref[2 * p][:HEAD_DIM], acc_ref[2 * p + 1][HEAD_DIM:]], axis=0)
        o_ref[:, p * LANES:(p + 1) * LANES] = o_t.T.astype(o_ref.dtype)


def _attn_b(p_abc, batch, s):
    n = p_abc.shape[0]
    q_spec, k_spec, v_spec, o_spec = _attn_specs(B_W, 3 * A_W // B_W, s)
    return pl.pallas_call(
        _attn_b_kernel,
        grid=(batch, s // ATT_T),
        in_specs=[q_spec, k_spec, v_spec],
        out_specs=o_spec,
        out_shape=jax.ShapeDtypeStruct((n, B_W), BF16),
        scratch_shapes=[pltpu.VMEM((B_HEADS, LANES, ATT_T), F32),
                        pltpu.VMEM((B_W, s), BF16)],
        compiler_params=_cparams(2),
        name="attn_stick",
    )(p_abc, p_abc, p_abc)


def _attn_c_kernel(q_ref, k_ref, v_ref, cf_ref, fb_ref, o_ref, acc_ref, cumk_ref, cumr_ref, vaug_ref):
    qi = pl.program_id(1)
    s_len = cf_ref.shape[0]

    @pl.when(qi == 0)
    def _():
        r = lax.broadcasted_iota(jnp.int32, (LANES, LANES), 0)
        c = lax.broadcasted_iota(jnp.int32, (LANES, LANES), 1)
        lower = jnp.where(c <= r, 1.0, 0.0).astype(BF16)
        carry = jnp.zeros((1, LANES), F32)
        for blk in range(s_len // LANES):
            rows = slice(blk * LANES, (blk + 1) * LANES)
            log_f = _log_sigmoid(cf_ref[rows, :] + fb_ref[...])
            cum = _dot_exact_lhs(lower, log_f) + carry
            cumr_ref[:, rows] = (cum * LOG2E).T
            for h in range(C_HEADS):
                cumk_ref[h, rows, :] = jnp.broadcast_to(cum[:, h:h + 1] * LOG2E, (LANES, LANES))
            carry = cum[LANES - 1:LANES, :]
        own = lax.broadcasted_iota(jnp.int32, (LANES, ATT_T), 0) < HEAD_DIM
        for p in range(C_HEADS // 2):
            for blk in range(s_len // ATT_T):
                tile = v_ref[blk * ATT_T:(blk + 1) * ATT_T, p * LANES:(p + 1) * LANES]
                tile_t = tile.astype(F32).T
                dst = (slice(p * LANES, (p + 1) * LANES), slice(blk * ATT_T, (blk + 1) * ATT_T))
                vaug_ref[(0,) + dst] = jnp.where(own, tile_t, 1.0).astype(BF16)
                vaug_ref[(1,) + dst] = jnp.where(own, 1.0, tile_t).astype(BF16)

    qs = _split_heads(q_ref[...])
    acc_ref[...] = jnp.zeros(acc_ref.shape, F32)
    q_cols = _rows(qi, ATT_T)
    cum_q = [cumr_ref[h:h + 1, q_cols] for h in range(C_HEADS)]
    n_full, odd = _window_split(qi)

    def causal(width):
        key = lax.broadcasted_iota(jnp.int32, (width, ATT_T), 0) - (width - ATT_T)
        return jnp.where(key <= lax.broadcasted_iota(jnp.int32, (width, ATT_T), 1), 0.0, NEG_INF)

    def window(rows, carry, mask):
        scores, vts = [], []
        for h in range(C_HEADS):
            cols = slice((h // 2) * LANES, (h // 2 + 1) * LANES)
            cum_k = cumk_ref[h, rows, :]
            s = _qk(k_ref[rows, cols], qs[h]) + cum_q[h] - jnp.concatenate([cum_k, cum_k], axis=1)
            scores.append(s if mask is None else s + mask)
            vts.append(vaug_ref[h % 2, cols, rows])
        return tuple(_softmax_stages_t(scores, carry, acc_ref, vts)[0])

    init = (jnp.full((1, ATT_T), NEG_INF, F32),) * C_HEADS
    carry = lax.fori_loop(0, n_full - odd, lambda j, c: window(_rows(j, ATT_WIN), c, None), init)
    carry = lax.fori_loop(n_full - odd, n_full,
                          lambda j, c: window(_rows(j, ATT_WIN), c, causal(ATT_WIN)), carry)
    lax.fori_loop(0, 1 - odd, lambda _, c: window(_rows(qi, ATT_T), c, causal(ATT_T)), carry)
    for p in range(C_HEADS // 2):
        a0, a1 = acc_ref[2 * p], acc_ref[2 * p + 1]
        num = jnp.concatenate([a0[:HEAD_DIM], a1[HEAD_DIM:]], axis=0)
        den = jnp.concatenate([a0[HEAD_DIM:], a1[:HEAD_DIM]], axis=0)
        o_ref[:, p * LANES:(p + 1) * LANES] = (num / den).T.astype(o_ref.dtype)


def _attn_c(p_abc, cf, forget_bias, batch, s):
    n = p_abc.shape[0]
    q_spec, k_spec, v_spec, o_spec = _attn_specs(C_W, 3 * (A_W + B_W) // C_W, s)
    fb = jnp.pad(forget_bias.astype(F32), (0, LANES - C_HEADS)).reshape(1, LANES)
    return pl.pallas_call(
        _attn_c_kernel,
        grid=(batch, s // ATT_T),
        in_specs=[q_spec, k_spec, v_spec,
                  pl.BlockSpec((s, LANES), lambda b, i: (b, 0)),
                  pl.BlockSpec((1, LANES), lambda b, i: (0, 0))],
        out_specs=o_spec,
        out_shape=jax.ShapeDtypeStruct((n, C_W), BF16),
        scratch_shapes=[pltpu.VMEM((C_HEADS, LANES, ATT_T), F32),
                        pltpu.VMEM((C_HEADS, s, LANES), F32),
                        pltpu.VMEM((LANES, s), F32),
                        pltpu.VMEM((2, C_W, s), BF16)],
        compiler_params=_cparams(2),
        name="attn_forget",
    )(p_abc, p_abc, p_abc, cf, fb)


def _attn_d_kernel(*refs):
    qkv_refs, (bias_ref, o_ref, qc_ref, kc_ref, vc_ref, oc_ref, lc_ref, og_ref, lse_ref) = (
        refs[:3 * D_GROUPS], refs[3 * D_GROUPS:])
    s_len = o_ref.shape[0]
    n_tiles = s_len // D_T
    first = _lane_half((D_T, LANES))
    kc_ref[0:D_T, :] = jnp.zeros((D_T, LANES), BF16)
    vc_ref[0:D_T, :] = jnp.zeros((D_T, LANES), BF16)

    for g, (_, dil) in enumerate(D_PAIRS):
        q_ref, k_ref, v_ref = qkv_refs[g], qkv_refs[D_GROUPS + g], qkv_refs[2 * D_GROUPS + g]
        class_len = s_len // dil
        tiles_per_class = class_len // D_T

        for r in range(dil):
            src = pl.ds(r, class_len, stride=dil) if dil > 1 else slice(None)
            dst = slice(r * class_len, (r + 1) * class_len)
            dst_kv = slice(D_T + r * class_len, D_T + (r + 1) * class_len)
            qc_ref[dst, :] = q_ref[src, :].astype(BF16)
            kc_ref[dst_kv, :] = k_ref[src, :].astype(BF16)
            vc_ref[dst_kv, :] = v_ref[src, :].astype(BF16)

        def tile_group(i, carry):
            tiles = [i * D_UNROLL + u for u in range(D_UNROLL)]
            rows = [pl.ds(pl.multiple_of(t * D_T, D_T), D_T) for t in tiles]
            if tiles_per_class > 1:
                keys = [pl.ds(pl.multiple_of(t * D_T, D_T), 2 * D_T) for t in tiles]
                variant = [jnp.where(t % tiles_per_class == 0, 1, 0) for t in tiles]
                bias = [[bias_ref[g, h, variant[u]] for h in range(2)] for u in range(D_UNROLL)]
            else:
                keys = [pl.ds(pl.multiple_of(t * D_T + D_T, D_T), D_T) for t in tiles]
                bias = [[bias_ref[g, h, 1][:, D_T:] for h in range(2)]] * D_UNROLL
            chains = [(u, h) for u in range(D_UNROLL) for h in range(2)]
            qh = [_split_heads(qc_ref[rows[u], :]) for u in range(D_UNROLL)]
            sc = [_qk(qh[u][h], kc_ref[keys[u], :]) + bias[u][h] for u, h in chains]
            m = [jnp.max(s, axis=-1, keepdims=True) for s in sc]
            e = [jnp.exp(s - mm) for s, mm in zip(sc, m)]
            l = [jnp.sum(x, axis=-1, keepdims=True) for x in e]
            o = [jnp.dot(x.astype(BF16), vc_ref[keys[u], :], preferred_element_type=F32)
                 for x, (u, h) in zip(e, chains)]
            o = [x / ll for x, ll in zip(o, l)]
            lse = [jnp.broadcast_to(mm + jnp.log(ll), (D_T, LANES)) for mm, ll in zip(m, l)]
            for u in range(D_UNROLL):
                oc_ref[rows[u], :] = jnp.where(first, o[2 * u], o[2 * u + 1])
                lc_ref[rows[u], :] = jnp.where(first, lse[2 * u], lse[2 * u + 1])
            return carry

        lax.fori_loop(0, n_tiles // D_UNROLL, tile_group, 0)

        for r in range(dil):
            dst = pl.ds(r, class_len, stride=dil) if dil > 1 else slice(None)
            src = slice(r * class_len, (r + 1) * class_len)
            og_ref[g, dst, :] = oc_ref[src, :]
            lse_ref[g, dst, :] = lc_ref[src, :]

    lse = [lse_ref[g] for g in range(D_GROUPS)]
    top = functools.reduce(jnp.maximum, lse)
    w = [jnp.exp(x - top) for x in lse]
    num = sum(w[g] * og_ref[g] for g in range(D_GROUPS))
    o_ref[...] = (num / sum(w)).astype(o_ref.dtype)


def _attn_d(p_d, bias_d, batch, s):
    n = p_d.shape[0]
    return pl.pallas_call(
        _attn_d_kernel,
        grid=(batch,),
        in_specs=([pl.BlockSpec((s, LANES), functools.partial(lambda b, c: (b, c), c=c))
                   for c in range(3 * D_GROUPS)]
                  + [pl.BlockSpec(bias_d.shape, lambda b: (0, 0, 0, 0, 0))]),
        out_specs=pl.BlockSpec((s, LANES), lambda b: (b, 0)),
        out_shape=jax.ShapeDtypeStruct((n, LANES), BF16),
        scratch_shapes=[pltpu.VMEM((s, LANES), BF16), pltpu.VMEM((s + D_T, LANES), BF16),
                        pltpu.VMEM((s + D_T, LANES), BF16),
                        pltpu.VMEM((s, LANES), F32), pltpu.VMEM((s, LANES), F32),
                        pltpu.VMEM((D_GROUPS, s, LANES), F32),
                        pltpu.VMEM((D_GROUPS, s, LANES), F32)],
        compiler_params=_cparams(1),
        name="attn_dilated",
    )(*([p_d] * (3 * D_GROUPS)), bias_d)


def _merge_kernel(x_ref, g_ref, wg_ref, gb_ref, oa_ref, ob_ref, oc_ref, od_ref,
                  wa_ref, wb_ref, wc_ref, wd_ref, wout_ref, o_ref):
    x = x_ref[...]
    h = _rms(x, g_ref[...]).astype(BF16)
    merged = jnp.zeros(x.shape, F32)
    branches = ((oa_ref, wa_ref), (ob_ref, wb_ref), (oc_ref, wc_ref), (od_ref, wd_ref))
    for i, (br_ref, w_ref) in enumerate(branches):
        cols = slice(i * D_MODEL, (i + 1) * D_MODEL)
        logits = jnp.dot(h, wg_ref[:, cols], preferred_element_type=F32) + gb_ref[:, cols]
        branch = jnp.dot(br_ref[...], w_ref[...], preferred_element_type=F32)
        merged = merged + jax.nn.sigmoid(logits) * branch
    o_ref[...] = x + jnp.dot(merged.astype(BF16), wout_ref[...], preferred_element_type=F32)


def _merge(x2, gain, w_all, layer, gate_bias, outs, w_branch, w_out):
    n = x2.shape[0]
    gate_w = N_BRANCH * D_MODEL
    widths = (A_W, B_W, C_W, 2 * HEAD_DIM)
    offs = np.concatenate([[0], np.cumsum(widths)])
    w_br = [w_branch[offs[i]:offs[i + 1]].astype(BF16) for i in range(N_BRANCH)]
    const = lambda i: (0, 0)
    row = lambda i: (i, 0)
    return pl.pallas_call(
        _merge_kernel,
        grid=(n // TOK_TILE,),
        in_specs=([pl.BlockSpec((TOK_TILE, D_MODEL), row),
                   pl.BlockSpec((1, D_MODEL), const),
                   pl.BlockSpec((None, D_MODEL, gate_w), lambda i: (layer, 0, W_GATE_OFF // gate_w),
                                pipeline_mode=pl.Buffered(1)),
                   pl.BlockSpec((1, gate_w), const)]
                  + [pl.BlockSpec((TOK_TILE, w), row) for w in widths]
                  + [pl.BlockSpec((w, D_MODEL), const) for w in widths]
                  + [pl.BlockSpec((D_MODEL, D_MODEL), const)]),
        out_specs=pl.BlockSpec((TOK_TILE, D_MODEL), row),
        out_shape=jax.ShapeDtypeStruct((n, D_MODEL), F32),
        compiler_params=_cparams(1),
        name="merge",
    )(x2, gain.reshape(1, D_MODEL), w_all, gate_bias.reshape(1, -1).astype(F32),
      *outs, *w_br, w_out.astype(BF16))


def kernel(x, rel_table, ffn1_norm, ffn1_w_in, ffn1_w_out, mix_norm, w_in, gate_bias, forget_bias,
           a_q_norm, a_k_norm, a_lambda, a_subln, c_q_norm, c_k_norm, d_q_norm, d_k_norm,
           w_branch, w_out, ffn2_norm, ffn2_w_in, ffn2_w_out):
    batch, s, _ = x.shape
    depth = w_in.shape[0]
    assert s % (D_T * D_PAIRS[-1][1]) == 0 and (batch * s) % TOK_TILE == 0
    bias_a, bias_d = _bias_tiles(rel_table)
    w_all = _prep_w_in(w_in)
    x2 = x.reshape(batch * s, D_MODEL)
    for l in range(depth):
        x2 = _ffn(x2, ffn1_norm[l], ffn1_w_in[l], ffn1_w_out[l])
        p_abc, p_d, cf = _proj(x2, mix_norm[l], w_all, l, a_q_norm[l], a_k_norm[l],
                               c_q_norm[l], c_k_norm[l], d_q_norm[l], d_k_norm[l])
        lam_init = 0.8 - 0.6 * math.exp(-0.3 * l)
        oa = _attn_a(p_abc, bias_a, a_lambda[l], a_subln[l], lam_init, batch, s)
        ob = _attn_b(p_abc, batch, s)
        oc = _attn_c(p_abc, cf, forget_bias[l], batch, s)
        od = _attn_d(p_d, bias_d, batch, s)
        x2 = _merge(x2, mix_norm[l], w_all, l, gate_bias[l], (oa, ob, oc, od),
                    w_branch[l], w_out[l])
        x2 = _ffn(x2, ffn2_norm[l], ffn2_w_in[l], ffn2_w_out[l])
    return x2.reshape(batch, s, D_MODEL)
```

```python
import functools
import math

import numpy as np
import jax
import jax.numpy as jnp
from jax import lax
from jax.experimental import pallas as pl
from jax.experimental.pallas import tpu as pltpu

F32 = jnp.float32
BF16 = jnp.bfloat16

D_MODEL = 1024
HEAD_DIM = 64
SCALE = HEAD_DIM ** -0.5
A_HEADS = 4
B_HEADS = 6
C_HEADS = 6
D_PAIRS = ((128, 1), (512, 4), (2048, 16))
D_GROUPS = len(D_PAIRS)
N_BRANCH = 4
FFN_HIDDEN = 2816
REL_BUCKETS = 32
REL_MAX_DIST = 128
RMS_EPS = 1e-6
NEG_INF = -1e30

LANES = 128
A_W = A_HEADS * 2 * HEAD_DIM
B_W = B_HEADS * HEAD_DIM
C_W = C_HEADS * HEAD_DIM
D_W = D_GROUPS * 2 * HEAD_DIM
ABC_W = 3 * (A_W + B_W + C_W)

W_SHIFT = (ABC_W + C_HEADS) % LANES
W_ALIGNED_BLOCKS = ABC_W // LANES + 1
W_ABC_OFF = N_BRANCH * D_MODEL
W_CF_OFF = W_ABC_OFF + ABC_W
W_D_OFF = W_ABC_OFF + W_ALIGNED_BLOCKS * LANES
W_D_END = W_D_OFF + 3 * D_W
GROUP_W = 256

TOK_TILE = 512
FFN_CHUNK = 256
ATT_T = 256
ATT_WIN = 2 * ATT_T
LOG2E = math.log2(math.e)
D_UNROLL = 4
D_T = 128
VMEM_LIMIT = 56 * 1024 * 1024


def _cparams(n_axes):
    return pltpu.CompilerParams(dimension_semantics=("arbitrary",) * n_axes,
                                vmem_limit_bytes=VMEM_LIMIT)


def _rms(x, g):
    ms = jnp.mean(x * x, axis=-1, keepdims=True)
    return x * lax.rsqrt(ms + RMS_EPS) * g


def _split3(a):
    hi = a.astype(BF16)
    r1 = a - hi.astype(F32)
    mid = r1.astype(BF16)
    lo = (r1 - mid.astype(F32)).astype(BF16)
    return hi, mid, lo


def _dot_exact_lhs(a_bf16, b):
    hi, mid, lo = _split3(b)
    d = functools.partial(jnp.dot, preferred_element_type=F32)
    return d(a_bf16, hi) + d(a_bf16, mid) + d(a_bf16, lo)


def _qk(q, k):
    return lax.dot_general(q, k, (((1,), (1,)), ((), ())), preferred_element_type=F32)


def _log_sigmoid(x):
    return jnp.minimum(x, 0.0) - jnp.log(1.0 + jnp.exp(-jnp.abs(x)))


def _lane_half(shape):
    return lax.broadcasted_iota(jnp.int32, shape, len(shape) - 1) < HEAD_DIM


def _ffn_kernel(x_ref, g_ref, wi_ref, wo_ref, o_ref):
    x = x_ref[...]
    h = _rms(x, g_ref[...]).astype(BF16)
    acc = jnp.zeros(x.shape, F32)
    for c in range(FFN_HIDDEN // FFN_CHUNK):
        lo = c * FFN_CHUNK
        gate = jnp.dot(h, wi_ref[:, lo:lo + FFN_CHUNK], preferred_element_type=F32)
        up = jnp.dot(h, wi_ref[:, FFN_HIDDEN + lo:FFN_HIDDEN + lo + FFN_CHUNK],
                     preferred_element_type=F32)
        a = (gate * jax.nn.sigmoid(gate) * up).astype(BF16)
        acc = acc + jnp.dot(a, wo_ref[c * FFN_CHUNK:(c + 1) * FFN_CHUNK, :],
                            preferred_element_type=F32)
    o_ref[...] = x + 0.5 * acc


def _ffn(x2, gain, w_i, w_o):
    n = x2.shape[0]
    const = lambda i: (0, 0)
    return pl.pallas_call(
        _ffn_kernel,
        grid=(n // TOK_TILE,),
        in_specs=[
            pl.BlockSpec((TOK_TILE, D_MODEL), lambda i: (i, 0)),
            pl.BlockSpec((1, D_MODEL), const),
            pl.BlockSpec((D_MODEL, 2 * FFN_HIDDEN), const, pipeline_mode=pl.Buffered(1)),
            pl.BlockSpec((FFN_HIDDEN, D_MODEL), const, pipeline_mode=pl.Buffered(1)),
        ],
        out_specs=pl.BlockSpec((TOK_TILE, D_MODEL), lambda i: (i, 0)),
        out_shape=jax.ShapeDtypeStruct((n, D_MODEL), F32),
        compiler_params=_cparams(1),
        name="ffn",
    )(x2, gain.reshape(1, D_MODEL), w_i.astype(BF16), w_o.astype(BF16))


_ABC_SECTIONS = ((A_W, True), (A_W, True), (A_W, False),
                 (B_W, False), (B_W, False), (B_W, False),
                 (C_W, True), (C_W, True), (C_W, False))
_D_SECTIONS = ((D_W, True), (D_W, True), (D_W, False))


def _proj_sections(y, gain_ref, gmat, out_ref, sections):
    off = 0
    for width, normed in sections:
        lo = off
        while lo < off + width:
            blk = min(GROUP_W, off + width - lo)
            yb = y[:, lo:lo + blk]
            if normed:
                ms = jnp.dot((yb * yb).astype(BF16), gmat[:blk, :blk], preferred_element_type=F32)
                yb = yb * lax.rsqrt(ms + RMS_EPS)
            out_ref[:, lo:lo + blk] = (yb * gain_ref[:, lo:lo + blk]).astype(out_ref.dtype)
            lo += blk
        off += width


def _proj_kernel(x_ref, g_ref, w_ref, gabc_ref, gd_ref, gmat_ref, pabc_ref, pd_ref, cf_ref):
    h = _rms(x_ref[...], g_ref[...]).astype(BF16)
    gmat = gmat_ref[...]
    y_abc = jnp.dot(h, w_ref[:, W_ABC_OFF:W_CF_OFF], preferred_element_type=F32)
    _proj_sections(y_abc, gabc_ref, gmat, pabc_ref, _ABC_SECTIONS)
    y_d = jnp.dot(h, w_ref[:, W_CF_OFF:W_D_END], preferred_element_type=F32)
    cf_ref[...] = y_d[:, :LANES]
    _proj_sections(y_d[:, LANES:], gd_ref, gmat, pd_ref, _D_SECTIONS)


def _head_gain(gain, width, scale):
    return jnp.tile(gain.astype(F32), width // HEAD_DIM) * scale


def _win_kernel(a_ref, b_ref, o_ref):
    c = pl.program_id(1)

    @pl.when(c < W_ALIGNED_BLOCKS)
    def _():
        o_ref[...] = b_ref[...].astype(BF16)

    @pl.when(c >= W_ALIGNED_BLOCKS)
    def _():
        both = jnp.concatenate([a_ref[:, W_SHIFT:], b_ref[:, :W_SHIFT]], axis=1)
        o_ref[...] = both.astype(BF16)


def _prep_w_in(w_in):
    depth = w_in.shape[0]
    n_blocks = pl.cdiv(w_in.shape[2], LANES)
    gate_block0 = (W_D_END - W_ABC_OFF) // LANES
    return pl.pallas_call(
        _win_kernel,
        grid=(depth, n_blocks),
        in_specs=[pl.BlockSpec((None, D_MODEL, LANES), lambda l, c: (l, 0, jnp.maximum(c - 1, 0))),
                  pl.BlockSpec((None, D_MODEL, LANES), lambda l, c: (l, 0, c))],
        out_specs=pl.BlockSpec(
            (None, D_MODEL, LANES),
            lambda l, c: (l, 0, jnp.where(c < gate_block0, c + W_ABC_OFF // LANES, c - gate_block0))),
        out_shape=jax.ShapeDtypeStruct((depth, D_MODEL, W_D_END), BF16),
        compiler_params=_cparams(2),
        name="prep_w_in",
    )(w_in, w_in)


def _proj(x2, gain, w_all, layer, a_q, a_k, c_q, c_k, d_q, d_k):
    n = x2.shape[0]
    ones = lambda w, s: jnp.full((w,), s, F32)
    g_abc = jnp.concatenate([
        _head_gain(a_q, A_W, SCALE * LOG2E), _head_gain(a_k, A_W, 1.0), ones(A_W, 1.0),
        ones(B_W, SCALE * LOG2E), ones(B_W, 1.0), ones(B_W, 1.0),
        _head_gain(c_q, C_W, SCALE * LOG2E), _head_gain(c_k, C_W, 1.0),
        ones(C_W, 1.0)]).reshape(1, ABC_W)
    g_d = jnp.concatenate([_head_gain(d_q, D_W, SCALE), _head_gain(d_k, D_W, 1.0),
                           ones(D_W, 1.0)]).reshape(1, 3 * D_W)
    lane = np.arange(GROUP_W)
    gmat = jnp.asarray((lane[:, None] // HEAD_DIM == lane[None, :] // HEAD_DIM) / HEAD_DIM, BF16)
    const = lambda i: (0, 0)
    row = lambda i: (i, 0)
    return pl.pallas_call(
        _proj_kernel,
        grid=(n // TOK_TILE,),
        in_specs=[
            pl.BlockSpec((TOK_TILE, D_MODEL), row),
            pl.BlockSpec((1, D_MODEL), const),
            pl.BlockSpec((None, D_MODEL, W_D_END), lambda i: (layer, 0, 0),
                         pipeline_mode=pl.Buffered(1)),
            pl.BlockSpec((1, ABC_W), const),
            pl.BlockSpec((1, 3 * D_W), const),
            pl.BlockSpec((GROUP_W, GROUP_W), const),
        ],
        out_specs=[
            pl.BlockSpec((TOK_TILE, ABC_W), row),
            pl.BlockSpec((TOK_TILE, 3 * D_W), row),
            pl.BlockSpec((TOK_TILE, LANES), row),
        ],
        out_shape=[
            jax.ShapeDtypeStruct((n, ABC_W), BF16),
            jax.ShapeDtypeStruct((n, 3 * D_W), F32),
            jax.ShapeDtypeStruct((n, LANES), F32),
        ],
        compiler_params=_cparams(1),
        name="proj",
    )(x2, gain.reshape(1, D_MODEL), w_all, g_abc, g_d, gmat)


_MASKED_BUCKET = REL_BUCKETS


def _rel_bucket_np(n):
    n = np.asarray(n, np.int64)
    max_exact = REL_BUCKETS // 2
    nf = np.maximum(n, 1).astype(np.float32)
    large = max_exact + (np.log(nf / np.float32(max_exact)) / np.float32(math.log(REL_MAX_DIST / max_exact))
                         * np.float32(REL_BUCKETS - max_exact)).astype(np.int32)
    large = np.minimum(large, REL_BUCKETS - 1)
    return np.where(n < max_exact, n, large).astype(np.int32)


def _bucket_maps():
    t = ATT_T
    x, y = np.arange(t)[:, None], np.arange(t)[None, :]
    diag = np.where(x >= y, _rel_bucket_np(np.maximum(x - y, 0)), _MASKED_BUCKET)
    prev = _rel_bucket_np(t + x - y)
    idx_a = np.stack([prev.T, diag.T]).astype(np.int32)
    x, y = np.arange(D_T)[:, None], np.arange(D_T)[None, :]
    idx_d = np.empty((D_GROUPS, 2, D_T, 2 * D_T), np.int32)
    for g, (_, dil) in enumerate(D_PAIRS):
        prev = np.where(y >= x, _rel_bucket_np(dil * (D_T + x - y)), _MASKED_BUCKET)
        diag = np.where(y <= x, _rel_bucket_np(dil * np.maximum(x - y, 0)), _MASKED_BUCKET)
        idx_d[g, 0] = np.concatenate([prev, diag], axis=1)
        idx_d[g, 1] = np.concatenate([np.full_like(prev, _MASKED_BUCKET), diag], axis=1)
    return idx_a, idx_d


def _bias_kernel(tbl_ref, idxa_ref, idxd_ref, ba_ref, bd_ref):
    for h in range(A_HEADS):
        for kind in range(2):
            ba_ref[h, kind] = jnp.where(idxa_ref[kind] == _MASKED_BUCKET, NEG_INF, 0.0)
        ba_ref[h, 2] = jnp.zeros((ATT_T, ATT_T), F32)
        ba_ref[h, 3] = jnp.full((ATT_T, ATT_T), NEG_INF, F32)
    for g in range(D_GROUPS):
        for h in range(2):
            for var in range(2):
                bd_ref[g, h, var] = jnp.where(idxd_ref[g, var] == _MASKED_BUCKET, NEG_INF, 0.0)

    def body(b, carry):
        for h in range(A_HEADS):
            val = (tbl_ref[b, h] - tbl_ref[REL_BUCKETS - 1, h]) * LOG2E
            for kind in range(2):
                ba_ref[h, kind] = jnp.where(idxa_ref[kind] == b, val, ba_ref[h, kind])
        for g in range(D_GROUPS):
            for h in range(2):
                val = tbl_ref[b, A_HEADS + 2 * g + h]
                for var in range(2):
                    bd_ref[g, h, var] = jnp.where(idxd_ref[g, var] == b, val, bd_ref[g, h, var])
        return carry

    lax.fori_loop(0, REL_BUCKETS, body, 0)


def _bias_tiles(rel_table):
    idx_a, idx_d = _bucket_maps()
    return pl.pallas_call(
        _bias_kernel,
        in_specs=[pl.BlockSpec(memory_space=pltpu.SMEM),
                  pl.BlockSpec(memory_space=pltpu.VMEM),
                  pl.BlockSpec(memory_space=pltpu.VMEM)],
        out_specs=[pl.BlockSpec(memory_space=pltpu.VMEM),
                   pl.BlockSpec(memory_space=pltpu.VMEM)],
        out_shape=[jax.ShapeDtypeStruct((A_HEADS, 4, ATT_T, ATT_T), F32),
                   jax.ShapeDtypeStruct((D_GROUPS, 2, 2, D_T, 2 * D_T), F32)],
        compiler_params=pltpu.CompilerParams(vmem_limit_bytes=VMEM_LIMIT),
        name="rel_bias",
    )(rel_table.astype(F32), jnp.asarray(idx_a), jnp.asarray(idx_d))


def _rows(idx, size):
    return pl.ds(pl.multiple_of(idx * size, size), size)


def _split_heads(q):
    first = _lane_half((q.shape[0], LANES))
    zero = jnp.zeros((q.shape[0], LANES), q.dtype)
    out = []
    for p in range(q.shape[1] // LANES):
        qp = q[:, p * LANES:(p + 1) * LANES]
        out += [jnp.where(first, qp, zero), jnp.where(first, zero, qp)]
    return out


def _transpose_into(vt_ref, v_ref):
    for p in range(v_ref.shape[1] // LANES):
        for blk in range(v_ref.shape[0] // ATT_T):
            tile = v_ref[blk * ATT_T:(blk + 1) * ATT_T, p * LANES:(p + 1) * LANES]
            vt_ref[p * LANES:(p + 1) * LANES, blk * ATT_T:(blk + 1) * ATT_T] = (
                tile.astype(F32).T.astype(vt_ref.dtype))


def _attn_specs(width, col0, s):
    nq = s // ATT_T
    q_spec = pl.BlockSpec((ATT_T, width), lambda b, i: (b * nq + i, col0))
    k_spec = pl.BlockSpec((s, width), lambda b, i: (b, col0 + 1))
    v_spec = pl.BlockSpec((s, width), lambda b, i: (b, col0 + 2))
    o_spec = pl.BlockSpec((ATT_T, width), lambda b, i: (b * nq + i, 0))
    return q_spec, k_spec, v_spec, o_spec


def _window_split(qi):
    return lax.shift_right_logical(qi + 1, 1), jnp.bitwise_and(qi, 1)


def _softmax_stages_t(scores, ms, acc_ref, vts, ls=None):
    n = len(scores)
    m_new = [jnp.maximum(ms[c], jnp.max(scores[c], axis=0, keepdims=True)) for c in range(n)]
    alpha = [jnp.exp2(ms[c] - m_new[c]) for c in range(n)]
    p_bf16, l_new = [], []
    for c in range(n):
        p = jnp.exp2(scores[c] - m_new[c])
        p_bf16.append(p.astype(BF16))
        if ls is not None:
            l_new.append(alpha[c] * ls[c] + jnp.sum(p, axis=0, keepdims=True))
    pv = [jnp.dot(vts[c], p_bf16[c], preferred_element_type=F32) for c in range(n)]
    for c in range(n):
        acc_ref[c] = alpha[c] * acc_ref[c] + pv[c]
    return m_new, (l_new if ls is not None else None)


def _attn_a_kernel(q_ref, k_ref, v_ref, bias_ref, lam_ref, subln_ref, o_ref, acc_ref, vt_ref, *,
                   lam_init):
    qi = pl.program_id(1)
    s_len = k_ref.shape[0]

    @pl.when(qi == 0)
    def _():
        _transpose_into(vt_ref, v_ref)

    qs = _split_heads(q_ref[...])
    n_chain = len(qs)
    acc_ref[...] = jnp.zeros(acc_ref.shape, F32)
    stat = lambda val: jnp.full((1, ATT_T), val, F32)
    init = (stat(NEG_INF), stat(0.0)) * n_chain
    n_full, odd = _window_split(qi)

    def window(rows, carry, kinds):
        scores, vts = [], []
        for h in range(A_HEADS):
            cols = slice(h * LANES, (h + 1) * LANES)
            k, vt = k_ref[rows, cols], vt_ref[cols, rows]
            if kinds is not None:
                bias = jnp.concatenate([bias_ref[h, kind] for kind in kinds], axis=0)
            for c in (2 * h, 2 * h + 1):
                s = _qk(k, qs[c])
                scores.append(s if kinds is None else s + bias)
                vts.append(vt)
        ms, ls = _softmax_stages_t(scores, carry[0::2], acc_ref, vts, ls=carry[1::2])
        return tuple(x for pair in zip(ms, ls) for x in pair)

    n_plain = jnp.maximum(n_full - 1, 0)
    carry = lax.fori_loop(0, n_plain, lambda j, c: window(_rows(j, ATT_WIN), c, None), init)
    near = (jnp.where(odd == 1, 0, 2), jnp.where(odd == 1, 1, 0))
    carry = lax.fori_loop(n_plain, n_full, lambda j, c: window(_rows(j, ATT_WIN), c, near), carry)
    carry = lax.fori_loop(0, 1 - odd, lambda _, c: window(_rows(qi, ATT_T), c, (1,)), carry)

    lv = lam_ref[...]
    lam = (jnp.exp(jnp.sum(lv[0:1] * lv[1:2], axis=-1, keepdims=True))
           - jnp.exp(jnp.sum(lv[2:3] * lv[3:4], axis=-1, keepdims=True)) + lam_init)
    for h in range(A_HEADS):
        l0, l1 = carry[4 * h + 1], carry[4 * h + 3]
        o = acc_ref[2 * h] / l0 - lam * (acc_ref[2 * h + 1] / l1)
        ms = jnp.mean(o * o, axis=0, keepdims=True)
        o = o * lax.rsqrt(ms + RMS_EPS) * subln_ref[...] * (1.0 - lam_init)
        o_ref[:, h * LANES:(h + 1) * LANES] = o.T.astype(o_ref.dtype)


def _attn_a(p_abc, bias_a, lam_vecs, subln, lam_init, batch, s):
    n = p_abc.shape[0]
    q_spec, k_spec, v_spec, o_spec = _attn_specs(A_W, 0, s)
    return pl.pallas_call(
        functools.partial(_attn_a_kernel, lam_init=lam_init),
        grid=(batch, s // ATT_T),
        in_specs=[q_spec, k_spec, v_spec,
                  pl.BlockSpec(bias_a.shape, lambda b, i: (0, 0, 0, 0)),
                  pl.BlockSpec((4, HEAD_DIM), lambda b, i: (0, 0)),
                  pl.BlockSpec((2 * HEAD_DIM, ATT_T), lambda b, i: (0, 0))],
        out_specs=o_spec,
        out_shape=jax.ShapeDtypeStruct((n, A_W), BF16),
        scratch_shapes=[pltpu.VMEM((2 * A_HEADS, LANES, ATT_T), F32),
                        pltpu.VMEM((A_W, s), BF16)],
        compiler_params=_cparams(2),
        name="attn_diff",
    )(p_abc, p_abc, p_abc, bias_a, lam_vecs.astype(F32),
      jnp.broadcast_to(subln.astype(F32)[:, None], (2 * HEAD_DIM, ATT_T)))


def _attn_b_kernel(q_ref, k_ref, v_ref, o_ref, acc_ref, vt_ref):
    qi = pl.program_id(1)

    @pl.when(qi == 0)
    def _():
        _transpose_into(vt_ref, v_ref)

    qs = _split_heads(q_ref[...])
    acc_ref[...] = jnp.zeros(acc_ref.shape, F32)
    key = lax.broadcasted_iota(jnp.int32, (ATT_T, ATT_T), 0)
    qry = lax.broadcasted_iota(jnp.int32, (ATT_T, ATT_T), 1)
    from_here = jnp.where(qry >= key, 1.0, 0.0).astype(BF16)
    strict = key < qry

    def step(kj, rsum, diag):
        rows = _rows(kj, ATT_T)
        heads = range(B_HEADS)
        cols = [slice((h // 2) * LANES, (h // 2 + 1) * LANES) for h in heads]
        z = [_qk(k_ref[rows, cols[h]], qs[h]) for h in heads]
        neg_z = [-z[h] for h in heads]
        soft = [jnp.log2(1.0 + jnp.exp2(jnp.minimum(z[h], neg_z[h]))) for h in heads]
        u = [jnp.minimum(neg_z[h], 0.0) - soft[h] for h in heads]
        if diag:
            u = [jnp.where(strict, u[h], 0.0) for h in heads]
        a = [jnp.exp2(z[h] + jnp.dot(from_here, u[h].astype(BF16), preferred_element_type=F32)
                      + rsum[h]) for h in heads]
        if diag:
            a = [jnp.where(strict, a[h], 0.0) for h in heads]
        pv = [jnp.dot(vt_ref[cols[h], rows], a[h].astype(BF16), preferred_element_type=F32)
              for h in heads]
        for h in heads:
            acc_ref[h] = acc_ref[h] + pv[h]
        return tuple(rsum[h] + jnp.sum(u[h], axis=0, keepdims=True) for h in heads)

    zero = jnp.zeros((1, ATT_T), F32)
    rsum = step(qi, (zero,) * B_HEADS, True)
    lax.fori_loop(0, qi, lambda t, c: step(qi - 1 - t, c, False), rsum)
    for p in range(B_HEADS // 2):
        o_t = jnp.concatenate([acc_ref[2 * p][:HEAD_DIM], acc_ref[2 * p + 1][HEAD_DIM:]], axis=0)
        o_ref[:, p * LANES:(p + 1) * LANES] = o_t.T.astype(o_ref.dtype)


def _attn_b(p_abc, batch, s):
    n = p_abc.shape[0]
    q_spec, k_spec, v_spec, o_spec = _attn_specs(B_W, 3 * A_W // B_W, s)
    return pl.pallas_call(
        _attn_b_kernel,
        grid=(batch, s // ATT_T),
        in_specs=[q_spec, k_spec, v_spec],
        out_specs=o_spec,
        out_shape=jax.ShapeDtypeStruct((n, B_W), BF16),
        scratch_shapes=[pltpu.VMEM((B_HEADS, LANES, ATT_T), F32),
                        pltpu.VMEM((B_W, s), BF16)],
        compiler_params=_cparams(2),
        name="attn_stick",
    )(p_abc, p_abc, p_abc)


def _attn_c_kernel(q_ref, k_ref, v_ref, cf_ref, fb_ref, o_ref, acc_ref, cumk_ref, cumr_ref, vaug_ref):
    qi = pl.program_id(1)
    s_len = cf_ref.shape[0]

    @pl.when(qi == 0)
    def _():
        r = lax.broadcasted_iota(jnp.int32, (LANES, LANES), 0)
        c = lax.broadcasted_iota(jnp.int32, (LANES, LANES), 1)
        lower = jnp.where(c <= r, 1.0, 0.0).astype(BF16)
        carry = jnp.zeros((1, LANES), F32)
        for blk in range(s_len // LANES):
            rows = slice(blk * LANES, (blk + 1) * LANES)
            log_f = _log_sigmoid(cf_ref[rows, :] + fb_ref[...])
            cum = _dot_exact_lhs(lower, log_f) + carry
            cumr_ref[:, rows] = (cum * LOG2E).T
            for h in range(C_HEADS):
                cumk_ref[h, rows, :] = jnp.broadcast_to(cum[:, h:h + 1] * LOG2E, (LANES, LANES))
            carry = cum[LANES - 1:LANES, :]
        own = lax.broadcasted_iota(jnp.int32, (LANES, ATT_T), 0) < HEAD_DIM
        for p in range(C_HEADS // 2):
            for blk in range(s_len // ATT_T):
                tile = v_ref[blk * ATT_T:(blk + 1) * ATT_T, p * LANES:(p + 1) * LANES]
                tile_t = tile.astype(F32).T
                dst = (slice(p * LANES, (p + 1) * LANES), slice(blk * ATT_T, (blk + 1) * ATT_T))
                vaug_ref[(0,) + dst] = jnp.where(own, tile_t, 1.0).astype(BF16)
                vaug_ref[(1,) + dst] = jnp.where(own, 1.0, tile_t).astype(BF16)

    qs = _split_heads(q_ref[...])
    acc_ref[...] = jnp.zeros(acc_ref.shape, F32)
    q_cols = _rows(qi, ATT_T)
    cum_q = [cumr_ref[h:h + 1, q_cols] for h in range(C_HEADS)]
    n_full, odd = _window_split(qi)

    def causal(width):
        key = lax.broadcasted_iota(jnp.int32, (width, ATT_T), 0) - (width - ATT_T)
        return jnp.where(key <= lax.broadcasted_iota(jnp.int32, (width, ATT_T), 1), 0.0, NEG_INF)

    def window(rows, carry, mask):
        scores, vts = [], []
        for h in range(C_HEADS):
            cols = slice((h // 2) * LANES, (h // 2 + 1) * LANES)
            cum_k = cumk_ref[h, rows, :]
            s = _qk(k_ref[rows, cols], qs[h]) + cum_q[h] - jnp.concatenate([cum_k, cum_k], axis=1)
            scores.append(s if mask is None else s + mask)
            vts.append(vaug_ref[h % 2, cols, rows])
        return tuple(_softmax_stages_t(scores, carry, acc_ref, vts)[0])

    init = (jnp.full((1, ATT_T), NEG_INF, F32),) * C_HEADS
    carry = lax.fori_loop(0, n_full - odd, lambda j, c: window(_rows(j, ATT_WIN), c, None), init)
    carry = lax.fori_loop(n_full - odd, n_full,
                          lambda j, c: window(_rows(j, ATT_WIN), c, causal(ATT_WIN)), carry)
    lax.fori_loop(0, 1 - odd, lambda _, c: window(_rows(qi, ATT_T), c, causal(ATT_T)), carry)
    for p in range(C_HEADS // 2):
        a0, a1 = acc_ref[2 * p], acc_ref[2 * p + 1]
        num = jnp.concatenate([a0[:HEAD_DIM], a1[HEAD_DIM:]], axis=0)
        den = jnp.concatenate([a0[HEAD_DIM:], a1[:HEAD_DIM]], axis=0)
        o_ref[:, p * LANES:(p + 1) * LANES] = (num / den).T.astype(o_ref.dtype)


def _attn_c(p_abc, cf, forget_bias, batch, s):
    n = p_abc.shape[0]
    q_spec, k_spec, v_spec, o_spec = _attn_specs(C_W, 3 * (A_W + B_W) // C_W, s)
    fb = jnp.pad(forget_bias.astype(F32), (0, LANES - C_HEADS)).reshape(1, LANES)
    return pl.pallas_call(
        _attn_c_kernel,
        grid=(batch, s // ATT_T),
        in_specs=[q_spec, k_spec, v_spec,
                  pl.BlockSpec((s, LANES), lambda b, i: (b, 0)),
                  pl.BlockSpec((1, LANES), lambda b, i: (0, 0))],
        out_specs=o_spec,
        out_shape=jax.ShapeDtypeStruct((n, C_W), BF16),
        scratch_shapes=[pltpu.VMEM((C_HEADS, LANES, ATT_T), F32),
                        pltpu.VMEM((C_HEADS, s, LANES), F32),
                        pltpu.VMEM((LANES, s), F32),
                        pltpu.VMEM((2, C_W, s), BF16)],
        compiler_params=_cparams(2),
        name="attn_forget",
    )(p_abc, p_abc, p_abc, cf, fb)


def _attn_d_kernel(*refs):
    qkv_refs, (bias_ref, o_ref, qc_ref, kc_ref, vc_ref, oc_ref, lc_ref, og_ref, lse_ref) = (
        refs[:3 * D_GROUPS], refs[3 * D_GROUPS:])
    s_len = o_ref.shape[0]
    n_tiles = s_len // D_T
    first = _lane_half((D_T, LANES))
    kc_ref[0:D_T, :] = jnp.zeros((D_T, LANES), BF16)
    vc_ref[0:D_T, :] = jnp.zeros((D_T, LANES), BF16)

    for g, (_, dil) in enumerate(D_PAIRS):
        q_ref, k_ref, v_ref = qkv_refs[g], qkv_refs[D_GROUPS + g], qkv_refs[2 * D_GROUPS + g]
        class_len = s_len // dil
        tiles_per_class = class_len // D_T

        for r in range(dil):
            src = pl.ds(r, class_len, stride=dil) if dil > 1 else slice(None)
            dst = slice(r * class_len, (r + 1) * class_len)
            dst_kv = slice(D_T + r * class_len, D_T + (r + 1) * class_len)
            qc_ref[dst, :] = q_ref[src, :].astype(BF16)
            kc_ref[dst_kv, :] = k_ref[src, :].astype(BF16)
            vc_ref[dst_kv, :] = v_ref[src, :].astype(BF16)

        def tile_group(i, carry):
            tiles = [i * D_UNROLL + u for u in range(D_UNROLL)]
            rows = [pl.ds(pl.multiple_of(t * D_T, D_T), D_T) for t in tiles]
            if tiles_per_class > 1:
                keys = [pl.ds(pl.multiple_of(t * D_T, D_T), 2 * D_T) for t in tiles]
                variant = [jnp.where(t % tiles_per_class == 0, 1, 0) for t in tiles]
                bias = [[bias_ref[g, h, variant[u]] for h in range(2)] for u in range(D_UNROLL)]
            else:
                keys = [pl.ds(pl.multiple_of(t * D_T + D_T, D_T), D_T) for t in tiles]
                bias = [[bias_ref[g, h, 1][:, D_T:] for h in range(2)]] * D_UNROLL
            chains = [(u, h) for u in range(D_UNROLL) for h in range(2)]
            qh = [_split_heads(qc_ref[rows[u], :]) for u in range(D_UNROLL)]
            sc = [_qk(qh[u][h], kc_ref[keys[u], :]) + bias[u][h] for u, h in chains]
            m = [jnp.max(s, axis=-1, keepdims=True) for s in sc]
            e = [jnp.exp(s - mm) for s, mm in zip(sc, m)]
            l = [jnp.sum(x, axis=-1, keepdims=True) for x in e]
            o = [jnp.dot(x.astype(BF16), vc_ref[keys[u], :], preferred_element_type=F32)
                 for x, (u, h) in zip(e, chains)]
            o = [x / ll for x, ll in zip(o, l)]
            lse = [jnp.broadcast_to(mm + jnp.log(ll), (D_T, LANES)) for mm, ll in zip(m, l)]
            for u in range(D_UNROLL):
                oc_ref[rows[u], :] = jnp.where(first, o[2 * u], o[2 * u + 1])
                lc_ref[rows[u], :] = jnp.where(first, lse[2 * u], lse[2 * u + 1])
            return carry

        lax.fori_loop(0, n_tiles // D_UNROLL, tile_group, 0)

        for r in range(dil):
            dst = pl.ds(r, class_len, stride=dil) if dil > 1 else slice(None)
            src = slice(r * class_len, (r + 1) * class_len)
            og_ref[g, dst, :] = oc_ref[src, :]
            lse_ref[g, dst, :] = lc_ref[src, :]

    lse = [lse_ref[g] for g in range(D_GROUPS)]
    top = functools.reduce(jnp.maximum, lse)
    w = [jnp.exp(x - top) for x in lse]
    num = sum(w[g] * og_ref[g] for g in range(D_GROUPS))
    o_ref[...] = (num / sum(w)).astype(o_ref.dtype)


def _attn_d(p_d, bias_d, batch, s):
    n = p_d.shape[0]
    return pl.pallas_call(
        _attn_d_kernel,
        grid=(batch,),
        in_specs=([pl.BlockSpec((s, LANES), functools.partial(lambda b, c: (b, c), c=c))
                   for c in range(3 * D_GROUPS)]
                  + [pl.BlockSpec(bias_d.shape, lambda b: (0, 0, 0, 0, 0))]),
        out_specs=pl.BlockSpec((s, LANES), lambda b: (b, 0)),
        out_shape=jax.ShapeDtypeStruct((n, LANES), BF16),
        scratch_shapes=[pltpu.VMEM((s, LANES), BF16), pltpu.VMEM((s + D_T, LANES), BF16),
                        pltpu.VMEM((s + D_T, LANES), BF16),
                        pltpu.VMEM((s, LANES), F32), pltpu.VMEM((s, LANES), F32),
                        pltpu.VMEM((D_GROUPS, s, LANES), F32),
                        pltpu.VMEM((D_GROUPS, s, LANES), F32)],
        compiler_params=_cparams(1),
        name="attn_dilated",
    )(*([p_d] * (3 * D_GROUPS)), bias_d)


def _merge_kernel(x_ref, g_ref, wg_ref, gb_ref, oa_ref, ob_ref, oc_ref, od_ref,
                  wa_ref, wb_ref, wc_ref, wd_ref, wout_ref, o_ref):
    x = x_ref[...]
    h = _rms(x, g_ref[...]).astype(BF16)
    merged = jnp.zeros(x.shape, F32)
    branches = ((oa_ref, wa_ref), (ob_ref, wb_ref), (oc_ref, wc_ref), (od_ref, wd_ref))
    for i, (br_ref, w_ref) in enumerate(branches):
        cols = slice(i * D_MODEL, (i + 1) * D_MODEL)
        logits = jnp.dot(h, wg_ref[:, cols], preferred_element_type=F32) + gb_ref[:, cols]
        branch = jnp.dot(br_ref[...], w_ref[...], preferred_element_type=F32)
        merged = merged + jax.nn.sigmoid(logits) * branch
    o_ref[...] = x + jnp.dot(merged.astype(BF16), wout_ref[...], preferred_element_type=F32)


def _merge(x2, gain, w_all, layer, gate_bias, outs, w_branch, w_out):
    n = x2.shape[0]
    gate_w = N_BRANCH * D_MODEL
    widths = (A_W, B_W, C_W, 2 * HEAD_DIM)
    offs = np.concatenate([[0], np.cumsum(widths)])
    w_br = [w_branch[offs[i]:offs[i + 1]].astype(BF16) for i in range(N_BRANCH)]
    const = lambda i: (0, 0)
    row = lambda i: (i, 0)
    return pl.pallas_call(
        _merge_kernel,
        grid=(n // TOK_TILE,),
        in_specs=([pl.BlockSpec((TOK_TILE, D_MODEL), row),
                   pl.BlockSpec((1, D_MODEL), const),
                   pl.BlockSpec((None, D_MODEL, gate_w), lambda i: (layer, 0, 0),
                                pipeline_mode=pl.Buffered(1)),
                   pl.BlockSpec((1, gate_w), const)]
                  + [pl.BlockSpec((TOK_TILE, w), row) for w in widths]
                  + [pl.BlockSpec((w, D_MODEL), const) for w in widths]
                  + [pl.BlockSpec((D_MODEL, D_MODEL), const)]),
        out_specs=pl.BlockSpec((TOK_TILE, D_MODEL), row),
        out_shape=jax.ShapeDtypeStruct((n, D_MODEL), F32),
        compiler_params=_cparams(1),
        name="merge",
    )(x2, gain.reshape(1, D_MODEL), w_all, gate_bias.reshape(1, -1).astype(F32),
      *outs, *w_br, w_out.astype(BF16))


def kernel(x, rel_table, ffn1_norm, ffn1_w_in, ffn1_w_out, mix_norm, w_in, gate_bias, forget_bias,
           a_q_norm, a_k_norm, a_lambda, a_subln, c_q_norm, c_k_norm, d_q_norm, d_k_norm,
           w_branch, w_out, ffn2_norm, ffn2_w_in, ffn2_w_out):
    batch, s, _ = x.shape
    depth = w_in.shape[0]
    assert s % (D_T * D_PAIRS[-1][1]) == 0 and (batch * s) % TOK_TILE == 0
    bias_a, bias_d = _bias_tiles(rel_table)
    w_all = _prep_w_in(w_in)
    x2 = x.reshape(batch * s, D_MODEL)
    for l in range(depth):
        x2 = _ffn(x2, ffn1_norm[l], ffn1_w_in[l], ffn1_w_out[l])
        p_abc, p_d, cf = _proj(x2, mix_norm[l], w_all, l, a_q_norm[l], a_k_norm[l],
                               c_q_norm[l], c_k_norm[l], d_q_norm[l], d_k_norm[l])
        lam_init = 0.8 - 0.6 * math.exp(-0.3 * l)
        oa = _attn_a(p_abc, bias_a, a_lambda[l], a_subln[l], lam_init, batch, s)
        ob = _attn_b(p_abc, batch, s)
        oc = _attn_c(p_abc, cf, forget_bias[l], batch, s)
        od = _attn_d(p_d, bias_d, batch, s)
        x2 = _merge(x2, mix_norm[l], w_all, l, gate_bias[l], (oa, ob, oc, od),
                    w_branch[l], w_out[l])
        x2 = _ffn(x2, ffn2_norm[l], ffn2_w_in[l], ffn2_w_out[l])
    return x2.reshape(batch, s, D_MODEL)
```

```python
import functools
import math

import numpy as np
import jax
import jax.numpy as jnp
from jax import lax
from jax.experimental import pallas as pl
from jax.experimental.pallas import tpu as pltpu

F32 = jnp.float32
BF16 = jnp.bfloat16

D_MODEL = 1024
HEAD_DIM = 64
SCALE = HEAD_DIM ** -0.5
A_HEADS = 4
B_HEADS = 6
C_HEADS = 6
D_PAIRS = ((128, 1), (512, 4), (2048, 16))
D_GROUPS = len(D_PAIRS)
N_BRANCH = 4
FFN_HIDDEN = 2816
REL_BUCKETS = 32
REL_MAX_DIST = 128
RMS_EPS = 1e-6
NEG_INF = -1e30

LANES = 128
A_W = A_HEADS * 2 * HEAD_DIM
B_W = B_HEADS * HEAD_DIM
C_W = C_HEADS * HEAD_DIM
D_W = D_GROUPS * 2 * HEAD_DIM
ABC_W = 3 * (A_W + B_W + C_W)

W_SHIFT = (ABC_W + C_HEADS) % LANES
W_ALIGNED_BLOCKS = ABC_W // LANES + 1
W_ABC_OFF = N_BRANCH * D_MODEL
W_CF_OFF = W_ABC_OFF + ABC_W
W_D_OFF = W_ABC_OFF + W_ALIGNED_BLOCKS * LANES
W_D_END = W_D_OFF + 3 * D_W
GROUP_W = 256

TOK_TILE = 512
FFN_CHUNK = 256
ATT_T = 256
ATT_WIN = 2 * ATT_T
LOG2E = math.log2(math.e)
ATT_ROWS = 2
D_UNROLL = 4
D_T = 128
VMEM_LIMIT = 56 * 1024 * 1024


def _cparams(n_axes):
    return pltpu.CompilerParams(dimension_semantics=("arbitrary",) * n_axes,
                                vmem_limit_bytes=VMEM_LIMIT)


def _rms(x, g):
    ms = jnp.mean(x * x, axis=-1, keepdims=True)
    return x * lax.rsqrt(ms + RMS_EPS) * g


def _split3(a):
    hi = a.astype(BF16)
    r1 = a - hi.astype(F32)
    mid = r1.astype(BF16)
    lo = (r1 - mid.astype(F32)).astype(BF16)
    return hi, mid, lo


def _dot_exact_lhs(a_bf16, b):
    hi, mid, lo = _split3(b)
    d = functools.partial(jnp.dot, preferred_element_type=F32)
    return d(a_bf16, hi) + d(a_bf16, mid) + d(a_bf16, lo)


def _qk(q, k):
    return lax.dot_general(q, k, (((1,), (1,)), ((), ())), preferred_element_type=F32)


def _log_sigmoid(x):
    return jnp.minimum(x, 0.0) - jnp.log(1.0 + jnp.exp(-jnp.abs(x)))


def _lane_half(shape):
    return lax.broadcasted_iota(jnp.int32, shape, len(shape) - 1) < HEAD_DIM


def _ffn_kernel(x_ref, g_ref, wi_ref, wo_ref, o_ref):
    x = x_ref[...]
    h = _rms(x, g_ref[...]).astype(BF16)
    acc = jnp.zeros(x.shape, F32)
    for c in range(FFN_HIDDEN // FFN_CHUNK):
        lo = c * FFN_CHUNK
        gate = jnp.dot(h, wi_ref[:, lo:lo + FFN_CHUNK], preferred_element_type=F32)
        up = jnp.dot(h, wi_ref[:, FFN_HIDDEN + lo:FFN_HIDDEN + lo + FFN_CHUNK],
                     preferred_element_type=F32)
        a = (gate * jax.nn.sigmoid(gate) * up).astype(BF16)
        acc = acc + jnp.dot(a, wo_ref[c * FFN_CHUNK:(c + 1) * FFN_CHUNK, :],
                            preferred_element_type=F32)
    o_ref[...] = x + 0.5 * acc


def _ffn(x2, gain, w_i, w_o):
    n = x2.shape[0]
    const = lambda i: (0, 0)
    return pl.pallas_call(
        _ffn_kernel,
        grid=(n // TOK_TILE,),
        in_specs=[
            pl.BlockSpec((TOK_TILE, D_MODEL), lambda i: (i, 0)),
            pl.BlockSpec((1, D_MODEL), const),
            pl.BlockSpec((D_MODEL, 2 * FFN_HIDDEN), const, pipeline_mode=pl.Buffered(1)),
            pl.BlockSpec((FFN_HIDDEN, D_MODEL), const, pipeline_mode=pl.Buffered(1)),
        ],
        out_specs=pl.BlockSpec((TOK_TILE, D_MODEL), lambda i: (i, 0)),
        out_shape=jax.ShapeDtypeStruct((n, D_MODEL), F32),
        compiler_params=_cparams(1),
        name="ffn",
    )(x2, gain.reshape(1, D_MODEL), w_i.astype(BF16), w_o.astype(BF16))


_ABC_SECTIONS = ((A_W, True), (A_W, True), (A_W, False),
                 (B_W, False), (B_W, False), (B_W, False),
                 (C_W, True), (C_W, True), (C_W, False))
_D_SECTIONS = ((D_W, True), (D_W, True), (D_W, False))


def _proj_sections(y, gain_ref, gmat, out_ref, sections):
    off = 0
    for width, normed in sections:
        lo = off
        while lo < off + width:
            blk = min(GROUP_W, off + width - lo)
            yb = y[:, lo:lo + blk]
            if normed:
                ms = jnp.dot((yb * yb).astype(BF16), gmat[:blk, :blk], preferred_element_type=F32)
                yb = yb * lax.rsqrt(ms + RMS_EPS)
            out_ref[:, lo:lo + blk] = (yb * gain_ref[:, lo:lo + blk]).astype(out_ref.dtype)
            lo += blk
        off += width


def _proj_kernel(x_ref, g_ref, w_ref, gabc_ref, gd_ref, gmat_ref, pabc_ref, pd_ref, cf_ref):
    h = _rms(x_ref[...], g_ref[...]).astype(BF16)
    gmat = gmat_ref[...]
    y_abc = jnp.dot(h, w_ref[:, W_ABC_OFF:W_CF_OFF], preferred_element_type=F32)
    _proj_sections(y_abc, gabc_ref, gmat, pabc_ref, _ABC_SECTIONS)
    y_d = jnp.dot(h, w_ref[:, W_CF_OFF:W_D_END], preferred_element_type=F32)
    cf_ref[...] = y_d[:, :LANES]
    _proj_sections(y_d[:, LANES:], gd_ref, gmat, pd_ref, _D_SECTIONS)


def _head_gain(gain, width, scale):
    return jnp.tile(gain.astype(F32), width // HEAD_DIM) * scale


def _win_kernel(a_ref, b_ref, o_ref):
    c = pl.program_id(1)

    @pl.when(c < W_ALIGNED_BLOCKS)
    def _():
        o_ref[...] = b_ref[...].astype(BF16)

    @pl.when(c >= W_ALIGNED_BLOCKS)
    def _():
        both = jnp.concatenate([a_ref[:, W_SHIFT:], b_ref[:, :W_SHIFT]], axis=1)
        o_ref[...] = both.astype(BF16)


def _prep_w_in(w_in):
    depth = w_in.shape[0]
    n_blocks = pl.cdiv(w_in.shape[2], LANES)
    gate_block0 = (W_D_END - W_ABC_OFF) // LANES
    return pl.pallas_call(
        _win_kernel,
        grid=(depth, n_blocks),
        in_specs=[pl.BlockSpec((None, D_MODEL, LANES), lambda l, c: (l, 0, jnp.maximum(c - 1, 0))),
                  pl.BlockSpec((None, D_MODEL, LANES), lambda l, c: (l, 0, c))],
        out_specs=pl.BlockSpec(
            (None, D_MODEL, LANES),
            lambda l, c: (l, 0, jnp.where(c < gate_block0, c + W_ABC_OFF // LANES, c - gate_block0))),
        out_shape=jax.ShapeDtypeStruct((depth, D_MODEL, W_D_END), BF16),
        compiler_params=_cparams(2),
        name="prep_w_in",
    )(w_in, w_in)


def _proj(x2, gain, w_all, layer, a_q, a_k, c_q, c_k, d_q, d_k):
    n = x2.shape[0]
    ones = lambda w, s: jnp.full((w,), s, F32)
    g_abc = jnp.concatenate([
        _head_gain(a_q, A_W, SCALE * LOG2E), _head_gain(a_k, A_W, 1.0), ones(A_W, 1.0),
        ones(B_W, SCALE * LOG2E), ones(B_W, 1.0), ones(B_W, 1.0),
        _head_gain(c_q, C_W, SCALE * LOG2E), _head_gain(c_k, C_W, 1.0),
        ones(C_W, 1.0)]).reshape(1, ABC_W)
    g_d = jnp.concatenate([_head_gain(d_q, D_W, SCALE), _head_gain(d_k, D_W, 1.0),
                           ones(D_W, 1.0)]).reshape(1, 3 * D_W)
    lane = np.arange(GROUP_W)
    gmat = jnp.asarray((lane[:, None] // HEAD_DIM == lane[None, :] // HEAD_DIM) / HEAD_DIM, BF16)
    const = lambda i: (0, 0)
    row = lambda i: (i, 0)
    return pl.pallas_call(
        _proj_kernel,
        grid=(n // TOK_TILE,),
        in_specs=[
            pl.BlockSpec((TOK_TILE, D_MODEL), row),
            pl.BlockSpec((1, D_MODEL), const),
            pl.BlockSpec((None, D_MODEL, W_D_END), lambda i: (layer, 0, 0),
                         pipeline_mode=pl.Buffered(1)),
            pl.BlockSpec((1, ABC_W), const),
            pl.BlockSpec((1, 3 * D_W), const),
            pl.BlockSpec((GROUP_W, GROUP_W), const),
        ],
        out_specs=[
            pl.BlockSpec((TOK_TILE, ABC_W), row),
            pl.BlockSpec((TOK_TILE, 3 * D_W), row),
            pl.BlockSpec((TOK_TILE, LANES), row),
        ],
        out_shape=[
            jax.ShapeDtypeStruct((n, ABC_W), BF16),
            jax.ShapeDtypeStruct((n, 3 * D_W), F32),
            jax.ShapeDtypeStruct((n, LANES), F32),
        ],
        compiler_params=_cparams(1),
        name="proj",
    )(x2, gain.reshape(1, D_MODEL), w_all, g_abc, g_d, gmat)


_MASKED_BUCKET = REL_BUCKETS


def _rel_bucket_np(n):
    n = np.asarray(n, np.int64)
    max_exact = REL_BUCKETS // 2
    nf = np.maximum(n, 1).astype(np.float32)
    large = max_exact + (np.log(nf / np.float32(max_exact)) / np.float32(math.log(REL_MAX_DIST / max_exact))
                         * np.float32(REL_BUCKETS - max_exact)).astype(np.int32)
    large = np.minimum(large, REL_BUCKETS - 1)
    return np.where(n < max_exact, n, large).astype(np.int32)


def _bucket_maps():
    t = ATT_T
    x, y = np.arange(t)[:, None], np.arange(t)[None, :]
    diag = np.where(x >= y, _rel_bucket_np(np.maximum(x - y, 0)), _MASKED_BUCKET)
    prev = _rel_bucket_np(t + x - y)
    idx_a = np.stack([prev.T, diag.T]).astype(np.int32)
    x, y = np.arange(D_T)[:, None], np.arange(D_T)[None, :]
    idx_d = np.empty((D_GROUPS, 2, D_T, 2 * D_T), np.int32)
    for g, (_, dil) in enumerate(D_PAIRS):
        prev = np.where(y >= x, _rel_bucket_np(dil * (D_T + x - y)), _MASKED_BUCKET)
        diag = np.where(y <= x, _rel_bucket_np(dil * np.maximum(x - y, 0)), _MASKED_BUCKET)
        idx_d[g, 0] = np.concatenate([prev, diag], axis=1)
        idx_d[g, 1] = np.concatenate([np.full_like(prev, _MASKED_BUCKET), diag], axis=1)
    return idx_a, idx_d


def _bias_kernel(tbl_ref, idxa_ref, idxd_ref, ba_ref, bd_ref):
    for h in range(A_HEADS):
        for kind in range(2):
            ba_ref[h, kind] = jnp.where(idxa_ref[kind] == _MASKED_BUCKET, NEG_INF, 0.0)
        ba_ref[h, 2] = jnp.zeros((ATT_T, ATT_T), F32)
        ba_ref[h, 3] = jnp.full((ATT_T, ATT_T), NEG_INF, F32)
    for g in range(D_GROUPS):
        for h in range(2):
            for var in range(2):
                bd_ref[g, h, var] = jnp.where(idxd_ref[g, var] == _MASKED_BUCKET, NEG_INF, 0.0)

    def body(b, carry):
        for h in range(A_HEADS):
            val = (tbl_ref[b, h] - tbl_ref[REL_BUCKETS - 1, h]) * LOG2E
            for kind in range(2):
                ba_ref[h, kind] = jnp.where(idxa_ref[kind] == b, val, ba_ref[h, kind])
        for g in range(D_GROUPS):
            for h in range(2):
                val = tbl_ref[b, A_HEADS + 2 * g + h]
                for var in range(2):
                    bd_ref[g, h, var] = jnp.where(idxd_ref[g, var] == b, val, bd_ref[g, h, var])
        return carry

    lax.fori_loop(0, REL_BUCKETS, body, 0)


def _bias_tiles(rel_table):
    idx_a, idx_d = _bucket_maps()
    return pl.pallas_call(
        _bias_kernel,
        in_specs=[pl.BlockSpec(memory_space=pltpu.SMEM),
                  pl.BlockSpec(memory_space=pltpu.VMEM),
                  pl.BlockSpec(memory_space=pltpu.VMEM)],
        out_specs=[pl.BlockSpec(memory_space=pltpu.VMEM),
                   pl.BlockSpec(memory_space=pltpu.VMEM)],
        out_shape=[jax.ShapeDtypeStruct((A_HEADS, 4, ATT_T, ATT_T), F32),
                   jax.ShapeDtypeStruct((D_GROUPS, 2, 2, D_T, 2 * D_T), F32)],
        compiler_params=pltpu.CompilerParams(vmem_limit_bytes=VMEM_LIMIT),
        name="rel_bias",
    )(rel_table.astype(F32), jnp.asarray(idx_a), jnp.asarray(idx_d))


def _rows(idx, size):
    return pl.ds(pl.multiple_of(idx * size, size), size)


def _split_heads(q):
    first = _lane_half((q.shape[0], LANES))
    zero = jnp.zeros((q.shape[0], LANES), q.dtype)
    out = []
    for p in range(q.shape[1] // LANES):
        qp = q[:, p * LANES:(p + 1) * LANES]
        out += [jnp.where(first, qp, zero), jnp.where(first, zero, qp)]
    return out


def _transpose_into(vt_ref, v_ref):
    for p in range(v_ref.shape[1] // LANES):
        for blk in range(v_ref.shape[0] // ATT_T):
            tile = v_ref[blk * ATT_T:(blk + 1) * ATT_T, p * LANES:(p + 1) * LANES]
            vt_ref[p * LANES:(p + 1) * LANES, blk * ATT_T:(blk + 1) * ATT_T] = (
                tile.astype(F32).T.astype(vt_ref.dtype))


def _attn_specs(width, col0, s):
    nq = s // ATT_T
    q_spec = pl.BlockSpec((ATT_T, width), lambda b, i: (b * nq + i, col0))
    k_spec = pl.BlockSpec((s, width), lambda b, i: (b, col0 + 1))
    v_spec = pl.BlockSpec((s, width), lambda b, i: (b, col0 + 2))
    o_spec = pl.BlockSpec((ATT_T, width), lambda b, i: (b * nq + i, 0))
    return q_spec, k_spec, v_spec, o_spec


def _attn_specs_rows(width, col0, s, rows):
    q_spec = pl.BlockSpec((rows, ATT_T, width), lambda b, i: (b, i, col0))
    k_spec = pl.BlockSpec((rows, s, width), lambda b, i: (b, 0, col0 + 1))
    v_spec = pl.BlockSpec((rows, s, width), lambda b, i: (b, 0, col0 + 2))
    o_spec = pl.BlockSpec((rows, ATT_T, width), lambda b, i: (b, i, 0))
    return q_spec, k_spec, v_spec, o_spec


def _window_split(qi):
    return lax.shift_right_logical(qi + 1, 1), jnp.bitwise_and(qi, 1)


def _softmax_stages_t(scores, ms, acc_ref, vts, ls=None):
    n = len(scores)
    m_new = [jnp.maximum(ms[c], jnp.max(scores[c], axis=0, keepdims=True)) for c in range(n)]
    alpha = [jnp.exp2(ms[c] - m_new[c]) for c in range(n)]
    p_bf16, l_new = [], []
    for c in range(n):
        p = jnp.exp2(scores[c] - m_new[c])
        p_bf16.append(p.astype(BF16))
        if ls is not None:
            l_new.append(alpha[c] * ls[c] + jnp.sum(p, axis=0, keepdims=True))
    pv = [jnp.dot(vts[c], p_bf16[c], preferred_element_type=F32) for c in range(n)]
    for c in range(n):
        acc_ref[c] = alpha[c] * acc_ref[c] + pv[c]
    return m_new, (l_new if ls is not None else None)


def _attn_a_kernel(q_ref, k_ref, v_ref, bias_ref, lam_ref, subln_ref, o_ref, acc_ref, vt_ref, *,
                   lam_init):
    qi = pl.program_id(1)
    n_rows = q_ref.shape[0]

    @pl.when(qi == 0)
    def _():
        for b in range(n_rows):
            _transpose_into(vt_ref.at[b], v_ref.at[b])

    qs = [q for b in range(n_rows) for q in _split_heads(q_ref[b])]
    n_chain = len(qs)
    acc_ref[...] = jnp.zeros(acc_ref.shape, F32)
    stat = lambda val: jnp.full((1, ATT_T), val, F32)
    init = (stat(NEG_INF), stat(0.0)) * n_chain
    n_full, odd = _window_split(qi)

    def window(rows, carry, kinds):
        scores, vts = [], []
        for b in range(n_rows):
            for h in range(A_HEADS):
                cols = slice(h * LANES, (h + 1) * LANES)
                k, vt = k_ref[b, rows, cols], vt_ref[b, cols, rows]
                if kinds is not None:
                    bias = jnp.concatenate([bias_ref[h, kind] for kind in kinds], axis=0)
                for half in range(2):
                    s = _qk(k, qs[(b * A_HEADS + h) * 2 + half])
                    scores.append(s if kinds is None else s + bias)
                    vts.append(vt)
        ms, ls = _softmax_stages_t(scores, carry[0::2], acc_ref, vts, ls=carry[1::2])
        return tuple(x for pair in zip(ms, ls) for x in pair)

    n_plain = jnp.maximum(n_full - 1, 0)
    carry = lax.fori_loop(0, n_plain, lambda j, c: window(_rows(j, ATT_WIN), c, None), init)
    near = (jnp.where(odd == 1, 0, 2), jnp.where(odd == 1, 1, 0))
    carry = lax.fori_loop(n_plain, n_full, lambda j, c: window(_rows(j, ATT_WIN), c, near), carry)
    carry = lax.fori_loop(0, 1 - odd, lambda _, c: window(_rows(qi, ATT_T), c, (1,)), carry)

    lv = lam_ref[...]
    lam = (jnp.exp(jnp.sum(lv[0:1] * lv[1:2], axis=-1, keepdims=True))
           - jnp.exp(jnp.sum(lv[2:3] * lv[3:4], axis=-1, keepdims=True)) + lam_init)
    for b in range(n_rows):
        for h in range(A_HEADS):
            c0 = (b * A_HEADS + h) * 2
            l0, l1 = carry[2 * c0 + 1], carry[2 * c0 + 3]
            o = acc_ref[c0] / l0 - lam * (acc_ref[c0 + 1] / l1)
            ms = jnp.mean(o * o, axis=0, keepdims=True)
            o = o * lax.rsqrt(ms + RMS_EPS) * subln_ref[...] * (1.0 - lam_init)
            o_ref[b, :, h * LANES:(h + 1) * LANES] = o.T.astype(o_ref.dtype)


def _attn_a(p_abc, bias_a, lam_vecs, subln, lam_init, batch, s):
    n = p_abc.shape[0]
    rows = ATT_ROWS if batch % ATT_ROWS == 0 else 1
    q_spec, k_spec, v_spec, o_spec = _attn_specs_rows(A_W, 0, s, rows)
    p3 = p_abc.reshape(batch, s, ABC_W)
    out = pl.pallas_call(
        functools.partial(_attn_a_kernel, lam_init=lam_init),
        grid=(batch // rows, s // ATT_T),
        in_specs=[q_spec, k_spec, v_spec,
                  pl.BlockSpec(bias_a.shape, lambda b, i: (0, 0, 0, 0)),
                  pl.BlockSpec((4, HEAD_DIM), lambda b, i: (0, 0)),
                  pl.BlockSpec((2 * HEAD_DIM, ATT_T), lambda b, i: (0, 0))],
        out_specs=o_spec,
        out_shape=jax.ShapeDtypeStruct((batch, s, A_W), BF16),
        scratch_shapes=[pltpu.VMEM((rows * 2 * A_HEADS, LANES, ATT_T), F32),
                        pltpu.VMEM((rows, A_W, s), BF16)],
        compiler_params=_cparams(2),
        name="attn_diff",
    )(p3, p3, p3, bias_a, lam_vecs.astype(F32),
      jnp.broadcast_to(subln.astype(F32)[:, None], (2 * HEAD_DIM, ATT_T)))
    return out.reshape(n, A_W)


def _attn_b_kernel(q_ref, k_ref, v_ref, o_ref, acc_ref, vt_ref):
    qi = pl.program_id(1)
    n_rows = q_ref.shape[0]

    @pl.when(qi == 0)
    def _():
        for b in range(n_rows):
            _transpose_into(vt_ref.at[b], v_ref.at[b])

    qs = [q for b in range(n_rows) for q in _split_heads(q_ref[b])]
    acc_ref[...] = jnp.zeros(acc_ref.shape, F32)
    key = lax.broadcasted_iota(jnp.int32, (ATT_T, ATT_T), 0)
    qry = lax.broadcasted_iota(jnp.int32, (ATT_T, ATT_T), 1)
    from_here = jnp.where(qry >= key, 1.0, 0.0).astype(BF16)
    strict = key < qry

    def step(kj, rsum, diag):
        rows = _rows(kj, ATT_T)
        heads = range(n_rows * B_HEADS)
        row_of = [h // B_HEADS for h in heads]
        cols = [slice(((h % B_HEADS) // 2) * LANES, ((h % B_HEADS) // 2 + 1) * LANES) for h in heads]
        z = [_qk(k_ref[row_of[h], rows, cols[h]], qs[h]) for h in heads]
        neg_z = [-z[h] for h in heads]
        soft = [jnp.log2(1.0 + jnp.exp2(jnp.minimum(z[h], neg_z[h]))) for h in heads]
        u = [jnp.minimum(neg_z[h], 0.0) - soft[h] for h in heads]
        if diag:
            u = [jnp.where(strict, u[h], 0.0) for h in heads]
        a = [jnp.exp2(z[h] + jnp.dot(from_here, u[h].astype(BF16), preferred_element_type=F32)
                      + rsum[h]) for h in heads]
        if diag:
            a = [jnp.where(strict, a[h], 0.0) for h in heads]
        pv = [jnp.dot(vt_ref[row_of[h], cols[h], rows], a[h].astype(BF16),
                      preferred_element_type=F32) for h in heads]
        for h in heads:
            acc_ref[h] = acc_ref[h] + pv[h]
        return tuple(rsum[h] + jnp.sum(u[h], axis=0, keepdims=True) for h in heads)

    zero = jnp.zeros((1, ATT_T), F32)
    rsum = step(qi, (zero,) * (n_rows * B_HEADS), True)
    lax.fori_loop(0, qi, lambda t, c: step(qi - 1 - t, c, False), rsum)
    for b in range(n_rows):
        for p in range(B_HEADS // 2):
            c0 = b * B_HEADS + 2 * p
            o_t = jnp.concatenate([acc_ref[c0][:HEAD_DIM], acc_ref[c0 + 1][HEAD_DIM:]], axis=0)
            o_ref[b, :, p * LANES:(p + 1) * LANES] = o_t.T.astype(o_ref.dtype)


def _attn_b(p_abc, batch, s):
    n = p_abc.shape[0]
    rows = ATT_ROWS if batch % ATT_ROWS == 0 else 1
    q_spec, k_spec, v_spec, o_spec = _attn_specs_rows(B_W, 3 * A_W // B_W, s, rows)
    p3 = p_abc.reshape(batch, s, ABC_W)
    out = pl.pallas_call(
        _attn_b_kernel,
        grid=(batch // rows, s // ATT_T),
        in_specs=[q_spec, k_spec, v_spec],
        out_specs=o_spec,
        out_shape=jax.ShapeDtypeStruct((batch, s, B_W), BF16),
        scratch_shapes=[pltpu.VMEM((rows * B_HEADS, LANES, ATT_T), F32),
                        pltpu.VMEM((rows, B_W, s), BF16)],
        compiler_params=_cparams(2),
        name="attn_stick",
    )(p3, p3, p3)
    return out.reshape(n, B_W)


def _attn_c_kernel(q_ref, k_ref, v_ref, cf_ref, fb_ref, o_ref, acc_ref, cumk_ref, cumr_ref, vaug_ref):
    qi = pl.program_id(1)
    s_len = cf_ref.shape[0]

    @pl.when(qi == 0)
    def _():
        r = lax.broadcasted_iota(jnp.int32, (LANES, LANES), 0)
        c = lax.broadcasted_iota(jnp.int32, (LANES, LANES), 1)
        lower = jnp.where(c <= r, 1.0, 0.0).astype(BF16)
        carry = jnp.zeros((1, LANES), F32)
        for blk in range(s_len // LANES):
            rows = slice(blk * LANES, (blk + 1) * LANES)
            log_f = _log_sigmoid(cf_ref[rows, :] + fb_ref[...])
            cum = _dot_exact_lhs(lower, log_f) + carry
            cumr_ref[:, rows] = (cum * LOG2E).T
            for h in range(C_HEADS):
                cumk_ref[h, rows, :] = jnp.broadcast_to(cum[:, h:h + 1] * LOG2E, (LANES, LANES))
            carry = cum[LANES - 1:LANES, :]
        own = lax.broadcasted_iota(jnp.int32, (LANES, ATT_T), 0) < HEAD_DIM
        for p in range(C_HEADS // 2):
            for blk in range(s_len // ATT_T):
                tile = v_ref[blk * ATT_T:(blk + 1) * ATT_T, p * LANES:(p + 1) * LANES]
                tile_t = tile.astype(F32).T
                dst = (slice(p * LANES, (p + 1) * LANES), slice(blk * ATT_T, (blk + 1) * ATT_T))
                vaug_ref[(0,) + dst] = jnp.where(own, tile_t, 1.0).astype(BF16)
                vaug_ref[(1,) + dst] = jnp.where(own, 1.0, tile_t).astype(BF16)

    qs = _split_heads(q_ref[...])
    acc_ref[...] = jnp.zeros(acc_ref.shape, F32)
    q_cols = _rows(qi, ATT_T)
    cum_q = [cumr_ref[h:h + 1, q_cols] for h in range(C_HEADS)]
    n_full, odd = _window_split(qi)

    def causal(width):
        key = lax.broadcasted_iota(jnp.int32, (width, ATT_T), 0) - (width - ATT_T)
        return jnp.where(key <= lax.broadcasted_iota(jnp.int32, (width, ATT_T), 1), 0.0, NEG_INF)

    def window(rows, carry, mask):
        scores, vts = [], []
        for h in range(C_HEADS):
            cols = slice((h // 2) * LANES, (h // 2 + 1) * LANES)
            cum_k = cumk_ref[h, rows, :]
            s = _qk(k_ref[rows, cols], qs[h]) + cum_q[h] - jnp.concatenate([cum_k, cum_k], axis=1)
            scores.append(s if mask is None else s + mask)
            vts.append(vaug_ref[h % 2, cols, rows])
        return tuple(_softmax_stages_t(scores, carry, acc_ref, vts)[0])

    init = (jnp.full((1, ATT_T), NEG_INF, F32),) * C_HEADS
    carry = lax.fori_loop(0, n_full - odd, lambda j, c: window(_rows(j, ATT_WIN), c, None), init)
    carry = lax.fori_loop(n_full - odd, n_full,
                          lambda j, c: window(_rows(j, ATT_WIN), c, causal(ATT_WIN)), carry)
    lax.fori_loop(0, 1 - odd, lambda _, c: window(_rows(qi, ATT_T), c, causal(ATT_T)), carry)
    for p in range(C_HEADS // 2):
        a0, a1 = acc_ref[2 * p], acc_ref[2 * p + 1]
        num = jnp.concatenate([a0[:HEAD_DIM], a1[HEAD_DIM:]], axis=0)
        den = jnp.concatenate([a0[HEAD_DIM:], a1[:HEAD_DIM]], axis=0)
        o_ref[:, p * LANES:(p + 1) * LANES] = (num / den).T.astype(o_ref.dtype)


def _attn_c(p_abc, cf, forget_bias, batch, s):
    n = p_abc.shape[0]
    q_spec, k_spec, v_spec, o_spec = _attn_specs(C_W, 3 * (A_W + B_W) // C_W, s)
    fb = jnp.pad(forget_bias.astype(F32), (0, LANES - C_HEADS)).reshape(1, LANES)
    return pl.pallas_call(
        _attn_c_kernel,
        grid=(batch, s // ATT_T),
        in_specs=[q_spec, k_spec, v_spec,
                  pl.BlockSpec((s, LANES), lambda b, i: (b, 0)),
                  pl.BlockSpec((1, LANES), lambda b, i: (0, 0))],
        out_specs=o_spec,
        out_shape=jax.ShapeDtypeStruct((n, C_W), BF16),
        scratch_shapes=[pltpu.VMEM((C_HEADS, LANES, ATT_T), F32),
                        pltpu.VMEM((C_HEADS, s, LANES), F32),
                        pltpu.VMEM((LANES, s), F32),
                        pltpu.VMEM((2, C_W, s), BF16)],
        compiler_params=_cparams(2),
        name="attn_forget",
    )(p_abc, p_abc, p_abc, cf, fb)


def _attn_d_kernel(*refs):
    qkv_refs, (bias_ref, o_ref, qc_ref, kc_ref, vc_ref, oc_ref, lc_ref, og_ref, lse_ref) = (
        refs[:3 * D_GROUPS], refs[3 * D_GROUPS:])
    s_len = o_ref.shape[0]
    n_tiles = s_len // D_T
    first = _lane_half((D_T, LANES))
    kc_ref[0:D_T, :] = jnp.zeros((D_T, LANES), BF16)
    vc_ref[0:D_T, :] = jnp.zeros((D_T, LANES), BF16)

    for g, (_, dil) in enumerate(D_PAIRS):
        q_ref, k_ref, v_ref = qkv_refs[g], qkv_refs[D_GROUPS + g], qkv_refs[2 * D_GROUPS + g]
        class_len = s_len // dil
        tiles_per_class = class_len // D_T

        for r in range(dil):
            src = pl.ds(r, class_len, stride=dil) if dil > 1 else slice(None)
            dst = slice(r * class_len, (r + 1) * class_len)
            dst_kv = slice(D_T + r * class_len, D_T + (r + 1) * class_len)
            qc_ref[dst, :] = q_ref[src, :].astype(BF16)
            kc_ref[dst_kv, :] = k_ref[src, :].astype(BF16)
            vc_ref[dst_kv, :] = v_ref[src, :].astype(BF16)

        def tile_group(i, carry):
            tiles = [i * D_UNROLL + u for u in range(D_UNROLL)]
            rows = [pl.ds(pl.multiple_of(t * D_T, D_T), D_T) for t in tiles]
            if tiles_per_class > 1:
                keys = [pl.ds(pl.multiple_of(t * D_T, D_T), 2 * D_T) for t in tiles]
                variant = [jnp.where(t % tiles_per_class == 0, 1, 0) for t in tiles]
                bias = [[bias_ref[g, h, variant[u]] for h in range(2)] for u in range(D_UNROLL)]
            else:
                keys = [pl.ds(pl.multiple_of(t * D_T + D_T, D_T), D_T) for t in tiles]
                bias = [[bias_ref[g, h, 1][:, D_T:] for h in range(2)]] * D_UNROLL
            chains = [(u, h) for u in range(D_UNROLL) for h in range(2)]
            qh = [_split_heads(qc_ref[rows[u], :]) for u in range(D_UNROLL)]
            sc = [_qk(qh[u][h], kc_ref[keys[u], :]) + bias[u][h] for u, h in chains]
            m = [jnp.max(s, axis=-1, keepdims=True) for s in sc]
            e = [jnp.exp(s - mm) for s, mm in zip(sc, m)]
            l = [jnp.sum(x, axis=-1, keepdims=True) for x in e]
            o = [jnp.dot(x.astype(BF16), vc_ref[keys[u], :], preferred_element_type=F32)
                 for x, (u, h) in zip(e, chains)]
            o = [x / ll for x, ll in zip(o, l)]
            lse = [jnp.broadcast_to(mm + jnp.log(ll), (D_T, LANES)) for mm, ll in zip(m, l)]
            for u in range(D_UNROLL):
                oc_ref[rows[u], :] = jnp.where(first, o[2 * u], o[2 * u + 1])
                lc_ref[rows[u], :] = jnp.where(first, lse[2 * u], lse[2 * u + 1])
            return carry

        lax.fori_loop(0, n_tiles // D_UNROLL, tile_group, 0)

        for r in range(dil):
            dst = pl.ds(r, class_len, stride=dil) if dil > 1 else slice(None)
            src = slice(r * class_len, (r + 1) * class_len)
            og_ref[g, dst, :] = oc_ref[src, :]
            lse_ref[g, dst, :] = lc_ref[src, :]

    lse = [lse_ref[g] for g in range(D_GROUPS)]
    top = functools.reduce(jnp.maximum, lse)
    w = [jnp.exp(x - top) for x in lse]
    num = sum(w[g] * og_ref[g] for g in range(D_GROUPS))
    o_ref[...] = (num / sum(w)).astype(o_ref.dtype)


def _attn_d(p_d, bias_d, batch, s):
    n = p_d.shape[0]
    return pl.pallas_call(
        _attn_d_kernel,
        grid=(batch,),
        in_specs=([pl.BlockSpec((s, LANES), functools.partial(lambda b, c: (b, c), c=c))
                   for c in range(3 * D_GROUPS)]
                  + [pl.BlockSpec(bias_d.shape, lambda b: (0, 0, 0, 0, 0))]),
        out_specs=pl.BlockSpec((s, LANES), lambda b: (b, 0)),
        out_shape=jax.ShapeDtypeStruct((n, LANES), BF16),
        scratch_shapes=[pltpu.VMEM((s, LANES), BF16), pltpu.VMEM((s + D_T, LANES), BF16),
                        pltpu.VMEM((s + D_T, LANES), BF16),
                        pltpu.VMEM((s, LANES), F32), pltpu.VMEM((s, LANES), F32),
                        pltpu.VMEM((D_GROUPS, s, LANES), F32),
                        pltpu.VMEM((D_GROUPS, s, LANES), F32)],
        compiler_params=_cparams(1),
        name="attn_dilated",
    )(*([p_d] * (3 * D_GROUPS)), bias_d)


def _merge_kernel(x_ref, g_ref, wg_ref, gb_ref, oa_ref, ob_ref, oc_ref, od_ref,
                  wa_ref, wb_ref, wc_ref, wd_ref, wout_ref, o_ref):
    x = x_ref[...]
    h = _rms(x, g_ref[...]).astype(BF16)
    merged = jnp.zeros(x.shape, F32)
    branches = ((oa_ref, wa_ref), (ob_ref, wb_ref), (oc_ref, wc_ref), (od_ref, wd_ref))
    for i, (br_ref, w_ref) in enumerate(branches):
        cols = slice(i * D_MODEL, (i + 1) * D_MODEL)
        logits = jnp.dot(h, wg_ref[:, cols], preferred_element_type=F32) + gb_ref[:, cols]
        branch = jnp.dot(br_ref[...], w_ref[...], preferred_element_type=F32)
        merged = merged + jax.nn.sigmoid(logits) * branch
    o_ref[...] = x + jnp.dot(merged.astype(BF16), wout_ref[...], preferred_element_type=F32)


def _merge(x2, gain, w_all, layer, gate_bias, outs, w_branch, w_out):
    n = x2.shape[0]
    gate_w = N_BRANCH * D_MODEL
    widths = (A_W, B_W, C_W, 2 * HEAD_DIM)
    offs = np.concatenate([[0], np.cumsum(widths)])
    w_br = [w_branch[offs[i]:offs[i + 1]].astype(BF16) for i in range(N_BRANCH)]
    const = lambda i: (0, 0)
    row = lambda i: (i, 0)
    return pl.pallas_call(
        _merge_kernel,
        grid=(n // TOK_TILE,),
        in_specs=([pl.BlockSpec((TOK_TILE, D_MODEL), row),
                   pl.BlockSpec((1, D_MODEL), const),
                   pl.BlockSpec((None, D_MODEL, gate_w), lambda i: (layer, 0, 0),
                                pipeline_mode=pl.Buffered(1)),
                   pl.BlockSpec((1, gate_w), const)]
                  + [pl.BlockSpec((TOK_TILE, w), row) for w in widths]
                  + [pl.BlockSpec((w, D_MODEL), const) for w in widths]
                  + [pl.BlockSpec((D_MODEL, D_MODEL), const)]),
        out_specs=pl.BlockSpec((TOK_TILE, D_MODEL), row),
        out_shape=jax.ShapeDtypeStruct((n, D_MODEL), F32),
        compiler_params=_cparams(1),
        name="merge",
    )(x2, gain.reshape(1, D_MODEL), w_all, gate_bias.reshape(1, -1).astype(F32),
      *outs, *w_br, w_out.astype(BF16))


def kernel(x, rel_table, ffn1_norm, ffn1_w_in, ffn1_w_out, mix_norm, w_in, gate_bias, forget_bias,
           a_q_norm, a_k_norm, a_lambda, a_subln, c_q_norm, c_k_norm, d_q_norm, d_k_norm,
           w_branch, w_out, ffn2_norm, ffn2_w_in, ffn2_w_out):
    batch, s, _ = x.shape
    depth = w_in.shape[0]
    assert s % (D_T * D_PAIRS[-1][1]) == 0 and (batch * s) % TOK_TILE == 0
    bias_a, bias_d = _bias_tiles(rel_table)
    w_all = _prep_w_in(w_in)
    x2 = x.reshape(batch * s, D_MODEL)
    for l in range(depth):
        x2 = _ffn(x2, ffn1_norm[l], ffn1_w_in[l], ffn1_w_out[l])
        p_abc, p_d, cf = _proj(x2, mix_norm[l], w_all, l, a_q_norm[l], a_k_norm[l],
                               c_q_norm[l], c_k_norm[l], d_q_norm[l], d_k_norm[l])
        lam_init = 0.8 - 0.6 * math.exp(-0.3 * l)
        oa = _attn_a(p_abc, bias_a, a_lambda[l], a_subln[l], lam_init, batch, s)
        ob = _attn_b(p_abc, batch, s)
        oc = _attn_c(p_abc, cf, forget_bias[l], batch, s)
        od = _attn_d(p_d, bias_d, batch, s)
        x2 = _merge(x2, mix_norm[l], w_all, l, gate_bias[l], (oa, ob, oc, od),
                    w_branch[l], w_out[l])
        x2 = _ffn(x2, ffn2_norm[l], ffn2_w_in[l], ffn2_w_out[l])
    return x2.reshape(batch, s, D_MODEL)
```

```python
import functools
import math

import numpy as np
import jax
import jax.numpy as jnp
from jax import lax
from jax.experimental import pallas as pl
from jax.experimental.pallas import tpu as pltpu

F32 = jnp.float32
BF16 = jnp.bfloat16

D_MODEL = 1024
HEAD_DIM = 64
SCALE = HEAD_DIM ** -0.5
A_HEADS = 4
B_HEADS = 6
C_HEADS = 6
D_PAIRS = ((128, 1), (512, 4), (2048, 16))
D_GROUPS = len(D_PAIRS)
N_BRANCH = 4
FFN_HIDDEN = 2816
REL_BUCKETS = 32
REL_MAX_DIST = 128
RMS_EPS = 1e-6
NEG_INF = -1e30

LANES = 128
A_W = A_HEADS * 2 * HEAD_DIM
B_W = B_HEADS * HEAD_DIM
C_W = C_HEADS * HEAD_DIM
D_W = D_GROUPS * 2 * HEAD_DIM
ABC_W = 3 * (A_W + B_W + C_W)

W_SHIFT = (ABC_W + C_HEADS) % LANES
W_ALIGNED_BLOCKS = ABC_W // LANES + 1
W_ABC_OFF = N_BRANCH * D_MODEL
W_CF_OFF = W_ABC_OFF + ABC_W
W_D_OFF = W_ABC_OFF + W_ALIGNED_BLOCKS * LANES
W_D_END = W_D_OFF + 3 * D_W
GROUP_W = 256

TOK_TILE = 512
FFN_CHUNK = 256
ATT_T = 256
ATT_WIN = 2 * ATT_T
LOG2E = math.log2(math.e)
ATT_ROWS = 2
D_UNROLL = 4
D_T = 128
VMEM_LIMIT = 56 * 1024 * 1024


def _cparams(n_axes):
    return pltpu.CompilerParams(dimension_semantics=("arbitrary",) * n_axes,
                                vmem_limit_bytes=VMEM_LIMIT)


def _rms(x, g):
    ms = jnp.mean(x * x, axis=-1, keepdims=True)
    return x * lax.rsqrt(ms + RMS_EPS) * g


def _split3(a):
    hi = a.astype(BF16)
    r1 = a - hi.astype(F32)
    mid = r1.astype(BF16)
    lo = (r1 - mid.astype(F32)).astype(BF16)
    return hi, mid, lo


def _dot_exact_lhs(a_bf16, b):
    hi, mid, lo = _split3(b)
    d = functools.partial(jnp.dot, preferred_element_type=F32)
    return d(a_bf16, hi) + d(a_bf16, mid) + d(a_bf16, lo)


def _qk(q, k):
    return lax.dot_general(q, k, (((1,), (1,)), ((), ())), preferred_element_type=F32)


def _log_sigmoid(x):
    return jnp.minimum(x, 0.0) - jnp.log(1.0 + jnp.exp(-jnp.abs(x)))


def _lane_half(shape):
    return lax.broadcasted_iota(jnp.int32, shape, len(shape) - 1) < HEAD_DIM


def _ffn_kernel(x_ref, g_ref, wi_ref, wo_ref, o_ref):
    x = x_ref[...]
    h = _rms(x, g_ref[...]).astype(BF16)
    acc = jnp.zeros(x.shape, F32)
    for c in range(FFN_HIDDEN // FFN_CHUNK):
        lo = c * FFN_CHUNK
        gate = jnp.dot(h, wi_ref[:, lo:lo + FFN_CHUNK], preferred_element_type=F32)
        up = jnp.dot(h, wi_ref[:, FFN_HIDDEN + lo:FFN_HIDDEN + lo + FFN_CHUNK],
                     preferred_element_type=F32)
        a = (gate * jax.nn.sigmoid(gate) * up).astype(BF16)
        acc = acc + jnp.dot(a, wo_ref[c * FFN_CHUNK:(c + 1) * FFN_CHUNK, :],
                            preferred_element_type=F32)
    o_ref[...] = x + 0.5 * acc


def _ffn(x2, gain, w_i, w_o):
    n = x2.shape[0]
    const = lambda i: (0, 0)
    return pl.pallas_call(
        _ffn_kernel,
        grid=(n // TOK_TILE,),
        in_specs=[
            pl.BlockSpec((TOK_TILE, D_MODEL), lambda i: (i, 0)),
            pl.BlockSpec((1, D_MODEL), const),
            pl.BlockSpec((D_MODEL, 2 * FFN_HIDDEN), const, pipeline_mode=pl.Buffered(1)),
            pl.BlockSpec((FFN_HIDDEN, D_MODEL), const, pipeline_mode=pl.Buffered(1)),
        ],
        out_specs=pl.BlockSpec((TOK_TILE, D_MODEL), lambda i: (i, 0)),
        out_shape=jax.ShapeDtypeStruct((n, D_MODEL), F32),
        compiler_params=_cparams(1),
        name="ffn",
    )(x2, gain.reshape(1, D_MODEL), w_i.astype(BF16), w_o.astype(BF16))


_ABC_SECTIONS = ((A_W, True), (A_W, True), (A_W, False),
                 (B_W, False), (B_W, False), (B_W, False),
                 (C_W, True), (C_W, True), (C_W, False))
_D_SECTIONS = ((D_W, True), (D_W, True), (D_W, False))


def _proj_sections(y, gain_ref, gmat, out_ref, sections):
    off = 0
    for width, normed in sections:
        lo = off
        while lo < off + width:
            blk = min(GROUP_W, off + width - lo)
            yb = y[:, lo:lo + blk]
            if normed:
                ms = jnp.dot((yb * yb).astype(BF16), gmat[:blk, :blk], preferred_element_type=F32)
                yb = yb * lax.rsqrt(ms + RMS_EPS)
            out_ref[:, lo:lo + blk] = (yb * gain_ref[:, lo:lo + blk]).astype(out_ref.dtype)
            lo += blk
        off += width


def _proj_kernel(x_ref, g_ref, w_ref, gabc_ref, gd_ref, gmat_ref, pabc_ref, pd_ref, cf_ref):
    h = _rms(x_ref[...], g_ref[...]).astype(BF16)
    gmat = gmat_ref[...]
    y_abc = jnp.dot(h, w_ref[:, W_ABC_OFF:W_CF_OFF], preferred_element_type=F32)
    _proj_sections(y_abc, gabc_ref, gmat, pabc_ref, _ABC_SECTIONS)
    y_d = jnp.dot(h, w_ref[:, W_CF_OFF:W_D_END], preferred_element_type=F32)
    cf_ref[...] = y_d[:, :LANES]
    _proj_sections(y_d[:, LANES:], gd_ref, gmat, pd_ref, _D_SECTIONS)


def _head_gain(gain, width, scale):
    return jnp.tile(gain.astype(F32), width // HEAD_DIM) * scale


def _win_kernel(a_ref, b_ref, o_ref):
    c = pl.program_id(1)

    @pl.when(c < W_ALIGNED_BLOCKS)
    def _():
        o_ref[...] = b_ref[...].astype(BF16)

    @pl.when(c >= W_ALIGNED_BLOCKS)
    def _():
        both = jnp.concatenate([a_ref[:, W_SHIFT:], b_ref[:, :W_SHIFT]], axis=1)
        o_ref[...] = both.astype(BF16)


def _prep_w_in(w_in):
    depth = w_in.shape[0]
    n_blocks = pl.cdiv(w_in.shape[2], LANES)
    gate_block0 = (W_D_END - W_ABC_OFF) // LANES
    return pl.pallas_call(
        _win_kernel,
        grid=(depth, n_blocks),
        in_specs=[pl.BlockSpec((None, D_MODEL, LANES), lambda l, c: (l, 0, jnp.maximum(c - 1, 0))),
                  pl.BlockSpec((None, D_MODEL, LANES), lambda l, c: (l, 0, c))],
        out_specs=pl.BlockSpec(
            (None, D_MODEL, LANES),
            lambda l, c: (l, 0, jnp.where(c < gate_block0, c + W_ABC_OFF // LANES, c - gate_block0))),
        out_shape=jax.ShapeDtypeStruct((depth, D_MODEL, W_D_END), BF16),
        compiler_params=_cparams(2),
        name="prep_w_in",
    )(w_in, w_in)


def _proj(x2, gain, w_all, layer, a_q, a_k, c_q, c_k, d_q, d_k):
    n = x2.shape[0]
    ones = lambda w, s: jnp.full((w,), s, F32)
    g_abc = jnp.concatenate([
        _head_gain(a_q, A_W, SCALE * LOG2E), _head_gain(a_k, A_W, 1.0), ones(A_W, 1.0),
        ones(B_W, SCALE * LOG2E), ones(B_W, 1.0), ones(B_W, 1.0),
        _head_gain(c_q, C_W, SCALE * LOG2E), _head_gain(c_k, C_W, 1.0),
        ones(C_W, 1.0)]).reshape(1, ABC_W)
    g_d = jnp.concatenate([_head_gain(d_q, D_W, SCALE), _head_gain(d_k, D_W, 1.0),
                           ones(D_W, 1.0)]).reshape(1, 3 * D_W)
    lane = np.arange(GROUP_W)
    gmat = jnp.asarray((lane[:, None] // HEAD_DIM == lane[None, :] // HEAD_DIM) / HEAD_DIM, BF16)
    const = lambda i: (0, 0)
    row = lambda i: (i, 0)
    return pl.pallas_call(
        _proj_kernel,
        grid=(n // TOK_TILE,),
        in_specs=[
            pl.BlockSpec((TOK_TILE, D_MODEL), row),
            pl.BlockSpec((1, D_MODEL), const),
            pl.BlockSpec((None, D_MODEL, W_D_END), lambda i: (layer, 0, 0),
                         pipeline_mode=pl.Buffered(1)),
            pl.BlockSpec((1, ABC_W), const),
            pl.BlockSpec((1, 3 * D_W), const),
            pl.BlockSpec((GROUP_W, GROUP_W), const),
        ],
        out_specs=[
            pl.BlockSpec((TOK_TILE, ABC_W), row),
            pl.BlockSpec((TOK_TILE, 3 * D_W), row),
            pl.BlockSpec((TOK_TILE, LANES), row),
        ],
        out_shape=[
            jax.ShapeDtypeStruct((n, ABC_W), BF16),
            jax.ShapeDtypeStruct((n, 3 * D_W), F32),
            jax.ShapeDtypeStruct((n, LANES), F32),
        ],
        compiler_params=_cparams(1),
        name="proj",
    )(x2, gain.reshape(1, D_MODEL), w_all, g_abc, g_d, gmat)


_MASKED_BUCKET = REL_BUCKETS


def _rel_bucket_np(n):
    n = np.asarray(n, np.int64)
    max_exact = REL_BUCKETS // 2
    nf = np.maximum(n, 1).astype(np.float32)
    large = max_exact + (np.log(nf / np.float32(max_exact)) / np.float32(math.log(REL_MAX_DIST / max_exact))
                         * np.float32(REL_BUCKETS - max_exact)).astype(np.int32)
    large = np.minimum(large, REL_BUCKETS - 1)
    return np.where(n < max_exact, n, large).astype(np.int32)


def _bucket_maps():
    t = ATT_T
    x, y = np.arange(t)[:, None], np.arange(t)[None, :]
    diag = np.where(x >= y, _rel_bucket_np(np.maximum(x - y, 0)), _MASKED_BUCKET)
    prev = _rel_bucket_np(t + x - y)
    idx_a = np.stack([prev.T, diag.T]).astype(np.int32)
    x, y = np.arange(D_T)[:, None], np.arange(D_T)[None, :]
    idx_d = np.empty((D_GROUPS, 2, D_T, 2 * D_T), np.int32)
    for g, (_, dil) in enumerate(D_PAIRS):
        prev = np.where(y >= x, _rel_bucket_np(dil * (D_T + x - y)), _MASKED_BUCKET)
        diag = np.where(y <= x, _rel_bucket_np(dil * np.maximum(x - y, 0)), _MASKED_BUCKET)
        idx_d[g, 0] = np.concatenate([prev, diag], axis=1)
        idx_d[g, 1] = np.concatenate([np.full_like(prev, _MASKED_BUCKET), diag], axis=1)
    return idx_a, idx_d


def _bias_kernel(tbl_ref, idxa_ref, idxd_ref, ba_ref, bd_ref):
    for h in range(A_HEADS):
        for kind in range(2):
            ba_ref[h, kind] = jnp.where(idxa_ref[kind] == _MASKED_BUCKET, NEG_INF, 0.0)
        ba_ref[h, 2] = jnp.zeros((ATT_T, ATT_T), F32)
        ba_ref[h, 3] = jnp.full((ATT_T, ATT_T), NEG_INF, F32)
    for g in range(D_GROUPS):
        for h in range(2):
            for var in range(2):
                bd_ref[g, h, var] = jnp.where(idxd_ref[g, var] == _MASKED_BUCKET, NEG_INF, 0.0)

    def body(b, carry):
        for h in range(A_HEADS):
            val = (tbl_ref[b, h] - tbl_ref[REL_BUCKETS - 1, h]) * LOG2E
            for kind in range(2):
                ba_ref[h, kind] = jnp.where(idxa_ref[kind] == b, val, ba_ref[h, kind])
        for g in range(D_GROUPS):
            for h in range(2):
                val = tbl_ref[b, A_HEADS + 2 * g + h]
                for var in range(2):
                    bd_ref[g, h, var] = jnp.where(idxd_ref[g, var] == b, val, bd_ref[g, h, var])
        return carry

    lax.fori_loop(0, REL_BUCKETS, body, 0)


def _bias_tiles(rel_table):
    idx_a, idx_d = _bucket_maps()
    return pl.pallas_call(
        _bias_kernel,
        in_specs=[pl.BlockSpec(memory_space=pltpu.SMEM),
                  pl.BlockSpec(memory_space=pltpu.VMEM),
                  pl.BlockSpec(memory_space=pltpu.VMEM)],
        out_specs=[pl.BlockSpec(memory_space=pltpu.VMEM),
                   pl.BlockSpec(memory_space=pltpu.VMEM)],
        out_shape=[jax.ShapeDtypeStruct((A_HEADS, 4, ATT_T, ATT_T), F32),
                   jax.ShapeDtypeStruct((D_GROUPS, 2, 2, D_T, 2 * D_T), F32)],
        compiler_params=pltpu.CompilerParams(vmem_limit_bytes=VMEM_LIMIT),
        name="rel_bias",
    )(rel_table.astype(F32), jnp.asarray(idx_a), jnp.asarray(idx_d))


def _rows(idx, size):
    return pl.ds(pl.multiple_of(idx * size, size), size)


def _split_heads(q):
    first = _lane_half((q.shape[0], LANES))
    zero = jnp.zeros((q.shape[0], LANES), q.dtype)
    out = []
    for p in range(q.shape[1] // LANES):
        qp = q[:, p * LANES:(p + 1) * LANES]
        out += [jnp.where(first, qp, zero), jnp.where(first, zero, qp)]
    return out


def _transpose_into(vt_ref, v_ref):
    for p in range(v_ref.shape[1] // LANES):
        for blk in range(v_ref.shape[0] // ATT_T):
            tile = v_ref[blk * ATT_T:(blk + 1) * ATT_T, p * LANES:(p + 1) * LANES]
            vt_ref[p * LANES:(p + 1) * LANES, blk * ATT_T:(blk + 1) * ATT_T] = (
                tile.astype(F32).T.astype(vt_ref.dtype))


def _attn_specs_rows(width, col0, s, rows):
    q_spec = pl.BlockSpec((rows, ATT_T, width), lambda b, i: (b, i, col0))
    k_spec = pl.BlockSpec((rows, s, width), lambda b, i: (b, 0, col0 + 1))
    v_spec = pl.BlockSpec((rows, s, width), lambda b, i: (b, 0, col0 + 2))
    o_spec = pl.BlockSpec((rows, ATT_T, width), lambda b, i: (b, i, 0))
    return q_spec, k_spec, v_spec, o_spec


def _window_split(qi):
    return lax.shift_right_logical(qi + 1, 1), jnp.bitwise_and(qi, 1)


def _softmax_stages_t(scores, ms, acc_ref, vts, ls=None):
    n = len(scores)
    m_new = [jnp.maximum(ms[c], jnp.max(scores[c], axis=0, keepdims=True)) for c in range(n)]
    alpha = [jnp.exp2(ms[c] - m_new[c]) for c in range(n)]
    p_bf16, l_new = [], []
    for c in range(n):
        p = jnp.exp2(scores[c] - m_new[c])
        p_bf16.append(p.astype(BF16))
        if ls is not None:
            l_new.append(alpha[c] * ls[c] + jnp.sum(p, axis=0, keepdims=True))
    pv = [jnp.dot(vts[c], p_bf16[c], preferred_element_type=F32) for c in range(n)]
    for c in range(n):
        acc_ref[c] = alpha[c] * acc_ref[c] + pv[c]
    return m_new, (l_new if ls is not None else None)


def _attn_a_kernel(q_ref, k_ref, v_ref, bias_ref, lam_ref, subln_ref, o_ref, acc_ref, vt_ref, *,
                   lam_init):
    qi = pl.program_id(1)
    n_rows = q_ref.shape[0]

    @pl.when(qi == 0)
    def _():
        for b in range(n_rows):
            _transpose_into(vt_ref.at[b], v_ref.at[b])

    qs = [q for b in range(n_rows) for q in _split_heads(q_ref[b])]
    n_chain = len(qs)
    acc_ref[...] = jnp.zeros(acc_ref.shape, F32)
    stat = lambda val: jnp.full((1, ATT_T), val, F32)
    init = (stat(NEG_INF), stat(0.0)) * n_chain
    n_full, odd = _window_split(qi)

    def window(rows, carry, kinds):
        scores, vts = [], []
        for b in range(n_rows):
            for h in range(A_HEADS):
                cols = slice(h * LANES, (h + 1) * LANES)
                k, vt = k_ref[b, rows, cols], vt_ref[b, cols, rows]
                if kinds is not None:
                    bias = jnp.concatenate([bias_ref[h, kind] for kind in kinds], axis=0)
                for half in range(2):
                    s = _qk(k, qs[(b * A_HEADS + h) * 2 + half])
                    scores.append(s if kinds is None else s + bias)
                    vts.append(vt)
        ms, ls = _softmax_stages_t(scores, carry[0::2], acc_ref, vts, ls=carry[1::2])
        return tuple(x for pair in zip(ms, ls) for x in pair)

    n_plain = jnp.maximum(n_full - 1, 0)
    carry = lax.fori_loop(0, n_plain, lambda j, c: window(_rows(j, ATT_WIN), c, None), init)
    near = (jnp.where(odd == 1, 0, 2), jnp.where(odd == 1, 1, 0))
    carry = lax.fori_loop(n_plain, n_full, lambda j, c: window(_rows(j, ATT_WIN), c, near), carry)
    carry = lax.fori_loop(0, 1 - odd, lambda _, c: window(_rows(qi, ATT_T), c, (1,)), carry)

    lv = lam_ref[...]
    lam = (jnp.exp(jnp.sum(lv[0:1] * lv[1:2], axis=-1, keepdims=True))
           - jnp.exp(jnp.sum(lv[2:3] * lv[3:4], axis=-1, keepdims=True)) + lam_init)
    for b in range(n_rows):
        for h in range(A_HEADS):
            c0 = (b * A_HEADS + h) * 2
            l0, l1 = carry[2 * c0 + 1], carry[2 * c0 + 3]
            o = acc_ref[c0] / l0 - lam * (acc_ref[c0 + 1] / l1)
            ms = jnp.mean(o * o, axis=0, keepdims=True)
            o = o * lax.rsqrt(ms + RMS_EPS) * subln_ref[...] * (1.0 - lam_init)
            o_ref[b, :, h * LANES:(h + 1) * LANES] = o.T.astype(o_ref.dtype)


def _attn_a(p_abc, bias_a, lam_vecs, subln, lam_init, batch, s):
    n = p_abc.shape[0]
    rows = ATT_ROWS if batch % ATT_ROWS == 0 else 1
    q_spec, k_spec, v_spec, o_spec = _attn_specs_rows(A_W, 0, s, rows)
    p3 = p_abc.reshape(batch, s, ABC_W)
    out = pl.pallas_call(
        functools.partial(_attn_a_kernel, lam_init=lam_init),
        grid=(batch // rows, s // ATT_T),
        in_specs=[q_spec, k_spec, v_spec,
                  pl.BlockSpec(bias_a.shape, lambda b, i: (0, 0, 0, 0)),
                  pl.BlockSpec((4, HEAD_DIM), lambda b, i: (0, 0)),
                  pl.BlockSpec((2 * HEAD_DIM, ATT_T), lambda b, i: (0, 0))],
        out_specs=o_spec,
        out_shape=jax.ShapeDtypeStruct((batch, s, A_W), BF16),
        scratch_shapes=[pltpu.VMEM((rows * 2 * A_HEADS, LANES, ATT_T), F32),
                        pltpu.VMEM((rows, A_W, s), BF16)],
        compiler_params=_cparams(2),
        name="attn_diff",
    )(p3, p3, p3, bias_a, lam_vecs.astype(F32),
      jnp.broadcast_to(subln.astype(F32)[:, None], (2 * HEAD_DIM, ATT_T)))
    return out.reshape(n, A_W)


def _attn_b_kernel(q_ref, k_ref, v_ref, o_ref, acc_ref, vt_ref):
    qi = pl.program_id(1)
    n_rows = q_ref.shape[0]

    @pl.when(qi == 0)
    def _():
        for b in range(n_rows):
            _transpose_into(vt_ref.at[b], v_ref.at[b])

    qs = [q for b in range(n_rows) for q in _split_heads(q_ref[b])]
    acc_ref[...] = jnp.zeros(acc_ref.shape, F32)
    key = lax.broadcasted_iota(jnp.int32, (ATT_T, ATT_T), 0)
    qry = lax.broadcasted_iota(jnp.int32, (ATT_T, ATT_T), 1)
    from_here = jnp.where(qry >= key, 1.0, 0.0).astype(BF16)
    strict = key < qry

    def step(kj, rsum, diag):
        rows = _rows(kj, ATT_T)
        heads = range(n_rows * B_HEADS)
        row_of = [h // B_HEADS for h in heads]
        cols = [slice(((h % B_HEADS) // 2) * LANES, ((h % B_HEADS) // 2 + 1) * LANES) for h in heads]
        z = [_qk(k_ref[row_of[h], rows, cols[h]], qs[h]) for h in heads]
        neg_z = [-z[h] for h in heads]
        soft = [jnp.log2(1.0 + jnp.exp2(jnp.minimum(z[h], neg_z[h]))) for h in heads]
        u = [jnp.minimum(neg_z[h], 0.0) - soft[h] for h in heads]
        if diag:
            u = [jnp.where(strict, u[h], 0.0) for h in heads]
        a = [jnp.exp2(z[h] + jnp.dot(from_here, u[h].astype(BF16), preferred_element_type=F32)
                      + rsum[h]) for h in heads]
        if diag:
            a = [jnp.where(strict, a[h], 0.0) for h in heads]
        pv = [jnp.dot(vt_ref[row_of[h], cols[h], rows], a[h].astype(BF16),
                      preferred_element_type=F32) for h in heads]
        for h in heads:
            acc_ref[h] = acc_ref[h] + pv[h]
        return tuple(rsum[h] + jnp.sum(u[h], axis=0, keepdims=True) for h in heads)

    zero = jnp.zeros((1, ATT_T), F32)
    rsum = step(qi, (zero,) * (n_rows * B_HEADS), True)
    lax.fori_loop(0, qi, lambda t, c: step(qi - 1 - t, c, False), rsum)
    for b in range(n_rows):
        for p in range(B_HEADS // 2):
            c0 = b * B_HEADS + 2 * p
            o_t = jnp.concatenate([acc_ref[c0][:HEAD_DIM], acc_ref[c0 + 1][HEAD_DIM:]], axis=0)
            o_ref[b, :, p * LANES:(p + 1) * LANES] = o_t.T.astype(o_ref.dtype)


def _attn_b(p_abc, batch, s):
    n = p_abc.shape[0]
    rows = ATT_ROWS if batch % ATT_ROWS == 0 else 1
    q_spec, k_spec, v_spec, o_spec = _attn_specs_rows(B_W, 3 * A_W // B_W, s, rows)
    p3 = p_abc.reshape(batch, s, ABC_W)
    out = pl.pallas_call(
        _attn_b_kernel,
        grid=(batch // rows, s // ATT_T),
        in_specs=[q_spec, k_spec, v_spec],
        out_specs=o_spec,
        out_shape=jax.ShapeDtypeStruct((batch, s, B_W), BF16),
        scratch_shapes=[pltpu.VMEM((rows * B_HEADS, LANES, ATT_T), F32),
                        pltpu.VMEM((rows, B_W, s), BF16)],
        compiler_params=_cparams(2),
        name="attn_stick",
    )(p3, p3, p3)
    return out.reshape(n, B_W)


def _attn_c_kernel(q_ref, k_ref, v_ref, cf_ref, fb_ref, o_ref, acc_ref, cumc_ref, cumr_ref, vaug_ref):
    qi = pl.program_id(1)
    n_rows, s_len = cf_ref.shape[0], cf_ref.shape[1]

    @pl.when(qi == 0)
    def _():
        r = lax.broadcasted_iota(jnp.int32, (LANES, LANES), 0)
        c = lax.broadcasted_iota(jnp.int32, (LANES, LANES), 1)
        lower = jnp.where(c <= r, 1.0, 0.0).astype(BF16)
        own = lax.broadcasted_iota(jnp.int32, (LANES, ATT_T), 0) < HEAD_DIM
        for b in range(n_rows):
            carry = jnp.zeros((1, LANES), F32)
            for blk in range(s_len // LANES):
                rows = slice(blk * LANES, (blk + 1) * LANES)
                log_f = _log_sigmoid(cf_ref[b, rows, :] + fb_ref[...])
                cum = _dot_exact_lhs(lower, log_f) + carry
                cumc_ref[b, rows, :] = cum * LOG2E
                cumr_ref[b, :, rows] = (cum * LOG2E).T
                carry = cum[LANES - 1:LANES, :]
            for p in range(C_HEADS // 2):
                for blk in range(s_len // ATT_T):
                    tile = v_ref[b, blk * ATT_T:(blk + 1) * ATT_T, p * LANES:(p + 1) * LANES]
                    tile_t = tile.astype(F32).T
                    dst = (slice(p * LANES, (p + 1) * LANES), slice(blk * ATT_T, (blk + 1) * ATT_T))
                    vaug_ref[(b, 0) + dst] = jnp.where(own, tile_t, 1.0).astype(BF16)
                    vaug_ref[(b, 1) + dst] = jnp.where(own, 1.0, tile_t).astype(BF16)

    chains = [(b, h) for b in range(n_rows) for h in range(C_HEADS)]
    qs = [q for b in range(n_rows) for q in _split_heads(q_ref[b])]
    acc_ref[...] = jnp.zeros(acc_ref.shape, F32)
    q_cols = _rows(qi, ATT_T)
    cum_q = [cumr_ref[b, h:h + 1, q_cols] for b, h in chains]
    n_full, odd = _window_split(qi)

    def causal(width):
        key = lax.broadcasted_iota(jnp.int32, (width, ATT_T), 0) - (width - ATT_T)
        return jnp.where(key <= lax.broadcasted_iota(jnp.int32, (width, ATT_T), 1), 0.0, NEG_INF)

    def window(rows, carry, mask):
        scores, vts = [], []
        for c, (b, h) in enumerate(chains):
            cols = slice((h // 2) * LANES, (h // 2 + 1) * LANES)
            s = _qk(k_ref[b, rows, cols], qs[c]) + cum_q[c] - cumc_ref[b, rows, h:h + 1]
            scores.append(s if mask is None else s + mask)
            vts.append(vaug_ref[b, h % 2, cols, rows])
        return tuple(_softmax_stages_t(scores, carry, acc_ref, vts)[0])

    init = (jnp.full((1, ATT_T), NEG_INF, F32),) * len(chains)
    carry = lax.fori_loop(0, n_full - odd, lambda j, c: window(_rows(j, ATT_WIN), c, None), init)
    carry = lax.fori_loop(n_full - odd, n_full,
                          lambda j, c: window(_rows(j, ATT_WIN), c, causal(ATT_WIN)), carry)
    lax.fori_loop(0, 1 - odd, lambda _, c: window(_rows(qi, ATT_T), c, causal(ATT_T)), carry)
    for b in range(n_rows):
        for p in range(C_HEADS // 2):
            c0 = b * C_HEADS + 2 * p
            a0, a1 = acc_ref[c0], acc_ref[c0 + 1]
            num = jnp.concatenate([a0[:HEAD_DIM], a1[HEAD_DIM:]], axis=0)
            den = jnp.concatenate([a0[HEAD_DIM:], a1[:HEAD_DIM]], axis=0)
            o_ref[b, :, p * LANES:(p + 1) * LANES] = (num / den).T.astype(o_ref.dtype)


def _attn_c(p_abc, cf, forget_bias, batch, s):
    n = p_abc.shape[0]
    rows = ATT_ROWS if batch % ATT_ROWS == 0 else 1
    q_spec, k_spec, v_spec, o_spec = _attn_specs_rows(C_W, 3 * (A_W + B_W) // C_W, s, rows)
    fb = jnp.pad(forget_bias.astype(F32), (0, LANES - C_HEADS)).reshape(1, LANES)
    p3 = p_abc.reshape(batch, s, ABC_W)
    out = pl.pallas_call(
        _attn_c_kernel,
        grid=(batch // rows, s // ATT_T),
        in_specs=[q_spec, k_spec, v_spec,
                  pl.BlockSpec((rows, s, LANES), lambda b, i: (b, 0, 0)),
                  pl.BlockSpec((1, LANES), lambda b, i: (0, 0))],
        out_specs=o_spec,
        out_shape=jax.ShapeDtypeStruct((batch, s, C_W), BF16),
        scratch_shapes=[pltpu.VMEM((rows * C_HEADS, LANES, ATT_T), F32),
                        pltpu.VMEM((rows, s, LANES), F32),
                        pltpu.VMEM((rows, LANES, s), F32),
                        pltpu.VMEM((rows, 2, C_W, s), BF16)],
        compiler_params=_cparams(2),
        name="attn_forget",
    )(p3, p3, p3, cf.reshape(batch, s, LANES), fb)
    return out.reshape(n, C_W)


def _attn_d_kernel(*refs):
    qkv_refs, (bias_ref, o_ref, qc_ref, kc_ref, vc_ref, oc_ref, lc_ref, og_ref, lse_ref) = (
        refs[:3 * D_GROUPS], refs[3 * D_GROUPS:])
    s_len = o_ref.shape[0]
    n_tiles = s_len // D_T
    first = _lane_half((D_T, LANES))
    kc_ref[0:D_T, :] = jnp.zeros((D_T, LANES), BF16)
    vc_ref[0:D_T, :] = jnp.zeros((D_T, LANES), BF16)

    for g, (_, dil) in enumerate(D_PAIRS):
        q_ref, k_ref, v_ref = qkv_refs[g], qkv_refs[D_GROUPS + g], qkv_refs[2 * D_GROUPS + g]
        class_len = s_len // dil
        tiles_per_class = class_len // D_T

        for r in range(dil):
            src = pl.ds(r, class_len, stride=dil) if dil > 1 else slice(None)
            dst = slice(r * class_len, (r + 1) * class_len)
            dst_kv = slice(D_T + r * class_len, D_T + (r + 1) * class_len)
            qc_ref[dst, :] = q_ref[src, :].astype(BF16)
            kc_ref[dst_kv, :] = k_ref[src, :].astype(BF16)
            vc_ref[dst_kv, :] = v_ref[src, :].astype(BF16)

        def tile_group(i, carry):
            tiles = [i * D_UNROLL + u for u in range(D_UNROLL)]
            rows = [pl.ds(pl.multiple_of(t * D_T, D_T), D_T) for t in tiles]
            if tiles_per_class > 1:
                keys = [pl.ds(pl.multiple_of(t * D_T, D_T), 2 * D_T) for t in tiles]
                variant = [jnp.where(t % tiles_per_class == 0, 1, 0) for t in tiles]
                bias = [[bias_ref[g, h, variant[u]] for h in range(2)] for u in range(D_UNROLL)]
            else:
                keys = [pl.ds(pl.multiple_of(t * D_T + D_T, D_T), D_T) for t in tiles]
                bias = [[bias_ref[g, h, 1][:, D_T:] for h in range(2)]] * D_UNROLL
            chains = [(u, h) for u in range(D_UNROLL) for h in range(2)]
            qh = [_split_heads(qc_ref[rows[u], :]) for u in range(D_UNROLL)]
            sc = [_qk(qh[u][h], kc_ref[keys[u], :]) + bias[u][h] for u, h in chains]
            m = [jnp.max(s, axis=-1, keepdims=True) for s in sc]
            e = [jnp.exp(s - mm) for s, mm in zip(sc, m)]
            l = [jnp.sum(x, axis=-1, keepdims=True) for x in e]
            o = [jnp.dot(x.astype(BF16), vc_ref[keys[u], :], preferred_element_type=F32)
                 for x, (u, h) in zip(e, chains)]
            o = [x / ll for x, ll in zip(o, l)]
            lse = [jnp.broadcast_to(mm + jnp.log(ll), (D_T, LANES)) for mm, ll in zip(m, l)]
            for u in range(D_UNROLL):
                oc_ref[rows[u], :] = jnp.where(first, o[2 * u], o[2 * u + 1])
                lc_ref[rows[u], :] = jnp.where(first, lse[2 * u], lse[2 * u + 1])
            return carry

        lax.fori_loop(0, n_tiles // D_UNROLL, tile_group, 0)

        for r in range(dil):
            dst = pl.ds(r, class_len, stride=dil) if dil > 1 else slice(None)
            src = slice(r * class_len, (r + 1) * class_len)
            og_ref[g, dst, :] = oc_ref[src, :]
            lse_ref[g, dst, :] = lc_ref[src, :]

    lse = [lse_ref[g] for g in range(D_GROUPS)]
    top = functools.reduce(jnp.maximum, lse)
    w = [jnp.exp(x - top) for x in lse]
    num = sum(w[g] * og_ref[g] for g in range(D_GROUPS))
    o_ref[...] = (num / sum(w)).astype(o_ref.dtype)


def _attn_d(p_d, bias_d, batch, s):
    n = p_d.shape[0]
    return pl.pallas_call(
        _attn_d_kernel,
        grid=(batch,),
        in_specs=([pl.BlockSpec((s, LANES), functools.partial(lambda b, c: (b, c), c=c))
                   for c in range(3 * D_GROUPS)]
                  + [pl.BlockSpec(bias_d.shape, lambda b: (0, 0, 0, 0, 0))]),
        out_specs=pl.BlockSpec((s, LANES), lambda b: (b, 0)),
        out_shape=jax.ShapeDtypeStruct((n, LANES), BF16),
        scratch_shapes=[pltpu.VMEM((s, LANES), BF16), pltpu.VMEM((s + D_T, LANES), BF16),
                        pltpu.VMEM((s + D_T, LANES), BF16),
                        pltpu.VMEM((s, LANES), F32), pltpu.VMEM((s, LANES), F32),
                        pltpu.VMEM((D_GROUPS, s, LANES), F32),
                        pltpu.VMEM((D_GROUPS, s, LANES), F32)],
        compiler_params=_cparams(1),
        name="attn_dilated",
    )(*([p_d] * (3 * D_GROUPS)), bias_d)


def _merge_kernel(x_ref, g_ref, wg_ref, gb_ref, oa_ref, ob_ref, oc_ref, od_ref,
                  wa_ref, wb_ref, wc_ref, wd_ref, wout_ref, o_ref):
    x = x_ref[...]
    h = _rms(x, g_ref[...]).astype(BF16)
    merged = jnp.zeros(x.shape, F32)
    branches = ((oa_ref, wa_ref), (ob_ref, wb_ref), (oc_ref, wc_ref), (od_ref, wd_ref))
    for i, (br_ref, w_ref) in enumerate(branches):
        cols = slice(i * D_MODEL, (i + 1) * D_MODEL)
        logits = jnp.dot(h, wg_ref[:, cols], preferred_element_type=F32) + gb_ref[:, cols]
        branch = jnp.dot(br_ref[...], w_ref[...], preferred_element_type=F32)
        merged = merged + jax.nn.sigmoid(logits) * branch
    o_ref[...] = x + jnp.dot(merged.astype(BF16), wout_ref[...], preferred_element_type=F32)


def _merge(x2, gain, w_all, layer, gate_bias, outs, w_branch, w_out):
    n = x2.shape[0]
    gate_w = N_BRANCH * D_MODEL
    widths = (A_W, B_W, C_W, 2 * HEAD_DIM)
    offs = np.concatenate([[0], np.cumsum(widths)])
    w_br = [w_branch[offs[i]:offs[i + 1]].astype(BF16) for i in range(N_BRANCH)]
    const = lambda i: (0, 0)
    row = lambda i: (i, 0)
    return pl.pallas_call(
        _merge_kernel,
        grid=(n // TOK_TILE,),
        in_specs=([pl.BlockSpec((TOK_TILE, D_MODEL), row),
                   pl.BlockSpec((1, D_MODEL), const),
                   pl.BlockSpec((None, D_MODEL, gate_w), lambda i: (layer, 0, 0),
                                pipeline_mode=pl.Buffered(1)),
                   pl.BlockSpec((1, gate_w), const)]
                  + [pl.BlockSpec((TOK_TILE, w), row) for w in widths]
                  + [pl.BlockSpec((w, D_MODEL), const) for w in widths]
                  + [pl.BlockSpec((D_MODEL, D_MODEL), const)]),
        out_specs=pl.BlockSpec((TOK_TILE, D_MODEL), row),
        out_shape=jax.ShapeDtypeStruct((n, D_MODEL), F32),
        compiler_params=_cparams(1),
        name="merge",
    )(x2, gain.reshape(1, D_MODEL), w_all, gate_bias.reshape(1, -1).astype(F32),
      *outs, *w_br, w_out.astype(BF16))


def kernel(x, rel_table, ffn1_norm, ffn1_w_in, ffn1_w_out, mix_norm, w_in, gate_bias, forget_bias,
           a_q_norm, a_k_norm, a_lambda, a_subln, c_q_norm, c_k_norm, d_q_norm, d_k_norm,
           w_branch, w_out, ffn2_norm, ffn2_w_in, ffn2_w_out):
    batch, s, _ = x.shape
    depth = w_in.shape[0]
    assert s % (D_T * D_PAIRS[-1][1]) == 0 and (batch * s) % TOK_TILE == 0
    bias_a, bias_d = _bias_tiles(rel_table)
    w_all = _prep_w_in(w_in)
    x2 = x.reshape(batch * s, D_MODEL)
    for l in range(depth):
        x2 = _ffn(x2, ffn1_norm[l], ffn1_w_in[l], ffn1_w_out[l])
        p_abc, p_d, cf = _proj(x2, mix_norm[l], w_all, l, a_q_norm[l], a_k_norm[l],
                               c_q_norm[l], c_k_norm[l], d_q_norm[l], d_k_norm[l])
        lam_init = 0.8 - 0.6 * math.exp(-0.3 * l)
        oa = _attn_a(p_abc, bias_a, a_lambda[l], a_subln[l], lam_init, batch, s)
        ob = _attn_b(p_abc, batch, s)
        oc = _attn_c(p_abc, cf, forget_bias[l], batch, s)
        od = _attn_d(p_d, bias_d, batch, s)
        x2 = _merge(x2, mix_norm[l], w_all, l, gate_bias[l], (oa, ob, oc, od),
                    w_branch[l], w_out[l])
        x2 = _ffn(x2, ffn2_norm[l], ffn2_w_in[l], ffn2_w_out[l])
    return x2.reshape(batch, s, D_MODEL)
```

```python
import functools
import math

import numpy as np
import jax
import jax.numpy as jnp
from jax import lax
from jax.experimental import pallas as pl
from jax.experimental.pallas import tpu as pltpu

F32 = jnp.float32
BF16 = jnp.bfloat16

D_MODEL = 1024
HEAD_DIM = 64
SCALE = HEAD_DIM ** -0.5
A_HEADS = 4
B_HEADS = 6
C_HEADS = 6
D_PAIRS = ((128, 1), (512, 4), (2048, 16))
D_GROUPS = len(D_PAIRS)
N_BRANCH = 4
FFN_HIDDEN = 2816
REL_BUCKETS = 32
REL_MAX_DIST = 128
RMS_EPS = 1e-6
NEG_INF = -1e30

LANES = 128
A_W = A_HEADS * 2 * HEAD_DIM
B_W = B_HEADS * HEAD_DIM
C_W = C_HEADS * HEAD_DIM
D_W = D_GROUPS * 2 * HEAD_DIM
ABC_W = 3 * (A_W + B_W + C_W)

W_SHIFT = (ABC_W + C_HEADS) % LANES
W_ALIGNED_BLOCKS = ABC_W // LANES + 1
W_ABC_OFF = N_BRANCH * D_MODEL
W_CF_OFF = W_ABC_OFF + ABC_W
W_D_OFF = W_ABC_OFF + W_ALIGNED_BLOCKS * LANES
W_D_END = W_D_OFF + 3 * D_W
GROUP_W = 256

TOK_TILE = 512
FFN_CHUNK = 256
ATT_T = 256
ATT_WIN = 2 * ATT_T
LOG2E = math.log2(math.e)
ATT_ROWS = 2
D_UNROLL = 4
D_T = 128
VMEM_LIMIT = 56 * 1024 * 1024


def _cparams(n_axes):
    return pltpu.CompilerParams(dimension_semantics=("arbitrary",) * n_axes,
                                vmem_limit_bytes=VMEM_LIMIT)


def _rms(x, g):
    ms = jnp.mean(x * x, axis=-1, keepdims=True)
    return x * lax.rsqrt(ms + RMS_EPS) * g


def _split3(a):
    hi = a.astype(BF16)
    r1 = a - hi.astype(F32)
    mid = r1.astype(BF16)
    lo = (r1 - mid.astype(F32)).astype(BF16)
    return hi, mid, lo


def _dot_exact_lhs(a_bf16, b):
    hi, mid, lo = _split3(b)
    d = functools.partial(jnp.dot, preferred_element_type=F32)
    return d(a_bf16, hi) + d(a_bf16, mid) + d(a_bf16, lo)


def _qk(q, k):
    return lax.dot_general(q, k, (((1,), (1,)), ((), ())), preferred_element_type=F32)


def _log_sigmoid(x):
    return jnp.minimum(x, 0.0) - jnp.log(1.0 + jnp.exp(-jnp.abs(x)))


def _lane_half(shape):
    return lax.broadcasted_iota(jnp.int32, shape, len(shape) - 1) < HEAD_DIM


def _ffn_kernel(x_ref, g_ref, wi_ref, wo_ref, o_ref):
    x = x_ref[...]
    h = _rms(x, g_ref[...]).astype(BF16)
    acc = jnp.zeros(x.shape, F32)
    for c in range(FFN_HIDDEN // FFN_CHUNK):
        lo = c * FFN_CHUNK
        gate = jnp.dot(h, wi_ref[:, lo:lo + FFN_CHUNK], preferred_element_type=F32)
        up = jnp.dot(h, wi_ref[:, FFN_HIDDEN + lo:FFN_HIDDEN + lo + FFN_CHUNK],
                     preferred_element_type=F32)
        a = (gate * jax.nn.sigmoid(gate) * up).astype(BF16)
        acc = acc + jnp.dot(a, wo_ref[c * FFN_CHUNK:(c + 1) * FFN_CHUNK, :],
                            preferred_element_type=F32)
    o_ref[...] = x + 0.5 * acc


def _ffn(x2, gain, w_i, w_o):
    n = x2.shape[0]
    const = lambda i: (0, 0)
    return pl.pallas_call(
        _ffn_kernel,
        grid=(n // TOK_TILE,),
        in_specs=[
            pl.BlockSpec((TOK_TILE, D_MODEL), lambda i: (i, 0)),
            pl.BlockSpec((1, D_MODEL), const),
            pl.BlockSpec((D_MODEL, 2 * FFN_HIDDEN), const, pipeline_mode=pl.Buffered(1)),
            pl.BlockSpec((FFN_HIDDEN, D_MODEL), const, pipeline_mode=pl.Buffered(1)),
        ],
        out_specs=pl.BlockSpec((TOK_TILE, D_MODEL), lambda i: (i, 0)),
        out_shape=jax.ShapeDtypeStruct((n, D_MODEL), F32),
        compiler_params=_cparams(1),
        name="ffn",
    )(x2, gain.reshape(1, D_MODEL), w_i.astype(BF16), w_o.astype(BF16))


_ABC_SECTIONS = ((A_W, True), (A_W, True), (A_W, False),
                 (B_W, False), (B_W, False), (B_W, False),
                 (C_W, True), (C_W, True), (C_W, False))
_D_SECTIONS = ((D_W, True), (D_W, True), (D_W, False))


def _proj_sections(y, gain_ref, gmat, out_ref, sections):
    off = 0
    for width, normed in sections:
        lo = off
        while lo < off + width:
            blk = min(GROUP_W, off + width - lo)
            yb = y[:, lo:lo + blk]
            if normed:
                ms = jnp.dot((yb * yb).astype(BF16), gmat[:blk, :blk], preferred_element_type=F32)
                yb = yb * lax.rsqrt(ms + RMS_EPS)
            out_ref[:, lo:lo + blk] = (yb * gain_ref[:, lo:lo + blk]).astype(out_ref.dtype)
            lo += blk
        off += width


def _proj_kernel(x_ref, g_ref, w_ref, gabc_ref, gd_ref, gmat_ref, pabc_ref, pd_ref, cf_ref):
    h = _rms(x_ref[...], g_ref[...]).astype(BF16)
    gmat = gmat_ref[...]
    y_abc = jnp.dot(h, w_ref[:, W_ABC_OFF:W_CF_OFF], preferred_element_type=F32)
    _proj_sections(y_abc, gabc_ref, gmat, pabc_ref, _ABC_SECTIONS)
    y_d = jnp.dot(h, w_ref[:, W_CF_OFF:W_D_END], preferred_element_type=F32)
    cf_ref[...] = y_d[:, :LANES]
    _proj_sections(y_d[:, LANES:], gd_ref, gmat, pd_ref, _D_SECTIONS)


def _head_gain(gain, width, scale):
    return jnp.tile(gain.astype(F32), width // HEAD_DIM) * scale


def _win_kernel(a_ref, b_ref, o_ref):
    c = pl.program_id(1)

    @pl.when(c < W_ALIGNED_BLOCKS)
    def _():
        o_ref[...] = b_ref[...].astype(BF16)

    @pl.when(c >= W_ALIGNED_BLOCKS)
    def _():
        both = jnp.concatenate([a_ref[:, W_SHIFT:], b_ref[:, :W_SHIFT]], axis=1)
        o_ref[...] = both.astype(BF16)


def _prep_w_in(w_in):
    depth = w_in.shape[0]
    n_blocks = pl.cdiv(w_in.shape[2], LANES)
    gate_block0 = (W_D_END - W_ABC_OFF) // LANES
    return pl.pallas_call(
        _win_kernel,
        grid=(depth, n_blocks),
        in_specs=[pl.BlockSpec((None, D_MODEL, LANES), lambda l, c: (l, 0, jnp.maximum(c - 1, 0))),
                  pl.BlockSpec((None, D_MODEL, LANES), lambda l, c: (l, 0, c))],
        out_specs=pl.BlockSpec(
            (None, D_MODEL, LANES),
            lambda l, c: (l, 0, jnp.where(c < gate_block0, c + W_ABC_OFF // LANES, c - gate_block0))),
        out_shape=jax.ShapeDtypeStruct((depth, D_MODEL, W_D_END), BF16),
        compiler_params=_cparams(2),
        name="prep_w_in",
    )(w_in, w_in)


def _proj(x2, gain, w_all, layer, a_q, a_k, c_q, c_k, d_q, d_k):
    n = x2.shape[0]
    ones = lambda w, s: jnp.full((w,), s, F32)
    g_abc = jnp.concatenate([
        _head_gain(a_q, A_W, SCALE * LOG2E), _head_gain(a_k, A_W, 1.0), ones(A_W, 1.0),
        ones(B_W, SCALE * LOG2E), ones(B_W, 1.0), ones(B_W, 1.0),
        _head_gain(c_q, C_W, SCALE * LOG2E), _head_gain(c_k, C_W, 1.0),
        ones(C_W, 1.0)]).reshape(1, ABC_W)
    g_d = jnp.concatenate([_head_gain(d_q, D_W, SCALE), _head_gain(d_k, D_W, 1.0),
                           ones(D_W, 1.0)]).reshape(1, 3 * D_W)
    lane = np.arange(GROUP_W)
    gmat = jnp.asarray((lane[:, None] // HEAD_DIM == lane[None, :] // HEAD_DIM) / HEAD_DIM, BF16)
    const = lambda i: (0, 0)
    row = lambda i: (i, 0)
    return pl.pallas_call(
        _proj_kernel,
        grid=(n // TOK_TILE,),
        in_specs=[
            pl.BlockSpec((TOK_TILE, D_MODEL), row),
            pl.BlockSpec((1, D_MODEL), const),
            pl.BlockSpec((None, D_MODEL, W_D_END), lambda i: (layer, 0, 0),
                         pipeline_mode=pl.Buffered(1)),
            pl.BlockSpec((1, ABC_W), const),
            pl.BlockSpec((1, 3 * D_W), const),
            pl.BlockSpec((GROUP_W, GROUP_W), const),
        ],
        out_specs=[
            pl.BlockSpec((TOK_TILE, ABC_W), row),
            pl.BlockSpec((TOK_TILE, 3 * D_W), row),
            pl.BlockSpec((TOK_TILE, LANES), row),
        ],
        out_shape=[
            jax.ShapeDtypeStruct((n, ABC_W), BF16),
            jax.ShapeDtypeStruct((n, 3 * D_W), F32),
            jax.ShapeDtypeStruct((n, LANES), F32),
        ],
        compiler_params=_cparams(1),
        name="proj",
    )(x2, gain.reshape(1, D_MODEL), w_all, g_abc, g_d, gmat)


_MASKED_BUCKET = REL_BUCKETS


def _rel_bucket_np(n):
    n = np.asarray(n, np.int64)
    max_exact = REL_BUCKETS // 2
    nf = np.maximum(n, 1).astype(np.float32)
    large = max_exact + (np.log(nf / np.float32(max_exact)) / np.float32(math.log(REL_MAX_DIST / max_exact))
                         * np.float32(REL_BUCKETS - max_exact)).astype(np.int32)
    large = np.minimum(large, REL_BUCKETS - 1)
    return np.where(n < max_exact, n, large).astype(np.int32)


def _bucket_maps():
    t = ATT_T
    x, y = np.arange(t)[:, None], np.arange(t)[None, :]
    diag = np.where(x >= y, _rel_bucket_np(np.maximum(x - y, 0)), _MASKED_BUCKET)
    prev = _rel_bucket_np(t + x - y)
    idx_a = np.stack([prev.T, diag.T]).astype(np.int32)
    x, y = np.arange(D_T)[:, None], np.arange(D_T)[None, :]
    idx_d = np.empty((D_GROUPS, 2, D_T, 2 * D_T), np.int32)
    for g, (_, dil) in enumerate(D_PAIRS):
        prev = np.where(y >= x, _rel_bucket_np(dil * (D_T + x - y)), _MASKED_BUCKET)
        diag = np.where(y <= x, _rel_bucket_np(dil * np.maximum(x - y, 0)), _MASKED_BUCKET)
        idx_d[g, 0] = np.concatenate([prev, diag], axis=1)
        idx_d[g, 1] = np.concatenate([np.full_like(prev, _MASKED_BUCKET), diag], axis=1)
    return idx_a, idx_d


def _bias_kernel(tbl_ref, idxa_ref, idxd_ref, ba_ref, bd_ref):
    for h in range(A_HEADS):
        for kind in range(2):
            ba_ref[h, kind] = jnp.where(idxa_ref[kind] == _MASKED_BUCKET, NEG_INF, 0.0)
        ba_ref[h, 2] = jnp.zeros((ATT_T, ATT_T), F32)
        ba_ref[h, 3] = jnp.full((ATT_T, ATT_T), NEG_INF, F32)
    for g in range(D_GROUPS):
        for h in range(2):
            for var in range(2):
                bd_ref[g, h, var] = jnp.where(idxd_ref[g, var] == _MASKED_BUCKET, NEG_INF, 0.0)

    def body(b, carry):
        for h in range(A_HEADS):
            val = (tbl_ref[b, h] - tbl_ref[REL_BUCKETS - 1, h]) * LOG2E
            for kind in range(2):
                ba_ref[h, kind] = jnp.where(idxa_ref[kind] == b, val, ba_ref[h, kind])
        for g in range(D_GROUPS):
            for h in range(2):
                val = tbl_ref[b, A_HEADS + 2 * g + h]
                for var in range(2):
                    bd_ref[g, h, var] = jnp.where(idxd_ref[g, var] == b, val, bd_ref[g, h, var])
        return carry

    lax.fori_loop(0, REL_BUCKETS, body, 0)


def _bias_tiles(rel_table):
    idx_a, idx_d = _bucket_maps()
    return pl.pallas_call(
        _bias_kernel,
        in_specs=[pl.BlockSpec(memory_space=pltpu.SMEM),
                  pl.BlockSpec(memory_space=pltpu.VMEM),
                  pl.BlockSpec(memory_space=pltpu.VMEM)],
        out_specs=[pl.BlockSpec(memory_space=pltpu.VMEM),
                   pl.BlockSpec(memory_space=pltpu.VMEM)],
        out_shape=[jax.ShapeDtypeStruct((A_HEADS, 4, ATT_T, ATT_T), F32),
                   jax.ShapeDtypeStruct((D_GROUPS, 2, 2, D_T, 2 * D_T), F32)],
        compiler_params=pltpu.CompilerParams(vmem_limit_bytes=VMEM_LIMIT),
        name="rel_bias",
    )(rel_table.astype(F32), jnp.asarray(idx_a), jnp.asarray(idx_d))


def _rows(idx, size):
    return pl.ds(pl.multiple_of(idx * size, size), size)


def _split_heads(q):
    first = _lane_half((q.shape[0], LANES))
    zero = jnp.zeros((q.shape[0], LANES), q.dtype)
    out = []
    for p in range(q.shape[1] // LANES):
        qp = q[:, p * LANES:(p + 1) * LANES]
        out += [jnp.where(first, qp, zero), jnp.where(first, zero, qp)]
    return out


def _transpose_into(vt_ref, v_ref):
    for p in range(v_ref.shape[1] // LANES):
        for blk in range(v_ref.shape[0] // ATT_T):
            tile = v_ref[blk * ATT_T:(blk + 1) * ATT_T, p * LANES:(p + 1) * LANES]
            vt_ref[p * LANES:(p + 1) * LANES, blk * ATT_T:(blk + 1) * ATT_T] = (
                tile.astype(F32).T.astype(vt_ref.dtype))


def _attn_specs_rows(width, col0, s, rows):
    q_spec = pl.BlockSpec((rows, ATT_T, width), lambda b, i: (b, i, col0))
    k_spec = pl.BlockSpec((rows, s, width), lambda b, i: (b, 0, col0 + 1))
    v_spec = pl.BlockSpec((rows, s, width), lambda b, i: (b, 0, col0 + 2))
    o_spec = pl.BlockSpec((rows, ATT_T, width), lambda b, i: (b, i, 0))
    return q_spec, k_spec, v_spec, o_spec


def _window_split(qi):
    return lax.shift_right_logical(qi + 1, 1), jnp.bitwise_and(qi, 1)


def _softmax_stages_t(scores, ms, acc_ref, vts, ls=None):
    n = len(scores)
    m_new = [jnp.maximum(ms[c], jnp.max(scores[c], axis=0, keepdims=True)) for c in range(n)]
    alpha = [jnp.exp2(ms[c] - m_new[c]) for c in range(n)]
    p_bf16, l_new = [], []
    for c in range(n):
        p = jnp.exp2(scores[c] - m_new[c])
        p_bf16.append(p.astype(BF16))
        if ls is not None:
            l_new.append(alpha[c] * ls[c] + jnp.sum(p, axis=0, keepdims=True))
    pv = [jnp.dot(vts[c], p_bf16[c], preferred_element_type=F32) for c in range(n)]
    for c in range(n):
        acc_ref[c] = alpha[c] * acc_ref[c] + pv[c]
    return m_new, (l_new if ls is not None else None)


def _attn_a_kernel(q_ref, k_ref, v_ref, bias_ref, lam_ref, subln_ref, o_ref, acc_ref, vt_ref, *,
                   lam_init):
    qi = pl.program_id(1)
    n_rows = q_ref.shape[0]

    @pl.when(qi == 0)
    def _():
        for b in range(n_rows):
            _transpose_into(vt_ref.at[b], v_ref.at[b])

    qs = [q for b in range(n_rows) for q in _split_heads(q_ref[b])]
    n_chain = len(qs)
    acc_ref[...] = jnp.zeros(acc_ref.shape, F32)
    stat = lambda val: jnp.full((1, ATT_T), val, F32)
    init = (stat(NEG_INF), stat(0.0)) * n_chain
    n_full, odd = _window_split(qi)

    def window(rows, carry, kinds):
        scores, vts = [], []
        for b in range(n_rows):
            for h in range(A_HEADS):
                cols = slice(h * LANES, (h + 1) * LANES)
                k, vt = k_ref[b, rows, cols], vt_ref[b, cols, rows]
                if kinds is not None:
                    bias = jnp.concatenate([bias_ref[h, kind] for kind in kinds], axis=0)
                for half in range(2):
                    s = _qk(k, qs[(b * A_HEADS + h) * 2 + half])
                    scores.append(s if kinds is None else s + bias)
                    vts.append(vt)
        ms, ls = _softmax_stages_t(scores, carry[0::2], acc_ref, vts, ls=carry[1::2])
        return tuple(x for pair in zip(ms, ls) for x in pair)

    n_plain = jnp.maximum(n_full - 1, 0)
    carry = lax.fori_loop(0, n_plain, lambda j, c: window(_rows(j, ATT_WIN), c, None), init)
    near = (jnp.where(odd == 1, 0, 2), jnp.where(odd == 1, 1, 0))
    carry = lax.fori_loop(n_plain, n_full, lambda j, c: window(_rows(j, ATT_WIN), c, near), carry)
    carry = lax.fori_loop(0, 1 - odd, lambda _, c: window(_rows(qi, ATT_T), c, (1,)), carry)

    lv = lam_ref[...]
    lam = (jnp.exp(jnp.sum(lv[0:1] * lv[1:2], axis=-1, keepdims=True))
           - jnp.exp(jnp.sum(lv[2:3] * lv[3:4], axis=-1, keepdims=True)) + lam_init)
    for b in range(n_rows):
        for h in range(A_HEADS):
            c0 = (b * A_HEADS + h) * 2
            l0, l1 = carry[2 * c0 + 1], carry[2 * c0 + 3]
            o = acc_ref[c0] / l0 - lam * (acc_ref[c0 + 1] / l1)
            ms = jnp.mean(o * o, axis=0, keepdims=True)
            o = o * lax.rsqrt(ms + RMS_EPS) * subln_ref[...] * (1.0 - lam_init)
            o_ref[b, :, h * LANES:(h + 1) * LANES] = o.T.astype(o_ref.dtype)


def _attn_a(p_abc, bias_a, lam_vecs, subln, lam_init, batch, s):
    n = p_abc.shape[0]
    rows = ATT_ROWS if batch % ATT_ROWS == 0 else 1
    q_spec, k_spec, v_spec, o_spec = _attn_specs_rows(A_W, 0, s, rows)
    p3 = p_abc.reshape(batch, s, ABC_W)
    out = pl.pallas_call(
        functools.partial(_attn_a_kernel, lam_init=lam_init),
        grid=(batch // rows, s // ATT_T),
        in_specs=[q_spec, k_spec, v_spec,
                  pl.BlockSpec(bias_a.shape, lambda b, i: (0, 0, 0, 0)),
                  pl.BlockSpec((4, HEAD_DIM), lambda b, i: (0, 0)),
                  pl.BlockSpec((2 * HEAD_DIM, ATT_T), lambda b, i: (0, 0))],
        out_specs=o_spec,
        out_shape=jax.ShapeDtypeStruct((batch, s, A_W), BF16),
        scratch_shapes=[pltpu.VMEM((rows * 2 * A_HEADS, LANES, ATT_T), F32),
                        pltpu.VMEM((rows, A_W, s), BF16)],
        compiler_params=_cparams(2),
        name="attn_diff",
    )(p3, p3, p3, bias_a, lam_vecs.astype(F32),
      jnp.broadcast_to(subln.astype(F32)[:, None], (2 * HEAD_DIM, ATT_T)))
    return out.reshape(n, A_W)


def _attn_b_kernel(q_ref, k_ref, v_ref, o_ref, acc_ref, vt_ref):
    qi = pl.program_id(1)
    n_rows = q_ref.shape[0]

    @pl.when(qi == 0)
    def _():
        for b in range(n_rows):
            _transpose_into(vt_ref.at[b], v_ref.at[b])

    qs = [q for b in range(n_rows) for q in _split_heads(q_ref[b])]
    acc_ref[...] = jnp.zeros(acc_ref.shape, F32)
    key = lax.broadcasted_iota(jnp.int32, (ATT_T, ATT_T), 0)
    qry = lax.broadcasted_iota(jnp.int32, (ATT_T, ATT_T), 1)
    later = jnp.where(qry > key, 1.0, 0.0).astype(BF16)
    strict = key < qry

    def step(kj, rsum, diag):
        rows = _rows(kj, ATT_T)
        heads = range(n_rows * B_HEADS)
        row_of = [h // B_HEADS for h in heads]
        cols = [slice(((h % B_HEADS) // 2) * LANES, ((h % B_HEADS) // 2 + 1) * LANES) for h in heads]
        z = [_qk(k_ref[row_of[h], rows, cols[h]], qs[h]) for h in heads]
        ls = [jnp.minimum(z[h], 0.0) - jnp.log2(1.0 + jnp.exp2(-jnp.abs(z[h]))) for h in heads]
        u = [ls[h] - z[h] for h in heads]
        if diag:
            u = [jnp.where(strict, u[h], 0.0) for h in heads]
        a = [jnp.exp2(ls[h] + jnp.dot(later, u[h].astype(BF16), preferred_element_type=F32)
                      + rsum[h]) for h in heads]
        if diag:
            a = [jnp.where(strict, a[h], 0.0) for h in heads]
        pv = [jnp.dot(vt_ref[row_of[h], cols[h], rows], a[h].astype(BF16),
                      preferred_element_type=F32) for h in heads]
        for h in heads:
            acc_ref[h] = acc_ref[h] + pv[h]
        return tuple(rsum[h] + jnp.sum(u[h], axis=0, keepdims=True) for h in heads)

    zero = jnp.zeros((1, ATT_T), F32)
    rsum = step(qi, (zero,) * (n_rows * B_HEADS), True)
    lax.fori_loop(0, qi, lambda t, c: step(qi - 1 - t, c, False), rsum)
    for b in range(n_rows):
        for p in range(B_HEADS // 2):
            c0 = b * B_HEADS + 2 * p
            o_t = jnp.concatenate([acc_ref[c0][:HEAD_DIM], acc_ref[c0 + 1][HEAD_DIM:]], axis=0)
            o_ref[b, :, p * LANES:(p + 1) * LANES] = o_t.T.astype(o_ref.dtype)


def _attn_b(p_abc, batch, s):
    n = p_abc.shape[0]
    rows = ATT_ROWS if batch % ATT_ROWS == 0 else 1
    q_spec, k_spec, v_spec, o_spec = _attn_specs_rows(B_W, 3 * A_W // B_W, s, rows)
    p3 = p_abc.reshape(batch, s, ABC_W)
    out = pl.pallas_call(
        _attn_b_kernel,
        grid=(batch // rows, s // ATT_T),
        in_specs=[q_spec, k_spec, v_spec],
        out_specs=o_spec,
        out_shape=jax.ShapeDtypeStruct((batch, s, B_W), BF16),
        scratch_shapes=[pltpu.VMEM((rows * B_HEADS, LANES, ATT_T), F32),
                        pltpu.VMEM((rows, B_W, s), BF16)],
        compiler_params=_cparams(2),
        name="attn_stick",
    )(p3, p3, p3)
    return out.reshape(n, B_W)


def _attn_c_kernel(q_ref, k_ref, v_ref, cf_ref, fb_ref, o_ref, acc_ref, cumc_ref, cumr_ref, vaug_ref):
    qi = pl.program_id(1)
    n_rows, s_len = cf_ref.shape[0], cf_ref.shape[1]

    @pl.when(qi == 0)
    def _():
        r = lax.broadcasted_iota(jnp.int32, (LANES, LANES), 0)
        c = lax.broadcasted_iota(jnp.int32, (LANES, LANES), 1)
        lower = jnp.where(c <= r, 1.0, 0.0).astype(BF16)
        own = lax.broadcasted_iota(jnp.int32, (LANES, ATT_T), 0) < HEAD_DIM
        for b in range(n_rows):
            carry = jnp.zeros((1, LANES), F32)
            for blk in range(s_len // LANES):
                rows = slice(blk * LANES, (blk + 1) * LANES)
                log_f = _log_sigmoid(cf_ref[b, rows, :] + fb_ref[...])
                cum = _dot_exact_lhs(lower, log_f) + carry
                cumc_ref[b, rows, :] = cum * LOG2E
                cumr_ref[b, :, rows] = (cum * LOG2E).T
                carry = cum[LANES - 1:LANES, :]
            for p in range(C_HEADS // 2):
                for blk in range(s_len // ATT_T):
                    tile = v_ref[b, blk * ATT_T:(blk + 1) * ATT_T, p * LANES:(p + 1) * LANES]
                    tile_t = tile.astype(F32).T
                    dst = (slice(p * LANES, (p + 1) * LANES), slice(blk * ATT_T, (blk + 1) * ATT_T))
                    vaug_ref[(b, 0) + dst] = jnp.where(own, tile_t, 1.0).astype(BF16)
                    vaug_ref[(b, 1) + dst] = jnp.where(own, 1.0, tile_t).astype(BF16)

    chains = [(b, h) for b in range(n_rows) for h in range(C_HEADS)]
    qs = [q for b in range(n_rows) for q in _split_heads(q_ref[b])]
    acc_ref[...] = jnp.zeros(acc_ref.shape, F32)
    q_cols = _rows(qi, ATT_T)
    cum_q = [cumr_ref[b, h:h + 1, q_cols] for b, h in chains]
    n_full, odd = _window_split(qi)

    def causal(width):
        key = lax.broadcasted_iota(jnp.int32, (width, ATT_T), 0) - (width - ATT_T)
        return jnp.where(key <= lax.broadcasted_iota(jnp.int32, (width, ATT_T), 1), 0.0, NEG_INF)

    def window(rows, carry, mask):
        scores, vts = [], []
        for c, (b, h) in enumerate(chains):
            cols = slice((h // 2) * LANES, (h // 2 + 1) * LANES)
            s = _qk(k_ref[b, rows, cols], qs[c]) + cum_q[c] - cumc_ref[b, rows, h:h + 1]
            scores.append(s if mask is None else s + mask)
            vts.append(vaug_ref[b, h % 2, cols, rows])
        return tuple(_softmax_stages_t(scores, carry, acc_ref, vts)[0])

    init = (jnp.full((1, ATT_T), NEG_INF, F32),) * len(chains)
    carry = lax.fori_loop(0, n_full - odd, lambda j, c: window(_rows(j, ATT_WIN), c, None), init)
    carry = lax.fori_loop(n_full - odd, n_full,
                          lambda j, c: window(_rows(j, ATT_WIN), c, causal(ATT_WIN)), carry)
    lax.fori_loop(0, 1 - odd, lambda _, c: window(_rows(qi, ATT_T), c, causal(ATT_T)), carry)
    for b in range(n_rows):
        for p in range(C_HEADS // 2):
            c0 = b * C_HEADS + 2 * p
            a0, a1 = acc_ref[c0], acc_ref[c0 + 1]
            num = jnp.concatenate([a0[:HEAD_DIM], a1[HEAD_DIM:]], axis=0)
            den = jnp.concatenate([a0[HEAD_DIM:], a1[:HEAD_DIM]], axis=0)
            o_ref[b, :, p * LANES:(p + 1) * LANES] = (num / den).T.astype(o_ref.dtype)


def _attn_c(p_abc, cf, forget_bias, batch, s):
    n = p_abc.shape[0]
    rows = ATT_ROWS if batch % ATT_ROWS == 0 else 1
    q_spec, k_spec, v_spec, o_spec = _attn_specs_rows(C_W, 3 * (A_W + B_W) // C_W, s, rows)
    fb = jnp.pad(forget_bias.astype(F32), (0, LANES - C_HEADS)).reshape(1, LANES)
    p3 = p_abc.reshape(batch, s, ABC_W)
    out = pl.pallas_call(
        _attn_c_kernel,
        grid=(batch // rows, s // ATT_T),
        in_specs=[q_spec, k_spec, v_spec,
                  pl.BlockSpec((rows, s, LANES), lambda b, i: (b, 0, 0)),
                  pl.BlockSpec((1, LANES), lambda b, i: (0, 0))],
        out_specs=o_spec,
        out_shape=jax.ShapeDtypeStruct((batch, s, C_W), BF16),
        scratch_shapes=[pltpu.VMEM((rows * C_HEADS, LANES, ATT_T), F32),
                        pltpu.VMEM((rows, s, LANES), F32),
                        pltpu.VMEM((rows, LANES, s), F32),
                        pltpu.VMEM((rows, 2, C_W, s), BF16)],
        compiler_params=_cparams(2),
        name="attn_forget",
    )(p3, p3, p3, cf.reshape(batch, s, LANES), fb)
    return out.reshape(n, C_W)


def _attn_d_kernel(*refs):
    qkv_refs, (bias_ref, o_ref, qc_ref, kc_ref, vc_ref, oc_ref, lc_ref, og_ref, lse_ref) = (
        refs[:3 * D_GROUPS], refs[3 * D_GROUPS:])
    s_len = o_ref.shape[0]
    n_tiles = s_len // D_T
    first = _lane_half((D_T, LANES))
    kc_ref[0:D_T, :] = jnp.zeros((D_T, LANES), BF16)
    vc_ref[0:D_T, :] = jnp.zeros((D_T, LANES), BF16)

    for g, (_, dil) in enumerate(D_PAIRS):
        q_ref, k_ref, v_ref = qkv_refs[g], qkv_refs[D_GROUPS + g], qkv_refs[2 * D_GROUPS + g]
        class_len = s_len // dil
        tiles_per_class = class_len // D_T

        for r in range(dil):
            src = pl.ds(r, class_len, stride=dil) if dil > 1 else slice(None)
            dst = slice(r * class_len, (r + 1) * class_len)
            dst_kv = slice(D_T + r * class_len, D_T + (r + 1) * class_len)
            qc_ref[dst, :] = q_ref[src, :].astype(BF16)
            kc_ref[dst_kv, :] = k_ref[src, :].astype(BF16)
            vc_ref[dst_kv, :] = v_ref[src, :].astype(BF16)

        def tile_group(i, carry):
            tiles = [i * D_UNROLL + u for u in range(D_UNROLL)]
            rows = [pl.ds(pl.multiple_of(t * D_T, D_T), D_T) for t in tiles]
            if tiles_per_class > 1:
                keys = [pl.ds(pl.multiple_of(t * D_T, D_T), 2 * D_T) for t in tiles]
                variant = [jnp.where(t % tiles_per_class == 0, 1, 0) for t in tiles]
                bias = [[bias_ref[g, h, variant[u]] for h in range(2)] for u in range(D_UNROLL)]
            else:
                keys = [pl.ds(pl.multiple_of(t * D_T + D_T, D_T), D_T) for t in tiles]
                bias = [[bias_ref[g, h, 1][:, D_T:] for h in range(2)]] * D_UNROLL
            chains = [(u, h) for u in range(D_UNROLL) for h in range(2)]
            qh = [_split_heads(qc_ref[rows[u], :]) for u in range(D_UNROLL)]
            sc = [_qk(qh[u][h], kc_ref[keys[u], :]) + bias[u][h] for u, h in chains]
            m = [jnp.max(s, axis=-1, keepdims=True) for s in sc]
            e = [jnp.exp(s - mm) for s, mm in zip(sc, m)]
            l = [jnp.sum(x, axis=-1, keepdims=True) for x in e]
            o = [jnp.dot(x.astype(BF16), vc_ref[keys[u], :], preferred_element_type=F32)
                 for x, (u, h) in zip(e, chains)]
            o = [x / ll for x, ll in zip(o, l)]
            lse = [jnp.broadcast_to(mm + jnp.log(ll), (D_T, LANES)) for mm, ll in zip(m, l)]
            for u in range(D_UNROLL):
                oc_ref[rows[u], :] = jnp.where(first, o[2 * u], o[2 * u + 1])
                lc_ref[rows[u], :] = jnp.where(first, lse[2 * u], lse[2 * u + 1])
            return carry

        lax.fori_loop(0, n_tiles // D_UNROLL, tile_group, 0)

        for r in range(dil):
            dst = pl.ds(r, class_len, stride=dil) if dil > 1 else slice(None)
            src = slice(r * class_len, (r + 1) * class_len)
            og_ref[g, dst, :] = oc_ref[src, :]
            lse_ref[g, dst, :] = lc_ref[src, :]

    lse = [lse_ref[g] for g in range(D_GROUPS)]
    top = functools.reduce(jnp.maximum, lse)
    w = [jnp.exp(x - top) for x in lse]
    num = sum(w[g] * og_ref[g] for g in range(D_GROUPS))
    o_ref[...] = (num / sum(w)).astype(o_ref.dtype)


def _attn_d(p_d, bias_d, batch, s):
    n = p_d.shape[0]
    return pl.pallas_call(
        _attn_d_kernel,
        grid=(batch,),
        in_specs=([pl.BlockSpec((s, LANES), functools.partial(lambda b, c: (b, c), c=c))
                   for c in range(3 * D_GROUPS)]
                  + [pl.BlockSpec(bias_d.shape, lambda b: (0, 0, 0, 0, 0))]),
        out_specs=pl.BlockSpec((s, LANES), lambda b: (b, 0)),
        out_shape=jax.ShapeDtypeStruct((n, LANES), BF16),
        scratch_shapes=[pltpu.VMEM((s, LANES), BF16), pltpu.VMEM((s + D_T, LANES), BF16),
                        pltpu.VMEM((s + D_T, LANES), BF16),
                        pltpu.VMEM((s, LANES), F32), pltpu.VMEM((s, LANES), F32),
                        pltpu.VMEM((D_GROUPS, s, LANES), F32),
                        pltpu.VMEM((D_GROUPS, s, LANES), F32)],
        compiler_params=_cparams(1),
        name="attn_dilated",
    )(*([p_d] * (3 * D_GROUPS)), bias_d)


def _merge_kernel(x_ref, g_ref, wg_ref, gb_ref, oa_ref, ob_ref, oc_ref, od_ref,
                  wa_ref, wb_ref, wc_ref, wd_ref, wout_ref, o_ref):
    x = x_ref[...]
    h = _rms(x, g_ref[...]).astype(BF16)
    merged = jnp.zeros(x.shape, F32)
    branches = ((oa_ref, wa_ref), (ob_ref, wb_ref), (oc_ref, wc_ref), (od_ref, wd_ref))
    for i, (br_ref, w_ref) in enumerate(branches):
        cols = slice(i * D_MODEL, (i + 1) * D_MODEL)
        logits = jnp.dot(h, wg_ref[:, cols], preferred_element_type=F32) + gb_ref[:, cols]
        branch = jnp.dot(br_ref[...], w_ref[...], preferred_element_type=F32)
        merged = merged + jax.nn.sigmoid(logits) * branch
    o_ref[...] = x + jnp.dot(merged.astype(BF16), wout_ref[...], preferred_element_type=F32)


def _merge(x2, gain, w_all, layer, gate_bias, outs, w_branch, w_out):
    n = x2.shape[0]
    gate_w = N_BRANCH * D_MODEL
    widths = (A_W, B_W, C_W, 2 * HEAD_DIM)
    offs = np.concatenate([[0], np.cumsum(widths)])
    w_br = [w_branch[offs[i]:offs[i + 1]].astype(BF16) for i in range(N_BRANCH)]
    const = lambda i: (0, 0)
    row = lambda i: (i, 0)
    return pl.pallas_call(
        _merge_kernel,
        grid=(n // TOK_TILE,),
        in_specs=([pl.BlockSpec((TOK_TILE, D_MODEL), row),
                   pl.BlockSpec((1, D_MODEL), const),
                   pl.BlockSpec((None, D_MODEL, gate_w), lambda i: (layer, 0, 0),
                                pipeline_mode=pl.Buffered(1)),
                   pl.BlockSpec((1, gate_w), const)]
                  + [pl.BlockSpec((TOK_TILE, w), row) for w in widths]
                  + [pl.BlockSpec((w, D_MODEL), const) for w in widths]
                  + [pl.BlockSpec((D_MODEL, D_MODEL), const)]),
        out_specs=pl.BlockSpec((TOK_TILE, D_MODEL), row),
        out_shape=jax.ShapeDtypeStruct((n, D_MODEL), F32),
        compiler_params=_cparams(1),
        name="merge",
    )(x2, gain.reshape(1, D_MODEL), w_all, gate_bias.reshape(1, -1).astype(F32),
      *outs, *w_br, w_out.astype(BF16))


def kernel(x, rel_table, ffn1_norm, ffn1_w_in, ffn1_w_out, mix_norm, w_in, gate_bias, forget_bias,
           a_q_norm, a_k_norm, a_lambda, a_subln, c_q_norm, c_k_norm, d_q_norm, d_k_norm,
           w_branch, w_out, ffn2_norm, ffn2_w_in, ffn2_w_out):
    batch, s, _ = x.shape
    depth = w_in.shape[0]
    assert s % (D_T * D_PAIRS[-1][1]) == 0 and (batch * s) % TOK_TILE == 0
    bias_a, bias_d = _bias_tiles(rel_table)
    w_all = _prep_w_in(w_in)
    x2 = x.reshape(batch * s, D_MODEL)
    for l in range(depth):
        x2 = _ffn(x2, ffn1_norm[l], ffn1_w_in[l], ffn1_w_out[l])
        p_abc, p_d, cf = _proj(x2, mix_norm[l], w_all, l, a_q_norm[l], a_k_norm[l],
                               c_q_norm[l], c_k_norm[l], d_q_norm[l], d_k_norm[l])
        lam_init = 0.8 - 0.6 * math.exp(-0.3 * l)
        oa = _attn_a(p_abc, bias_a, a_lambda[l], a_subln[l], lam_init, batch, s)
        ob = _attn_b(p_abc, batch, s)
        oc = _attn_c(p_abc, cf, forget_bias[l], batch, s)
        od = _attn_d(p_d, bias_d, batch, s)
        x2 = _merge(x2, mix_norm[l], w_all, l, gate_bias[l], (oa, ob, oc, od),
                    w_branch[l], w_out[l])
        x2 = _ffn(x2, ffn2_norm[l], ffn2_w_in[l], ffn2_w_out[l])
    return x2.reshape(batch, s, D_MODEL)
```

```python
import functools
import math

import numpy as np
import jax
import jax.numpy as jnp
from jax import lax
from jax.experimental import pallas as pl
from jax.experimental.pallas import tpu as pltpu

F32 = jnp.float32
BF16 = jnp.bfloat16

D_MODEL = 1024
HEAD_DIM = 64
SCALE = HEAD_DIM ** -0.5
A_HEADS = 4
B_HEADS = 6
C_HEADS = 6
D_PAIRS = ((128, 1), (512, 4), (2048, 16))
D_GROUPS = len(D_PAIRS)
N_BRANCH = 4
FFN_HIDDEN = 2816
REL_BUCKETS = 32
REL_MAX_DIST = 128
RMS_EPS = 1e-6
NEG_INF = -1e30

LANES = 128
A_W = A_HEADS * 2 * HEAD_DIM
B_W = B_HEADS * HEAD_DIM
C_W = C_HEADS * HEAD_DIM
D_W = D_GROUPS * 2 * HEAD_DIM
ABC_W = 3 * (A_W + B_W + C_W)

W_SHIFT = (ABC_W + C_HEADS) % LANES
W_ALIGNED_BLOCKS = ABC_W // LANES + 1
W_ABC_OFF = N_BRANCH * D_MODEL
W_CF_OFF = W_ABC_OFF + ABC_W
W_D_OFF = W_ABC_OFF + W_ALIGNED_BLOCKS * LANES
W_D_END = W_D_OFF + 3 * D_W
GROUP_W = 256

TOK_TILE = 512
FFN_CHUNK = 256
ATT_T = 256
ATT_WIN = 2 * ATT_T
LOG2E = math.log2(math.e)
ATT_ROWS = 2
D_UNROLL = 8
D_T = 128
VMEM_LIMIT = 56 * 1024 * 1024


def _cparams(n_axes):
    return pltpu.CompilerParams(dimension_semantics=("arbitrary",) * n_axes,
                                vmem_limit_bytes=VMEM_LIMIT)


def _rms(x, g):
    ms = jnp.mean(x * x, axis=-1, keepdims=True)
    return x * lax.rsqrt(ms + RMS_EPS) * g


def _split3(a):
    hi = a.astype(BF16)
    r1 = a - hi.astype(F32)
    mid = r1.astype(BF16)
    lo = (r1 - mid.astype(F32)).astype(BF16)
    return hi, mid, lo


def _dot_exact_lhs(a_bf16, b):
    hi, mid, lo = _split3(b)
    d = functools.partial(jnp.dot, preferred_element_type=F32)
    return d(a_bf16, hi) + d(a_bf16, mid) + d(a_bf16, lo)


def _qk(q, k):
    return lax.dot_general(q, k, (((1,), (1,)), ((), ())), preferred_element_type=F32)


def _log_sigmoid(x):
    return jnp.minimum(x, 0.0) - jnp.log(1.0 + jnp.exp(-jnp.abs(x)))


def _lane_half(shape):
    return lax.broadcasted_iota(jnp.int32, shape, len(shape) - 1) < HEAD_DIM


def _ffn_kernel(x_ref, g_ref, wi_ref, wo_ref, o_ref):
    x = x_ref[...]
    h = _rms(x, g_ref[...]).astype(BF16)
    acc = jnp.zeros(x.shape, F32)
    for c in range(FFN_HIDDEN // FFN_CHUNK):
        lo = c * FFN_CHUNK
        gate = jnp.dot(h, wi_ref[:, lo:lo + FFN_CHUNK], preferred_element_type=F32)
        up = jnp.dot(h, wi_ref[:, FFN_HIDDEN + lo:FFN_HIDDEN + lo + FFN_CHUNK],
                     preferred_element_type=F32)
        a = (gate * jax.nn.sigmoid(gate) * up).astype(BF16)
        acc = acc + jnp.dot(a, wo_ref[c * FFN_CHUNK:(c + 1) * FFN_CHUNK, :],
                            preferred_element_type=F32)
    o_ref[...] = x + 0.5 * acc


def _ffn(x2, gain, w_i, w_o):
    n = x2.shape[0]
    const = lambda i: (0, 0)
    return pl.pallas_call(
        _ffn_kernel,
        grid=(n // TOK_TILE,),
        in_specs=[
            pl.BlockSpec((TOK_TILE, D_MODEL), lambda i: (i, 0)),
            pl.BlockSpec((1, D_MODEL), const),
            pl.BlockSpec((D_MODEL, 2 * FFN_HIDDEN), const, pipeline_mode=pl.Buffered(1)),
            pl.BlockSpec((FFN_HIDDEN, D_MODEL), const, pipeline_mode=pl.Buffered(1)),
        ],
        out_specs=pl.BlockSpec((TOK_TILE, D_MODEL), lambda i: (i, 0)),
        out_shape=jax.ShapeDtypeStruct((n, D_MODEL), F32),
        compiler_params=_cparams(1),
        name="ffn",
    )(x2, gain.reshape(1, D_MODEL), w_i.astype(BF16), w_o.astype(BF16))


_ABC_SECTIONS = ((A_W, True), (A_W, True), (A_W, False),
                 (B_W, False), (B_W, False), (B_W, False),
                 (C_W, True), (C_W, True), (C_W, False))
_D_SECTIONS = ((D_W, True), (D_W, True), (D_W, False))


def _proj_sections(y, gain_ref, gmat, out_ref, sections):
    off = 0
    for width, normed in sections:
        lo = off
        while lo < off + width:
            blk = min(GROUP_W, off + width - lo)
            yb = y[:, lo:lo + blk]
            if normed:
                ms = jnp.dot((yb * yb).astype(BF16), gmat[:blk, :blk], preferred_element_type=F32)
                yb = yb * lax.rsqrt(ms + RMS_EPS)
            out_ref[:, lo:lo + blk] = (yb * gain_ref[:, lo:lo + blk]).astype(out_ref.dtype)
            lo += blk
        off += width


def _proj_kernel(x_ref, g_ref, w_ref, gabc_ref, gd_ref, gmat_ref, pabc_ref, pd_ref, cf_ref):
    h = _rms(x_ref[...], g_ref[...]).astype(BF16)
    gmat = gmat_ref[...]
    y_abc = jnp.dot(h, w_ref[:, W_ABC_OFF:W_CF_OFF], preferred_element_type=F32)
    _proj_sections(y_abc, gabc_ref, gmat, pabc_ref, _ABC_SECTIONS)
    y_d = jnp.dot(h, w_ref[:, W_CF_OFF:W_D_END], preferred_element_type=F32)
    cf_ref[...] = y_d[:, :LANES]
    _proj_sections(y_d[:, LANES:], gd_ref, gmat, pd_ref, _D_SECTIONS)


def _head_gain(gain, width, scale):
    return jnp.tile(gain.astype(F32), width // HEAD_DIM) * scale


def _win_kernel(a_ref, b_ref, o_ref):
    c = pl.program_id(1)

    @pl.when(c < W_ALIGNED_BLOCKS)
    def _():
        o_ref[...] = b_ref[...].astype(BF16)

    @pl.when(c >= W_ALIGNED_BLOCKS)
    def _():
        both = jnp.concatenate([a_ref[:, W_SHIFT:], b_ref[:, :W_SHIFT]], axis=1)
        o_ref[...] = both.astype(BF16)


def _prep_w_in(w_in):
    depth = w_in.shape[0]
    n_blocks = pl.cdiv(w_in.shape[2], LANES)
    gate_block0 = (W_D_END - W_ABC_OFF) // LANES
    return pl.pallas_call(
        _win_kernel,
        grid=(depth, n_blocks),
        in_specs=[pl.BlockSpec((None, D_MODEL, LANES), lambda l, c: (l, 0, jnp.maximum(c - 1, 0))),
                  pl.BlockSpec((None, D_MODEL, LANES), lambda l, c: (l, 0, c))],
        out_specs=pl.BlockSpec(
            (None, D_MODEL, LANES),
            lambda l, c: (l, 0, jnp.where(c < gate_block0, c + W_ABC_OFF // LANES, c - gate_block0))),
        out_shape=jax.ShapeDtypeStruct((depth, D_MODEL, W_D_END), BF16),
        compiler_params=_cparams(2),
        name="prep_w_in",
    )(w_in, w_in)


def _proj(x2, gain, w_all, layer, a_q, a_k, c_q, c_k, d_q, d_k):
    n = x2.shape[0]
    ones = lambda w, s: jnp.full((w,), s, F32)
    g_abc = jnp.concatenate([
        _head_gain(a_q, A_W, SCALE * LOG2E), _head_gain(a_k, A_W, 1.0), ones(A_W, 1.0),
        ones(B_W, SCALE * LOG2E), ones(B_W, 1.0), ones(B_W, 1.0),
        _head_gain(c_q, C_W, SCALE * LOG2E), _head_gain(c_k, C_W, 1.0),
        ones(C_W, 1.0)]).reshape(1, ABC_W)
    g_d = jnp.concatenate([_head_gain(d_q, D_W, SCALE), _head_gain(d_k, D_W, 1.0),
                           ones(D_W, 1.0)]).reshape(1, 3 * D_W)
    lane = np.arange(GROUP_W)
    gmat = jnp.asarray((lane[:, None] // HEAD_DIM == lane[None, :] // HEAD_DIM) / HEAD_DIM, BF16)
    const = lambda i: (0, 0)
    row = lambda i: (i, 0)
    return pl.pallas_call(
        _proj_kernel,
        grid=(n // TOK_TILE,),
        in_specs=[
            pl.BlockSpec((TOK_TILE, D_MODEL), row),
            pl.BlockSpec((1, D_MODEL), const),
            pl.BlockSpec((None, D_MODEL, W_D_END), lambda i: (layer, 0, 0),
                         pipeline_mode=pl.Buffered(1)),
            pl.BlockSpec((1, ABC_W), const),
            pl.BlockSpec((1, 3 * D_W), const),
            pl.BlockSpec((GROUP_W, GROUP_W), const),
        ],
        out_specs=[
            pl.BlockSpec((TOK_TILE, ABC_W), row),
            pl.BlockSpec((TOK_TILE, 3 * D_W), row),
            pl.BlockSpec((TOK_TILE, LANES), row),
        ],
        out_shape=[
            jax.ShapeDtypeStruct((n, ABC_W), BF16),
            jax.ShapeDtypeStruct((n, 3 * D_W), F32),
            jax.ShapeDtypeStruct((n, LANES), F32),
        ],
        compiler_params=_cparams(1),
        name="proj",
    )(x2, gain.reshape(1, D_MODEL), w_all, g_abc, g_d, gmat)


_MASKED_BUCKET = REL_BUCKETS


def _rel_bucket_np(n):
    n = np.asarray(n, np.int64)
    max_exact = REL_BUCKETS // 2
    nf = np.maximum(n, 1).astype(np.float32)
    large = max_exact + (np.log(nf / np.float32(max_exact)) / np.float32(math.log(REL_MAX_DIST / max_exact))
                         * np.float32(REL_BUCKETS - max_exact)).astype(np.int32)
    large = np.minimum(large, REL_BUCKETS - 1)
    return np.where(n < max_exact, n, large).astype(np.int32)


def _bucket_maps():
    t = ATT_T
    x, y = np.arange(t)[:, None], np.arange(t)[None, :]
    diag = np.where(x >= y, _rel_bucket_np(np.maximum(x - y, 0)), _MASKED_BUCKET)
    prev = _rel_bucket_np(t + x - y)
    idx_a = np.stack([prev.T, diag.T]).astype(np.int32)
    x, y = np.arange(D_T)[:, None], np.arange(D_T)[None, :]
    idx_d = np.empty((D_GROUPS, 2, D_T, 2 * D_T), np.int32)
    for g, (_, dil) in enumerate(D_PAIRS):
        prev = np.where(y >= x, _rel_bucket_np(dil * (D_T + x - y)), _MASKED_BUCKET)
        diag = np.where(y <= x, _rel_bucket_np(dil * np.maximum(x - y, 0)), _MASKED_BUCKET)
        idx_d[g, 0] = np.concatenate([prev, diag], axis=1)
        idx_d[g, 1] = np.concatenate([np.full_like(prev, _MASKED_BUCKET), diag], axis=1)
    return idx_a, idx_d


def _bias_kernel(tbl_ref, idxa_ref, idxd_ref, ba_ref, bd_ref):
    for h in range(A_HEADS):
        for kind in range(2):
            ba_ref[h, kind] = jnp.where(idxa_ref[kind] == _MASKED_BUCKET, NEG_INF, 0.0)
        ba_ref[h, 2] = jnp.zeros((ATT_T, ATT_T), F32)
        ba_ref[h, 3] = jnp.full((ATT_T, ATT_T), NEG_INF, F32)
    for g in range(D_GROUPS):
        for h in range(2):
            for var in range(2):
                bd_ref[g, h, var] = jnp.where(idxd_ref[g, var] == _MASKED_BUCKET, NEG_INF, 0.0)

    def body(b, carry):
        for h in range(A_HEADS):
            val = (tbl_ref[b, h] - tbl_ref[REL_BUCKETS - 1, h]) * LOG2E
            for kind in range(2):
                ba_ref[h, kind] = jnp.where(idxa_ref[kind] == b, val, ba_ref[h, kind])
        for g in range(D_GROUPS):
            for h in range(2):
                val = tbl_ref[b, A_HEADS + 2 * g + h]
                for var in range(2):
                    bd_ref[g, h, var] = jnp.where(idxd_ref[g, var] == b, val, bd_ref[g, h, var])
        return carry

    lax.fori_loop(0, REL_BUCKETS, body, 0)


def _bias_tiles(rel_table):
    idx_a, idx_d = _bucket_maps()
    return pl.pallas_call(
        _bias_kernel,
        in_specs=[pl.BlockSpec(memory_space=pltpu.SMEM),
                  pl.BlockSpec(memory_space=pltpu.VMEM),
                  pl.BlockSpec(memory_space=pltpu.VMEM)],
        out_specs=[pl.BlockSpec(memory_space=pltpu.VMEM),
                   pl.BlockSpec(memory_space=pltpu.VMEM)],
        out_shape=[jax.ShapeDtypeStruct((A_HEADS, 4, ATT_T, ATT_T), F32),
                   jax.ShapeDtypeStruct((D_GROUPS, 2, 2, D_T, 2 * D_T), F32)],
        compiler_params=pltpu.CompilerParams(vmem_limit_bytes=VMEM_LIMIT),
        name="rel_bias",
    )(rel_table.astype(F32), jnp.asarray(idx_a), jnp.asarray(idx_d))


def _rows(idx, size):
    return pl.ds(pl.multiple_of(idx * size, size), size)


def _split_heads(q):
    first = _lane_half((q.shape[0], LANES))
    zero = jnp.zeros((q.shape[0], LANES), q.dtype)
    out = []
    for p in range(q.shape[1] // LANES):
        qp = q[:, p * LANES:(p + 1) * LANES]
        out += [jnp.where(first, qp, zero), jnp.where(first, zero, qp)]
    return out


def _transpose_into(vt_ref, v_ref):
    for p in range(v_ref.shape[1] // LANES):
        for blk in range(v_ref.shape[0] // ATT_T):
            tile = v_ref[blk * ATT_T:(blk + 1) * ATT_T, p * LANES:(p + 1) * LANES]
            vt_ref[p * LANES:(p + 1) * LANES, blk * ATT_T:(blk + 1) * ATT_T] = (
                tile.astype(F32).T.astype(vt_ref.dtype))


def _attn_specs_rows(width, col0, s, rows):
    q_spec = pl.BlockSpec((rows, ATT_T, width), lambda b, i: (b, i, col0))
    k_spec = pl.BlockSpec((rows, s, width), lambda b, i: (b, 0, col0 + 1))
    v_spec = pl.BlockSpec((rows, s, width), lambda b, i: (b, 0, col0 + 2))
    o_spec = pl.BlockSpec((rows, ATT_T, width), lambda b, i: (b, i, 0))
    return q_spec, k_spec, v_spec, o_spec


def _window_split(qi):
    return lax.shift_right_logical(qi + 1, 1), jnp.bitwise_and(qi, 1)


def _softmax_stages_t(scores, ms, acc_ref, vts, ls=None):
    n = len(scores)
    m_new = [jnp.maximum(ms[c], jnp.max(scores[c], axis=0, keepdims=True)) for c in range(n)]
    alpha = [jnp.exp2(ms[c] - m_new[c]) for c in range(n)]
    p_bf16, l_new = [], []
    for c in range(n):
        p = jnp.exp2(scores[c] - m_new[c])
        p_bf16.append(p.astype(BF16))
        if ls is not None:
            l_new.append(alpha[c] * ls[c] + jnp.sum(p, axis=0, keepdims=True))
    pv = [jnp.dot(vts[c], p_bf16[c], preferred_element_type=F32) for c in range(n)]
    for c in range(n):
        acc_ref[c] = alpha[c] * acc_ref[c] + pv[c]
    return m_new, (l_new if ls is not None else None)


def _attn_a_kernel(q_ref, k_ref, v_ref, bias_ref, lam_ref, subln_ref, o_ref, acc_ref, vt_ref, *,
                   lam_init):
    qi = pl.program_id(1)
    n_rows = q_ref.shape[0]

    @pl.when(qi == 0)
    def _():
        for b in range(n_rows):
            _transpose_into(vt_ref.at[b], v_ref.at[b])

    qs = [q for b in range(n_rows) for q in _split_heads(q_ref[b])]
    n_chain = len(qs)
    acc_ref[...] = jnp.zeros(acc_ref.shape, F32)
    stat = lambda val: jnp.full((1, ATT_T), val, F32)
    init = (stat(NEG_INF), stat(0.0)) * n_chain
    n_full, odd = _window_split(qi)

    def window(rows, carry, kinds):
        scores, vts = [], []
        for b in range(n_rows):
            for h in range(A_HEADS):
                cols = slice(h * LANES, (h + 1) * LANES)
                k, vt = k_ref[b, rows, cols], vt_ref[b, cols, rows]
                if kinds is not None:
                    bias = jnp.concatenate([bias_ref[h, kind] for kind in kinds], axis=0)
                for half in range(2):
                    s = _qk(k, qs[(b * A_HEADS + h) * 2 + half])
                    scores.append(s if kinds is None else s + bias)
                    vts.append(vt)
        ms, ls = _softmax_stages_t(scores, carry[0::2], acc_ref, vts, ls=carry[1::2])
        return tuple(x for pair in zip(ms, ls) for x in pair)

    n_plain = jnp.maximum(n_full - 1, 0)
    carry = lax.fori_loop(0, n_plain, lambda j, c: window(_rows(j, ATT_WIN), c, None), init)
    near = (jnp.where(odd == 1, 0, 2), jnp.where(odd == 1, 1, 0))
    carry = lax.fori_loop(n_plain, n_full, lambda j, c: window(_rows(j, ATT_WIN), c, near), carry)
    carry = lax.fori_loop(0, 1 - odd, lambda _, c: window(_rows(qi, ATT_T), c, (1,)), carry)

    lv = lam_ref[...]
    lam = (jnp.exp(jnp.sum(lv[0:1] * lv[1:2], axis=-1, keepdims=True))
           - jnp.exp(jnp.sum(lv[2:3] * lv[3:4], axis=-1, keepdims=True)) + lam_init)
    for b in range(n_rows):
        for h in range(A_HEADS):
            c0 = (b * A_HEADS + h) * 2
            l0, l1 = carry[2 * c0 + 1], carry[2 * c0 + 3]
            o = acc_ref[c0] / l0 - lam * (acc_ref[c0 + 1] / l1)
            ms = jnp.mean(o * o, axis=0, keepdims=True)
            o = o * lax.rsqrt(ms + RMS_EPS) * subln_ref[...] * (1.0 - lam_init)
            o_ref[b, :, h * LANES:(h + 1) * LANES] = o.T.astype(o_ref.dtype)


def _attn_a(p_abc, bias_a, lam_vecs, subln, lam_init, batch, s):
    n = p_abc.shape[0]
    rows = ATT_ROWS if batch % ATT_ROWS == 0 else 1
    q_spec, k_spec, v_spec, o_spec = _attn_specs_rows(A_W, 0, s, rows)
    p3 = p_abc.reshape(batch, s, ABC_W)
    out = pl.pallas_call(
        functools.partial(_attn_a_kernel, lam_init=lam_init),
        grid=(batch // rows, s // ATT_T),
        in_specs=[q_spec, k_spec, v_spec,
                  pl.BlockSpec(bias_a.shape, lambda b, i: (0, 0, 0, 0)),
                  pl.BlockSpec((4, HEAD_DIM), lambda b, i: (0, 0)),
                  pl.BlockSpec((2 * HEAD_DIM, ATT_T), lambda b, i: (0, 0))],
        out_specs=o_spec,
        out_shape=jax.ShapeDtypeStruct((batch, s, A_W), BF16),
        scratch_shapes=[pltpu.VMEM((rows * 2 * A_HEADS, LANES, ATT_T), F32),
                        pltpu.VMEM((rows, A_W, s), BF16)],
        compiler_params=_cparams(2),
        name="attn_diff",
    )(p3, p3, p3, bias_a, lam_vecs.astype(F32),
      jnp.broadcast_to(subln.astype(F32)[:, None], (2 * HEAD_DIM, ATT_T)))
    return out.reshape(n, A_W)


def _attn_b_kernel(q_ref, k_ref, v_ref, o_ref, acc_ref, vt_ref):
    qi = pl.program_id(1)
    n_rows = q_ref.shape[0]

    @pl.when(qi == 0)
    def _():
        for b in range(n_rows):
            _transpose_into(vt_ref.at[b], v_ref.at[b])

    qs = [q for b in range(n_rows) for q in _split_heads(q_ref[b])]
    acc_ref[...] = jnp.zeros(acc_ref.shape, F32)
    key = lax.broadcasted_iota(jnp.int32, (ATT_T, ATT_T), 0)
    qry = lax.broadcasted_iota(jnp.int32, (ATT_T, ATT_T), 1)
    later = jnp.where(qry > key, 1.0, 0.0).astype(BF16)
    strict = key < qry

    def step(kj, rsum, diag):
        rows = _rows(kj, ATT_T)
        heads = range(n_rows * B_HEADS)
        row_of = [h // B_HEADS for h in heads]
        cols = [slice(((h % B_HEADS) // 2) * LANES, ((h % B_HEADS) // 2 + 1) * LANES) for h in heads]
        z = [_qk(k_ref[row_of[h], rows, cols[h]], qs[h]) for h in heads]
        ls = [jnp.minimum(z[h], 0.0) - jnp.log2(1.0 + jnp.exp2(-jnp.abs(z[h]))) for h in heads]
        u = [ls[h] - z[h] for h in heads]
        if diag:
            u = [jnp.where(strict, u[h], 0.0) for h in heads]
        a = [jnp.exp2(ls[h] + jnp.dot(later, u[h].astype(BF16), preferred_element_type=F32)
                      + rsum[h]) for h in heads]
        if diag:
            a = [jnp.where(strict, a[h], 0.0) for h in heads]
        pv = [jnp.dot(vt_ref[row_of[h], cols[h], rows], a[h].astype(BF16),
                      preferred_element_type=F32) for h in heads]
        for h in heads:
            acc_ref[h] = acc_ref[h] + pv[h]
        return tuple(rsum[h] + jnp.sum(u[h], axis=0, keepdims=True) for h in heads)

    zero = jnp.zeros((1, ATT_T), F32)
    rsum = step(qi, (zero,) * (n_rows * B_HEADS), True)
    lax.fori_loop(0, qi, lambda t, c: step(qi - 1 - t, c, False), rsum)
    for b in range(n_rows):
        for p in range(B_HEADS // 2):
            c0 = b * B_HEADS + 2 * p
            o_t = jnp.concatenate([acc_ref[c0][:HEAD_DIM], acc_ref[c0 + 1][HEAD_DIM:]], axis=0)
            o_ref[b, :, p * LANES:(p + 1) * LANES] = o_t.T.astype(o_ref.dtype)


def _attn_b(p_abc, batch, s):
    n = p_abc.shape[0]
    rows = ATT_ROWS if batch % ATT_ROWS == 0 else 1
    q_spec, k_spec, v_spec, o_spec = _attn_specs_rows(B_W, 3 * A_W // B_W, s, rows)
    p3 = p_abc.reshape(batch, s, ABC_W)
    out = pl.pallas_call(
        _attn_b_kernel,
        grid=(batch // rows, s // ATT_T),
        in_specs=[q_spec, k_spec, v_spec],
        out_specs=o_spec,
        out_shape=jax.ShapeDtypeStruct((batch, s, B_W), BF16),
        scratch_shapes=[pltpu.VMEM((rows * B_HEADS, LANES, ATT_T), F32),
                        pltpu.VMEM((rows, B_W, s), BF16)],
        compiler_params=_cparams(2),
        name="attn_stick",
    )(p3, p3, p3)
    return out.reshape(n, B_W)


def _attn_c_kernel(q_ref, k_ref, v_ref, cf_ref, fb_ref, o_ref, acc_ref, cumc_ref, cumr_ref, vaug_ref):
    qi = pl.program_id(1)
    n_rows, s_len = cf_ref.shape[0], cf_ref.shape[1]

    @pl.when(qi == 0)
    def _():
        r = lax.broadcasted_iota(jnp.int32, (LANES, LANES), 0)
        c = lax.broadcasted_iota(jnp.int32, (LANES, LANES), 1)
        lower = jnp.where(c <= r, 1.0, 0.0).astype(BF16)
        own = lax.broadcasted_iota(jnp.int32, (LANES, ATT_T), 0) < HEAD_DIM
        for b in range(n_rows):
            carry = jnp.zeros((1, LANES), F32)
            for blk in range(s_len // LANES):
                rows = slice(blk * LANES, (blk + 1) * LANES)
                log_f = _log_sigmoid(cf_ref[b, rows, :] + fb_ref[...])
                cum = _dot_exact_lhs(lower, log_f) + carry
                cumc_ref[b, rows, :] = cum * LOG2E
                cumr_ref[b, :, rows] = (cum * LOG2E).T
                carry = cum[LANES - 1:LANES, :]
            for p in range(C_HEADS // 2):
                for blk in range(s_len // ATT_T):
                    tile = v_ref[b, blk * ATT_T:(blk + 1) * ATT_T, p * LANES:(p + 1) * LANES]
                    tile_t = tile.astype(F32).T
                    dst = (slice(p * LANES, (p + 1) * LANES), slice(blk * ATT_T, (blk + 1) * ATT_T))
                    vaug_ref[(b, 0) + dst] = jnp.where(own, tile_t, 1.0).astype(BF16)
                    vaug_ref[(b, 1) + dst] = jnp.where(own, 1.0, tile_t).astype(BF16)

    chains = [(b, h) for b in range(n_rows) for h in range(C_HEADS)]
    qs = [q for b in range(n_rows) for q in _split_heads(q_ref[b])]
    acc_ref[...] = jnp.zeros(acc_ref.shape, F32)
    q_cols = _rows(qi, ATT_T)
    cum_q = [cumr_ref[b, h:h + 1, q_cols] for b, h in chains]
    n_full, odd = _window_split(qi)

    def causal(width):
        key = lax.broadcasted_iota(jnp.int32, (width, ATT_T), 0) - (width - ATT_T)
        return jnp.where(key <= lax.broadcasted_iota(jnp.int32, (width, ATT_T), 1), 0.0, NEG_INF)

    def window(rows, carry, mask):
        scores, vts = [], []
        for c, (b, h) in enumerate(chains):
            cols = slice((h // 2) * LANES, (h // 2 + 1) * LANES)
            s = _qk(k_ref[b, rows, cols], qs[c]) + cum_q[c] - cumc_ref[b, rows, h:h + 1]
            scores.append(s if mask is None else s + mask)
            vts.append(vaug_ref[b, h % 2, cols, rows])
        return tuple(_softmax_stages_t(scores, carry, acc_ref, vts)[0])

    init = (jnp.full((1, ATT_T), NEG_INF, F32),) * len(chains)
    carry = lax.fori_loop(0, n_full - odd, lambda j, c: window(_rows(j, ATT_WIN), c, None), init)
    carry = lax.fori_loop(n_full - odd, n_full,
                          lambda j, c: window(_rows(j, ATT_WIN), c, causal(ATT_WIN)), carry)
    lax.fori_loop(0, 1 - odd, lambda _, c: window(_rows(qi, ATT_T), c, causal(ATT_T)), carry)
    for b in range(n_rows):
        for p in range(C_HEADS // 2):
            c0 = b * C_HEADS + 2 * p
            a0, a1 = acc_ref[c0], acc_ref[c0 + 1]
            num = jnp.concatenate([a0[:HEAD_DIM], a1[HEAD_DIM:]], axis=0)
            den = jnp.concatenate([a0[HEAD_DIM:], a1[:HEAD_DIM]], axis=0)
            o_ref[b, :, p * LANES:(p + 1) * LANES] = (num / den).T.astype(o_ref.dtype)


def _attn_c(p_abc, cf, forget_bias, batch, s):
    n = p_abc.shape[0]
    rows = ATT_ROWS if batch % ATT_ROWS == 0 else 1
    q_spec, k_spec, v_spec, o_spec = _attn_specs_rows(C_W, 3 * (A_W + B_W) // C_W, s, rows)
    fb = jnp.pad(forget_bias.astype(F32), (0, LANES - C_HEADS)).reshape(1, LANES)
    p3 = p_abc.reshape(batch, s, ABC_W)
    out = pl.pallas_call(
        _attn_c_kernel,
        grid=(batch // rows, s // ATT_T),
        in_specs=[q_spec, k_spec, v_spec,
                  pl.BlockSpec((rows, s, LANES), lambda b, i: (b, 0, 0)),
                  pl.BlockSpec((1, LANES), lambda b, i: (0, 0))],
        out_specs=o_spec,
        out_shape=jax.ShapeDtypeStruct((batch, s, C_W), BF16),
        scratch_shapes=[pltpu.VMEM((rows * C_HEADS, LANES, ATT_T), F32),
                        pltpu.VMEM((rows, s, LANES), F32),
                        pltpu.VMEM((rows, LANES, s), F32),
                        pltpu.VMEM((rows, 2, C_W, s), BF16)],
        compiler_params=_cparams(2),
        name="attn_forget",
    )(p3, p3, p3, cf.reshape(batch, s, LANES), fb)
    return out.reshape(n, C_W)


def _attn_d_kernel(*refs):
    qkv_refs, (bias_ref, o_ref, qc_ref, kc_ref, vc_ref, oc_ref, lc_ref, og_ref, lse_ref) = (
        refs[:3 * D_GROUPS], refs[3 * D_GROUPS:])
    s_len = o_ref.shape[0]
    n_tiles = s_len // D_T
    first = _lane_half((D_T, LANES))
    kc_ref[0:D_T, :] = jnp.zeros((D_T, LANES), BF16)
    vc_ref[0:D_T, :] = jnp.zeros((D_T, LANES), BF16)

    for g, (_, dil) in enumerate(D_PAIRS):
        q_ref, k_ref, v_ref = qkv_refs[g], qkv_refs[D_GROUPS + g], qkv_refs[2 * D_GROUPS + g]
        class_len = s_len // dil
        tiles_per_class = class_len // D_T

        for r in range(dil):
            src = pl.ds(r, class_len, stride=dil) if dil > 1 else slice(None)
            dst = slice(r * class_len, (r + 1) * class_len)
            dst_kv = slice(D_T + r * class_len, D_T + (r + 1) * class_len)
            qc_ref[dst, :] = q_ref[src, :].astype(BF16)
            kc_ref[dst_kv, :] = k_ref[src, :].astype(BF16)
            vc_ref[dst_kv, :] = v_ref[src, :].astype(BF16)

        def tile_group(i, carry):
            tiles = [i * D_UNROLL + u for u in range(D_UNROLL)]
            rows = [pl.ds(pl.multiple_of(t * D_T, D_T), D_T) for t in tiles]
            if tiles_per_class > 1:
                keys = [pl.ds(pl.multiple_of(t * D_T, D_T), 2 * D_T) for t in tiles]
                variant = [jnp.where(t % tiles_per_class == 0, 1, 0) for t in tiles]
                bias = [[bias_ref[g, h, variant[u]] for h in range(2)] for u in range(D_UNROLL)]
            else:
                keys = [pl.ds(pl.multiple_of(t * D_T + D_T, D_T), D_T) for t in tiles]
                bias = [[bias_ref[g, h, 1][:, D_T:] for h in range(2)]] * D_UNROLL
            chains = [(u, h) for u in range(D_UNROLL) for h in range(2)]
            qh = [_split_heads(qc_ref[rows[u], :]) for u in range(D_UNROLL)]
            sc = [_qk(qh[u][h], kc_ref[keys[u], :]) + bias[u][h] for u, h in chains]
            m = [jnp.max(s, axis=-1, keepdims=True) for s in sc]
            e = [jnp.exp(s - mm) for s, mm in zip(sc, m)]
            l = [jnp.sum(x, axis=-1, keepdims=True) for x in e]
            o = [jnp.dot(x.astype(BF16), vc_ref[keys[u], :], preferred_element_type=F32)
                 for x, (u, h) in zip(e, chains)]
            o = [x / ll for x, ll in zip(o, l)]
            lse = [jnp.broadcast_to(mm + jnp.log(ll), (D_T, LANES)) for mm, ll in zip(m, l)]
            for u in range(D_UNROLL):
                oc_ref[rows[u], :] = jnp.where(first, o[2 * u], o[2 * u + 1])
                lc_ref[rows[u], :] = jnp.where(first, lse[2 * u], lse[2 * u + 1])
            return carry

        lax.fori_loop(0, n_tiles // D_UNROLL, tile_group, 0)

        for r in range(dil):
            dst = pl.ds(r, class_len, stride=dil) if dil > 1 else slice(None)
            src = slice(r * class_len, (r + 1) * class_len)
            og_ref[g, dst, :] = oc_ref[src, :]
            lse_ref[g, dst, :] = lc_ref[src, :]

    lse = [lse_ref[g] for g in range(D_GROUPS)]
    top = functools.reduce(jnp.maximum, lse)
    w = [jnp.exp(x - top) for x in lse]
    num = sum(w[g] * og_ref[g] for g in range(D_GROUPS))
    o_ref[...] = (num / sum(w)).astype(o_ref.dtype)


def _attn_d(p_d, bias_d, batch, s):
    n = p_d.shape[0]
    return pl.pallas_call(
        _attn_d_kernel,
        grid=(batch,),
        in_specs=([pl.BlockSpec((s, LANES), functools.partial(lambda b, c: (b, c), c=c))
                   for c in range(3 * D_GROUPS)]
                  + [pl.BlockSpec(bias_d.shape, lambda b: (0, 0, 0, 0, 0))]),
        out_specs=pl.BlockSpec((s, LANES), lambda b: (b, 0)),
        out_shape=jax.ShapeDtypeStruct((n, LANES), BF16),
        scratch_shapes=[pltpu.VMEM((s, LANES), BF16), pltpu.VMEM((s + D_T, LANES), BF16),
                        pltpu.VMEM((s + D_T, LANES), BF16),
                        pltpu.VMEM((s, LANES), F32), pltpu.VMEM((s, LANES), F32),
                        pltpu.VMEM((D_GROUPS, s, LANES), F32),
                        pltpu.VMEM((D_GROUPS, s, LANES), F32)],
        compiler_params=_cparams(1),
        name="attn_dilated",
    )(*([p_d] * (3 * D_GROUPS)), bias_d)


def _merge_kernel(x_ref, g_ref, wg_ref, gb_ref, oa_ref, ob_ref, oc_ref, od_ref,
                  wa_ref, wb_ref, wc_ref, wd_ref, wout_ref, o_ref):
    x = x_ref[...]
    h = _rms(x, g_ref[...]).astype(BF16)
    merged = jnp.zeros(x.shape, F32)
    branches = ((oa_ref, wa_ref), (ob_ref, wb_ref), (oc_ref, wc_ref), (od_ref, wd_ref))
    for i, (br_ref, w_ref) in enumerate(branches):
        cols = slice(i * D_MODEL, (i + 1) * D_MODEL)
        logits = jnp.dot(h, wg_ref[:, cols], preferred_element_type=F32) + gb_ref[:, cols]
        branch = jnp.dot(br_ref[...], w_ref[...], preferred_element_type=F32)
        merged = merged + jax.nn.sigmoid(logits) * branch
    o_ref[...] = x + jnp.dot(merged.astype(BF16), wout_ref[...], preferred_element_type=F32)


def _merge(x2, gain, w_all, layer, gate_bias, outs, w_branch, w_out):
    n = x2.shape[0]
    gate_w = N_BRANCH * D_MODEL
    widths = (A_W, B_W, C_W, 2 * HEAD_DIM)
    offs = np.concatenate([[0], np.cumsum(widths)])
    w_br = [w_branch[offs[i]:offs[i + 1]].astype(BF16) for i in range(N_BRANCH)]
    const = lambda i: (0, 0)
    row = lambda i: (i, 0)
    return pl.pallas_call(
        _merge_kernel,
        grid=(n // TOK_TILE,),
        in_specs=([pl.BlockSpec((TOK_TILE, D_MODEL), row),
                   pl.BlockSpec((1, D_MODEL), const),
                   pl.BlockSpec((None, D_MODEL, gate_w), lambda i: (layer, 0, 0),
                                pipeline_mode=pl.Buffered(1)),
                   pl.BlockSpec((1, gate_w), const)]
                  + [pl.BlockSpec((TOK_TILE, w), row) for w in widths]
                  + [pl.BlockSpec((w, D_MODEL), const) for w in widths]
                  + [pl.BlockSpec((D_MODEL, D_MODEL), const)]),
        out_specs=pl.BlockSpec((TOK_TILE, D_MODEL), row),
        out_shape=jax.ShapeDtypeStruct((n, D_MODEL), F32),
        compiler_params=_cparams(1),
        name="merge",
    )(x2, gain.reshape(1, D_MODEL), w_all, gate_bias.reshape(1, -1).astype(F32),
      *outs, *w_br, w_out.astype(BF16))


def kernel(x, rel_table, ffn1_norm, ffn1_w_in, ffn1_w_out, mix_norm, w_in, gate_bias, forget_bias,
           a_q_norm, a_k_norm, a_lambda, a_subln, c_q_norm, c_k_norm, d_q_norm, d_k_norm,
           w_branch, w_out, ffn2_norm, ffn2_w_in, ffn2_w_out):
    batch, s, _ = x.shape
    depth = w_in.shape[0]
    assert s % (D_T * D_PAIRS[-1][1]) == 0 and (batch * s) % TOK_TILE == 0
    bias_a, bias_d = _bias_tiles(rel_table)
    w_all = _prep_w_in(w_in)
    x2 = x.reshape(batch * s, D_MODEL)
    for l in range(depth):
        x2 = _ffn(x2, ffn1_norm[l], ffn1_w_in[l], ffn1_w_out[l])
        p_abc, p_d, cf = _proj(x2, mix_norm[l], w_all, l, a_q_norm[l], a_k_norm[l],
                               c_q_norm[l], c_k_norm[l], d_q_norm[l], d_k_norm[l])
        lam_init = 0.8 - 0.6 * math.exp(-0.3 * l)
        oa = _attn_a(p_abc, bias_a, a_lambda[l], a_subln[l], lam_init, batch, s)
        ob = _attn_b(p_abc, batch, s)
        oc = _attn_c(p_abc, cf, forget_bias[l], batch, s)
        od = _attn_d(p_d, bias_d, batch, s)
        x2 = _merge(x2, mix_norm[l], w_all, l, gate_bias[l], (oa, ob, oc, od),
                    w_branch[l], w_out[l])
        x2 = _ffn(x2, ffn2_norm[l], ffn2_w_in[l], ffn2_w_out[l])
    return x2.reshape(batch, s, D_MODEL)
```

```python
import functools
import math

import numpy as np
import jax
import jax.numpy as jnp
from jax import lax
from jax.experimental import pallas as pl
from jax.experimental.pallas import tpu as pltpu

F32 = jnp.float32
BF16 = jnp.bfloat16

D_MODEL = 1024
HEAD_DIM = 64
SCALE = HEAD_DIM ** -0.5
A_HEADS = 4
B_HEADS = 6
C_HEADS = 6
D_PAIRS = ((128, 1), (512, 4), (2048, 16))
D_GROUPS = len(D_PAIRS)
N_BRANCH = 4
FFN_HIDDEN = 2816
REL_BUCKETS = 32
REL_MAX_DIST = 128
RMS_EPS = 1e-6
NEG_INF = -1e30

LANES = 128
A_W = A_HEADS * 2 * HEAD_DIM
B_W = B_HEADS * HEAD_DIM
C_W = C_HEADS * HEAD_DIM
D_W = D_GROUPS * 2 * HEAD_DIM
ABC_W = 3 * (A_W + B_W + C_W)

W_SHIFT = (ABC_W + C_HEADS) % LANES
W_ALIGNED_BLOCKS = ABC_W // LANES + 1
W_ABC_OFF = N_BRANCH * D_MODEL
W_CF_OFF = W_ABC_OFF + ABC_W
W_D_OFF = W_ABC_OFF + W_ALIGNED_BLOCKS * LANES
W_D_END = W_D_OFF + 3 * D_W
GROUP_W = 256

TOK_TILE = 512
FFN_CHUNK = 256
ATT_T = 256
ATT_WIN = 2 * ATT_T
LOG2E = math.log2(math.e)
ATT_ROWS = 2
D_UNROLL = 8
D_T = 128
VMEM_LIMIT = 56 * 1024 * 1024


def _cparams(n_axes):
    return pltpu.CompilerParams(dimension_semantics=("arbitrary",) * n_axes,
                                vmem_limit_bytes=VMEM_LIMIT)


def _rms(x, g):
    ms = jnp.mean(x * x, axis=-1, keepdims=True)
    return x * lax.rsqrt(ms + RMS_EPS) * g


def _split3(a):
    hi = a.astype(BF16)
    r1 = a - hi.astype(F32)
    mid = r1.astype(BF16)
    lo = (r1 - mid.astype(F32)).astype(BF16)
    return hi, mid, lo


def _dot_exact_lhs(a_bf16, b):
    hi, mid, lo = _split3(b)
    d = functools.partial(jnp.dot, preferred_element_type=F32)
    return d(a_bf16, hi) + d(a_bf16, mid) + d(a_bf16, lo)


def _qk(q, k):
    return lax.dot_general(q, k, (((1,), (1,)), ((), ())), preferred_element_type=F32)


def _log_sigmoid(x):
    return jnp.minimum(x, 0.0) - jnp.log(1.0 + jnp.exp(-jnp.abs(x)))


def _lane_half(shape):
    return lax.broadcasted_iota(jnp.int32, shape, len(shape) - 1) < HEAD_DIM


def _ffn_kernel(x_ref, g_ref, wi_ref, wo_ref, o_ref):
    x = x_ref[...]
    h = _rms(x, g_ref[...]).astype(BF16)
    acc = jnp.zeros(x.shape, F32)
    for c in range(FFN_HIDDEN // FFN_CHUNK):
        lo = c * FFN_CHUNK
        gate = jnp.dot(h, wi_ref[:, lo:lo + FFN_CHUNK], preferred_element_type=F32)
        up = jnp.dot(h, wi_ref[:, FFN_HIDDEN + lo:FFN_HIDDEN + lo + FFN_CHUNK],
                     preferred_element_type=F32)
        a = (gate * jax.nn.sigmoid(gate) * up).astype(BF16)
        acc = acc + jnp.dot(a, wo_ref[c * FFN_CHUNK:(c + 1) * FFN_CHUNK, :],
                            preferred_element_type=F32)
    o_ref[...] = x + 0.5 * acc


def _ffn(x2, gain, w_i, w_o):
    n = x2.shape[0]
    const = lambda i: (0, 0)
    return pl.pallas_call(
        _ffn_kernel,
        grid=(n // TOK_TILE,),
        in_specs=[
            pl.BlockSpec((TOK_TILE, D_MODEL), lambda i: (i, 0)),
            pl.BlockSpec((1, D_MODEL), const),
            pl.BlockSpec((D_MODEL, 2 * FFN_HIDDEN), const, pipeline_mode=pl.Buffered(1)),
            pl.BlockSpec((FFN_HIDDEN, D_MODEL), const, pipeline_mode=pl.Buffered(1)),
        ],
        out_specs=pl.BlockSpec((TOK_TILE, D_MODEL), lambda i: (i, 0)),
        out_shape=jax.ShapeDtypeStruct((n, D_MODEL), F32),
        compiler_params=_cparams(1),
        name="ffn",
    )(x2, gain.reshape(1, D_MODEL), w_i.astype(BF16), w_o.astype(BF16))


_ABC_SECTIONS = ((A_W, True), (A_W, True), (A_W, False),
                 (B_W, False), (B_W, False), (B_W, False),
                 (C_W, True), (C_W, True), (C_W, False))
_D_SECTIONS = ((D_W, True), (D_W, True), (D_W, False))


def _proj_sections(y, gain_ref, gmat, out_ref, sections):
    off = 0
    for width, normed in sections:
        lo = off
        while lo < off + width:
            blk = min(GROUP_W, off + width - lo)
            yb = y[:, lo:lo + blk]
            if normed:
                ms = jnp.dot((yb * yb).astype(BF16), gmat[:blk, :blk], preferred_element_type=F32)
                yb = yb * lax.rsqrt(ms + RMS_EPS)
            out_ref[:, lo:lo + blk] = (yb * gain_ref[:, lo:lo + blk]).astype(out_ref.dtype)
            lo += blk
        off += width


def _proj_kernel(x_ref, g_ref, w_ref, gabc_ref, gd_ref, gmat_ref, pabc_ref, pd_ref, cf_ref):
    h = _rms(x_ref[...], g_ref[...]).astype(BF16)
    gmat = gmat_ref[...]
    y_abc = jnp.dot(h, w_ref[:, W_ABC_OFF:W_CF_OFF], preferred_element_type=F32)
    _proj_sections(y_abc, gabc_ref, gmat, pabc_ref, _ABC_SECTIONS)
    y_d = jnp.dot(h, w_ref[:, W_CF_OFF:W_D_END], preferred_element_type=F32)
    cf_ref[...] = y_d[:, :LANES]
    _proj_sections(y_d[:, LANES:], gd_ref, gmat, pd_ref, _D_SECTIONS)


def _head_gain(gain, width, scale):
    return jnp.tile(gain.astype(F32), width // HEAD_DIM) * scale


def _win_kernel(a_ref, b_ref, o_ref):
    c = pl.program_id(1)

    @pl.when(c < W_ALIGNED_BLOCKS)
    def _():
        o_ref[...] = b_ref[...].astype(BF16)

    @pl.when(c >= W_ALIGNED_BLOCKS)
    def _():
        both = jnp.concatenate([a_ref[:, W_SHIFT:], b_ref[:, :W_SHIFT]], axis=1)
        o_ref[...] = both.astype(BF16)


def _prep_w_in(w_in):
    depth = w_in.shape[0]
    n_blocks = pl.cdiv(w_in.shape[2], LANES)
    gate_block0 = (W_D_END - W_ABC_OFF) // LANES
    return pl.pallas_call(
        _win_kernel,
        grid=(depth, n_blocks),
        in_specs=[pl.BlockSpec((None, D_MODEL, LANES), lambda l, c: (l, 0, jnp.maximum(c - 1, 0))),
                  pl.BlockSpec((None, D_MODEL, LANES), lambda l, c: (l, 0, c))],
        out_specs=pl.BlockSpec(
            (None, D_MODEL, LANES),
            lambda l, c: (l, 0, jnp.where(c < gate_block0, c + W_ABC_OFF // LANES, c - gate_block0))),
        out_shape=jax.ShapeDtypeStruct((depth, D_MODEL, W_D_END), BF16),
        compiler_params=_cparams(2),
        name="prep_w_in",
    )(w_in, w_in)


def _proj(x2, gain, w_all, layer, a_q, a_k, c_q, c_k, d_q, d_k):
    n = x2.shape[0]
    ones = lambda w, s: jnp.full((w,), s, F32)
    g_abc = jnp.concatenate([
        _head_gain(a_q, A_W, SCALE * LOG2E), _head_gain(a_k, A_W, 1.0), ones(A_W, 1.0),
        ones(B_W, SCALE * LOG2E), ones(B_W, 1.0), ones(B_W, 1.0),
        _head_gain(c_q, C_W, SCALE * LOG2E), _head_gain(c_k, C_W, 1.0),
        ones(C_W, 1.0)]).reshape(1, ABC_W)
    g_d = jnp.concatenate([_head_gain(d_q, D_W, SCALE), _head_gain(d_k, D_W, 1.0),
                           ones(D_W, 1.0)]).reshape(1, 3 * D_W)
    lane = np.arange(GROUP_W)
    gmat = jnp.asarray((lane[:, None] // HEAD_DIM == lane[None, :] // HEAD_DIM) / HEAD_DIM, BF16)
    const = lambda i: (0, 0)
    row = lambda i: (i, 0)
    return pl.pallas_call(
        _proj_kernel,
        grid=(n // TOK_TILE,),
        in_specs=[
            pl.BlockSpec((TOK_TILE, D_MODEL), row),
            pl.BlockSpec((1, D_MODEL), const),
            pl.BlockSpec((None, D_MODEL, W_D_END), lambda i: (layer, 0, 0),
                         pipeline_mode=pl.Buffered(1)),
            pl.BlockSpec((1, ABC_W), const),
            pl.BlockSpec((1, 3 * D_W), const),
            pl.BlockSpec((GROUP_W, GROUP_W), const),
        ],
        out_specs=[
            pl.BlockSpec((TOK_TILE, ABC_W), row),
            pl.BlockSpec((TOK_TILE, 3 * D_W), row),
            pl.BlockSpec((TOK_TILE, LANES), row),
        ],
        out_shape=[
            jax.ShapeDtypeStruct((n, ABC_W), BF16),
            jax.ShapeDtypeStruct((n, 3 * D_W), F32),
            jax.ShapeDtypeStruct((n, LANES), F32),
        ],
        compiler_params=_cparams(1),
        name="proj",
    )(x2, gain.reshape(1, D_MODEL), w_all, g_abc, g_d, gmat)


_MASKED_BUCKET = REL_BUCKETS


def _rel_bucket_np(n):
    n = np.asarray(n, np.int64)
    max_exact = REL_BUCKETS // 2
    nf = np.maximum(n, 1).astype(np.float32)
    large = max_exact + (np.log(nf / np.float32(max_exact)) / np.float32(math.log(REL_MAX_DIST / max_exact))
                         * np.float32(REL_BUCKETS - max_exact)).astype(np.int32)
    large = np.minimum(large, REL_BUCKETS - 1)
    return np.where(n < max_exact, n, large).astype(np.int32)


def _bucket_maps():
    t = ATT_T
    x, y = np.arange(t)[:, None], np.arange(t)[None, :]
    diag = np.where(x >= y, _rel_bucket_np(np.maximum(x - y, 0)), _MASKED_BUCKET)
    prev = _rel_bucket_np(t + x - y)
    idx_a = np.stack([prev.T, diag.T]).astype(np.int32)
    x, y = np.arange(D_T)[:, None], np.arange(D_T)[None, :]
    idx_d = np.empty((D_GROUPS, 2, D_T, 2 * D_T), np.int32)
    for g, (_, dil) in enumerate(D_PAIRS):
        prev = np.where(y >= x, _rel_bucket_np(dil * (D_T + x - y)), _MASKED_BUCKET)
        diag = np.where(y <= x, _rel_bucket_np(dil * np.maximum(x - y, 0)), _MASKED_BUCKET)
        idx_d[g, 0] = np.concatenate([prev, diag], axis=1)
        idx_d[g, 1] = np.concatenate([np.full_like(prev, _MASKED_BUCKET), diag], axis=1)
    return idx_a, idx_d


def _bias_kernel(tbl_ref, idxa_ref, idxd_ref, ba_ref, bd_ref):
    for h in range(A_HEADS):
        for kind in range(2):
            ba_ref[h, kind] = jnp.where(idxa_ref[kind] == _MASKED_BUCKET, NEG_INF, 0.0)
        ba_ref[h, 2] = jnp.zeros((ATT_T, ATT_T), F32)
        ba_ref[h, 3] = jnp.full((ATT_T, ATT_T), NEG_INF, F32)
    for g in range(D_GROUPS):
        for h in range(2):
            for var in range(2):
                bd_ref[g, h, var] = jnp.where(idxd_ref[g, var] == _MASKED_BUCKET, NEG_INF, 0.0)

    def body(b, carry):
        for h in range(A_HEADS):
            val = (tbl_ref[b, h] - tbl_ref[REL_BUCKETS - 1, h]) * LOG2E
            for kind in range(2):
                ba_ref[h, kind] = jnp.where(idxa_ref[kind] == b, val, ba_ref[h, kind])
        for g in range(D_GROUPS):
            for h in range(2):
                val = tbl_ref[b, A_HEADS + 2 * g + h]
                for var in range(2):
                    bd_ref[g, h, var] = jnp.where(idxd_ref[g, var] == b, val, bd_ref[g, h, var])
        return carry

    lax.fori_loop(0, REL_BUCKETS, body, 0)


def _bias_tiles(rel_table):
    idx_a, idx_d = _bucket_maps()
    return pl.pallas_call(
        _bias_kernel,
        in_specs=[pl.BlockSpec(memory_space=pltpu.SMEM),
                  pl.BlockSpec(memory_space=pltpu.VMEM),
                  pl.BlockSpec(memory_space=pltpu.VMEM)],
        out_specs=[pl.BlockSpec(memory_space=pltpu.VMEM),
                   pl.BlockSpec(memory_space=pltpu.VMEM)],
        out_shape=[jax.ShapeDtypeStruct((A_HEADS, 4, ATT_T, ATT_T), F32),
                   jax.ShapeDtypeStruct((D_GROUPS, 2, 2, D_T, 2 * D_T), F32)],
        compiler_params=pltpu.CompilerParams(vmem_limit_bytes=VMEM_LIMIT),
        name="rel_bias",
    )(rel_table.astype(F32), jnp.asarray(idx_a), jnp.asarray(idx_d))


def _rows(idx, size):
    return pl.ds(pl.multiple_of(idx * size, size), size)


def _split_heads(q):
    first = _lane_half((q.shape[0], LANES))
    zero = jnp.zeros((q.shape[0], LANES), q.dtype)
    out = []
    for p in range(q.shape[1] // LANES):
        qp = q[:, p * LANES:(p + 1) * LANES]
        out += [jnp.where(first, qp, zero), jnp.where(first, zero, qp)]
    return out


def _transpose_into(vt_ref, v_ref):
    for p in range(v_ref.shape[1] // LANES):
        for blk in range(v_ref.shape[0] // ATT_T):
            tile = v_ref[blk * ATT_T:(blk + 1) * ATT_T, p * LANES:(p + 1) * LANES]
            vt_ref[p * LANES:(p + 1) * LANES, blk * ATT_T:(blk + 1) * ATT_T] = (
                tile.astype(F32).T.astype(vt_ref.dtype))


def _attn_specs_rows(width, col0, s, rows):
    q_spec = pl.BlockSpec((rows, ATT_T, width), lambda b, i: (b, i, col0))
    k_spec = pl.BlockSpec((rows, s, width), lambda b, i: (b, 0, col0 + 1))
    v_spec = pl.BlockSpec((rows, s, width), lambda b, i: (b, 0, col0 + 2))
    o_spec = pl.BlockSpec((rows, ATT_T, width), lambda b, i: (b, i, 0))
    return q_spec, k_spec, v_spec, o_spec


def _window_split(qi):
    return lax.shift_right_logical(qi + 1, 1), jnp.bitwise_and(qi, 1)


def _softmax_stages_t(scores, ms, acc_ref, vts, ls=None):
    n = len(scores)
    m_new = [jnp.maximum(ms[c], jnp.max(scores[c], axis=0, keepdims=True)) for c in range(n)]
    alpha = [jnp.exp2(ms[c] - m_new[c]) for c in range(n)]
    p_bf16, l_new = [], []
    for c in range(n):
        if ls is None:
            p_bf16.append(jnp.exp2((scores[c] - m_new[c]).astype(BF16)))
        else:
            p = jnp.exp2(scores[c] - m_new[c])
            p_bf16.append(p.astype(BF16))
            l_new.append(alpha[c] * ls[c] + jnp.sum(p, axis=0, keepdims=True))
    pv = [jnp.dot(vts[c], p_bf16[c], preferred_element_type=F32) for c in range(n)]
    for c in range(n):
        acc_ref[c] = alpha[c] * acc_ref[c] + pv[c]
    return m_new, (l_new if ls is not None else None)


def _attn_a_kernel(q_ref, k_ref, v_ref, bias_ref, lam_ref, subln_ref, o_ref, acc_ref, vt_ref, *,
                   lam_init):
    qi = pl.program_id(1)
    n_rows = q_ref.shape[0]

    @pl.when(qi == 0)
    def _():
        for b in range(n_rows):
            _transpose_into(vt_ref.at[b], v_ref.at[b])

    qs = [q for b in range(n_rows) for q in _split_heads(q_ref[b])]
    n_chain = len(qs)
    acc_ref[...] = jnp.zeros(acc_ref.shape, F32)
    stat = lambda val: jnp.full((1, ATT_T), val, F32)
    init = (stat(NEG_INF), stat(0.0)) * n_chain
    n_full, odd = _window_split(qi)

    def window(rows, carry, kinds):
        scores, vts = [], []
        for b in range(n_rows):
            for h in range(A_HEADS):
                cols = slice(h * LANES, (h + 1) * LANES)
                k, vt = k_ref[b, rows, cols], vt_ref[b, cols, rows]
                if kinds is not None:
                    bias = jnp.concatenate([bias_ref[h, kind] for kind in kinds], axis=0)
                for half in range(2):
                    s = _qk(k, qs[(b * A_HEADS + h) * 2 + half])
                    scores.append(s if kinds is None else s + bias)
                    vts.append(vt)
        ms, ls = _softmax_stages_t(scores, carry[0::2], acc_ref, vts, ls=carry[1::2])
        return tuple(x for pair in zip(ms, ls) for x in pair)

    n_plain = jnp.maximum(n_full - 1, 0)
    carry = lax.fori_loop(0, n_plain, lambda j, c: window(_rows(j, ATT_WIN), c, None), init)
    near = (jnp.where(odd == 1, 0, 2), jnp.where(odd == 1, 1, 0))
    carry = lax.fori_loop(n_plain, n_full, lambda j, c: window(_rows(j, ATT_WIN), c, near), carry)
    carry = lax.fori_loop(0, 1 - odd, lambda _, c: window(_rows(qi, ATT_T), c, (1,)), carry)

    lv = lam_ref[...]
    lam = (jnp.exp(jnp.sum(lv[0:1] * lv[1:2], axis=-1, keepdims=True))
           - jnp.exp(jnp.sum(lv[2:3] * lv[3:4], axis=-1, keepdims=True)) + lam_init)
    for b in range(n_rows):
        for h in range(A_HEADS):
            c0 = (b * A_HEADS + h) * 2
            l0, l1 = carry[2 * c0 + 1], carry[2 * c0 + 3]
            o = acc_ref[c0] / l0 - lam * (acc_ref[c0 + 1] / l1)
            ms = jnp.mean(o * o, axis=0, keepdims=True)
            o = o * lax.rsqrt(ms + RMS_EPS) * subln_ref[...] * (1.0 - lam_init)
            o_ref[b, :, h * LANES:(h + 1) * LANES] = o.T.astype(o_ref.dtype)


def _attn_a(p_abc, bias_a, lam_vecs, subln, lam_init, batch, s):
    n = p_abc.shape[0]
    rows = ATT_ROWS if batch % ATT_ROWS == 0 else 1
    q_spec, k_spec, v_spec, o_spec = _attn_specs_rows(A_W, 0, s, rows)
    p3 = p_abc.reshape(batch, s, ABC_W)
    out = pl.pallas_call(
        functools.partial(_attn_a_kernel, lam_init=lam_init),
        grid=(batch // rows, s // ATT_T),
        in_specs=[q_spec, k_spec, v_spec,
                  pl.BlockSpec(bias_a.shape, lambda b, i: (0, 0, 0, 0)),
                  pl.BlockSpec((4, HEAD_DIM), lambda b, i: (0, 0)),
                  pl.BlockSpec((2 * HEAD_DIM, ATT_T), lambda b, i: (0, 0))],
        out_specs=o_spec,
        out_shape=jax.ShapeDtypeStruct((batch, s, A_W), BF16),
        scratch_shapes=[pltpu.VMEM((rows * 2 * A_HEADS, LANES, ATT_T), F32),
                        pltpu.VMEM((rows, A_W, s), BF16)],
        compiler_params=_cparams(2),
        name="attn_diff",
    )(p3, p3, p3, bias_a, lam_vecs.astype(F32),
      jnp.broadcast_to(subln.astype(F32)[:, None], (2 * HEAD_DIM, ATT_T)))
    return out.reshape(n, A_W)


def _attn_b_kernel(q_ref, k_ref, v_ref, o_ref, acc_ref, vt_ref):
    qi = pl.program_id(1)
    n_rows = q_ref.shape[0]

    @pl.when(qi == 0)
    def _():
        for b in range(n_rows):
            _transpose_into(vt_ref.at[b], v_ref.at[b])

    qs = [q for b in range(n_rows) for q in _split_heads(q_ref[b])]
    acc_ref[...] = jnp.zeros(acc_ref.shape, F32)
    key = lax.broadcasted_iota(jnp.int32, (ATT_T, ATT_T), 0)
    qry = lax.broadcasted_iota(jnp.int32, (ATT_T, ATT_T), 1)
    later = jnp.where(qry > key, 1.0, 0.0).astype(BF16)
    strict = key < qry

    def step(kj, rsum, diag):
        rows = _rows(kj, ATT_T)
        heads = range(n_rows * B_HEADS)
        row_of = [h // B_HEADS for h in heads]
        cols = [slice(((h % B_HEADS) // 2) * LANES, ((h % B_HEADS) // 2 + 1) * LANES) for h in heads]
        z = [_qk(k_ref[row_of[h], rows, cols[h]], qs[h]) for h in heads]
        ls = [jnp.minimum(z[h], 0.0) - jnp.log2(1.0 + jnp.exp2(-jnp.abs(z[h]))) for h in heads]
        u = [ls[h] - z[h] for h in heads]
        if diag:
            u = [jnp.where(strict, u[h], 0.0) for h in heads]
        a = [jnp.exp2(ls[h] + jnp.dot(later, u[h].astype(BF16), preferred_element_type=F32)
                      + rsum[h]) for h in heads]
        if diag:
            a = [jnp.where(strict, a[h], 0.0) for h in heads]
        pv = [jnp.dot(vt_ref[row_of[h], cols[h], rows], a[h].astype(BF16),
                      preferred_element_type=F32) for h in heads]
        for h in heads:
            acc_ref[h] = acc_ref[h] + pv[h]
        return tuple(rsum[h] + jnp.sum(u[h], axis=0, keepdims=True) for h in heads)

    zero = jnp.zeros((1, ATT_T), F32)
    rsum = step(qi, (zero,) * (n_rows * B_HEADS), True)
    lax.fori_loop(0, qi, lambda t, c: step(qi - 1 - t, c, False), rsum)
    for b in range(n_rows):
        for p in range(B_HEADS // 2):
            c0 = b * B_HEADS + 2 * p
            o_t = jnp.concatenate([acc_ref[c0][:HEAD_DIM], acc_ref[c0 + 1][HEAD_DIM:]], axis=0)
            o_ref[b, :, p * LANES:(p + 1) * LANES] = o_t.T.astype(o_ref.dtype)


def _attn_b(p_abc, batch, s):
    n = p_abc.shape[0]
    rows = ATT_ROWS if batch % ATT_ROWS == 0 else 1
    q_spec, k_spec, v_spec, o_spec = _attn_specs_rows(B_W, 3 * A_W // B_W, s, rows)
    p3 = p_abc.reshape(batch, s, ABC_W)
    out = pl.pallas_call(
        _attn_b_kernel,
        grid=(batch // rows, s // ATT_T),
        in_specs=[q_spec, k_spec, v_spec],
        out_specs=o_spec,
        out_shape=jax.ShapeDtypeStruct((batch, s, B_W), BF16),
        scratch_shapes=[pltpu.VMEM((rows * B_HEADS, LANES, ATT_T), F32),
                        pltpu.VMEM((rows, B_W, s), BF16)],
        compiler_params=_cparams(2),
        name="attn_stick",
    )(p3, p3, p3)
    return out.reshape(n, B_W)


def _attn_c_kernel(q_ref, k_ref, v_ref, cf_ref, fb_ref, o_ref, acc_ref, cumc_ref, cumr_ref, vaug_ref):
    qi = pl.program_id(1)
    n_rows, s_len = cf_ref.shape[0], cf_ref.shape[1]

    @pl.when(qi == 0)
    def _():
        r = lax.broadcasted_iota(jnp.int32, (LANES, LANES), 0)
        c = lax.broadcasted_iota(jnp.int32, (LANES, LANES), 1)
        lower = jnp.where(c <= r, 1.0, 0.0).astype(BF16)
        own = lax.broadcasted_iota(jnp.int32, (LANES, ATT_T), 0) < HEAD_DIM
        for b in range(n_rows):
            carry = jnp.zeros((1, LANES), F32)
            for blk in range(s_len // LANES):
                rows = slice(blk * LANES, (blk + 1) * LANES)
                log_f = _log_sigmoid(cf_ref[b, rows, :] + fb_ref[...])
                cum = _dot_exact_lhs(lower, log_f) + carry
                cumc_ref[b, rows, :] = cum * LOG2E
                cumr_ref[b, :, rows] = (cum * LOG2E).T
                carry = cum[LANES - 1:LANES, :]
            for p in range(C_HEADS // 2):
                for blk in range(s_len // ATT_T):
                    tile = v_ref[b, blk * ATT_T:(blk + 1) * ATT_T, p * LANES:(p + 1) * LANES]
                    tile_t = tile.astype(F32).T
                    dst = (slice(p * LANES, (p + 1) * LANES), slice(blk * ATT_T, (blk + 1) * ATT_T))
                    vaug_ref[(b, 0) + dst] = jnp.where(own, tile_t, 1.0).astype(BF16)
                    vaug_ref[(b, 1) + dst] = jnp.where(own, 1.0, tile_t).astype(BF16)

    chains = [(b, h) for b in range(n_rows) for h in range(C_HEADS)]
    qs = [q for b in range(n_rows) for q in _split_heads(q_ref[b])]
    acc_ref[...] = jnp.zeros(acc_ref.shape, F32)
    q_cols = _rows(qi, ATT_T)
    cum_q = [cumr_ref[b, h:h + 1, q_cols] for b, h in chains]
    n_full, odd = _window_split(qi)

    def causal(width):
        key = lax.broadcasted_iota(jnp.int32, (width, ATT_T), 0) - (width - ATT_T)
        return jnp.where(key <= lax.broadcasted_iota(jnp.int32, (width, ATT_T), 1), 0.0, NEG_INF)

    def window(rows, carry, mask):
        scores, vts = [], []
        for c, (b, h) in enumerate(chains):
            cols = slice((h // 2) * LANES, (h // 2 + 1) * LANES)
            s = _qk(k_ref[b, rows, cols], qs[c]) + cum_q[c] - cumc_ref[b, rows, h:h + 1]
            scores.append(s if mask is None else s + mask)
            vts.append(vaug_ref[b, h % 2, cols, rows])
        return tuple(_softmax_stages_t(scores, carry, acc_ref, vts)[0])

    init = (jnp.full((1, ATT_T), NEG_INF, F32),) * len(chains)
    carry = lax.fori_loop(0, n_full - odd, lambda j, c: window(_rows(j, ATT_WIN), c, None), init)
    carry = lax.fori_loop(n_full - odd, n_full,
                          lambda j, c: window(_rows(j, ATT_WIN), c, causal(ATT_WIN)), carry)
    lax.fori_loop(0, 1 - odd, lambda _, c: window(_rows(qi, ATT_T), c, causal(ATT_T)), carry)
    for b in range(n_rows):
        for p in range(C_HEADS // 2):
            c0 = b * C_HEADS + 2 * p
            a0, a1 = acc_ref[c0], acc_ref[c0 + 1]
            num = jnp.concatenate([a0[:HEAD_DIM], a1[HEAD_DIM:]], axis=0)
            den = jnp.concatenate([a0[HEAD_DIM:], a1[:HEAD_DIM]], axis=0)
            o_ref[b, :, p * LANES:(p + 1) * LANES] = (num / den).T.astype(o_ref.dtype)


def _attn_c(p_abc, cf, forget_bias, batch, s):
    n = p_abc.shape[0]
    rows = ATT_ROWS if batch % ATT_ROWS == 0 else 1
    q_spec, k_spec, v_spec, o_spec = _attn_specs_rows(C_W, 3 * (A_W + B_W) // C_W, s, rows)
    fb = jnp.pad(forget_bias.astype(F32), (0, LANES - C_HEADS)).reshape(1, LANES)
    p3 = p_abc.reshape(batch, s, ABC_W)
    out = pl.pallas_call(
        _attn_c_kernel,
        grid=(batch // rows, s // ATT_T),
        in_specs=[q_spec, k_spec, v_spec,
                  pl.BlockSpec((rows, s, LANES), lambda b, i: (b, 0, 0)),
                  pl.BlockSpec((1, LANES), lambda b, i: (0, 0))],
        out_specs=o_spec,
        out_shape=jax.ShapeDtypeStruct((batch, s, C_W), BF16),
        scratch_shapes=[pltpu.VMEM((rows * C_HEADS, LANES, ATT_T), F32),
                        pltpu.VMEM((rows, s, LANES), F32),
                        pltpu.VMEM((rows, LANES, s), F32),
                        pltpu.VMEM((rows, 2, C_W, s), BF16)],
        compiler_params=_cparams(2),
        name="attn_forget",
    )(p3, p3, p3, cf.reshape(batch, s, LANES), fb)
    return out.reshape(n, C_W)


def _attn_d_kernel(*refs):
    qkv_refs, (bias_ref, o_ref, qc_ref, kc_ref, vc_ref, oc_ref, lc_ref, og_ref, lse_ref) = (
        refs[:3 * D_GROUPS], refs[3 * D_GROUPS:])
    s_len = o_ref.shape[0]
    n_tiles = s_len // D_T
    first = _lane_half((D_T, LANES))
    kc_ref[0:D_T, :] = jnp.zeros((D_T, LANES), BF16)
    vc_ref[0:D_T, :] = jnp.zeros((D_T, LANES), BF16)

    for g, (_, dil) in enumerate(D_PAIRS):
        q_ref, k_ref, v_ref = qkv_refs[g], qkv_refs[D_GROUPS + g], qkv_refs[2 * D_GROUPS + g]
        class_len = s_len // dil
        tiles_per_class = class_len // D_T

        for r in range(dil):
            src = pl.ds(r, class_len, stride=dil) if dil > 1 else slice(None)
            dst = slice(r * class_len, (r + 1) * class_len)
            dst_kv = slice(D_T + r * class_len, D_T + (r + 1) * class_len)
            qc_ref[dst, :] = q_ref[src, :].astype(BF16)
            kc_ref[dst_kv, :] = k_ref[src, :].astype(BF16)
            vc_ref[dst_kv, :] = v_ref[src, :].astype(BF16)

        def tile_group(i, carry):
            tiles = [i * D_UNROLL + u for u in range(D_UNROLL)]
            rows = [pl.ds(pl.multiple_of(t * D_T, D_T), D_T) for t in tiles]
            if tiles_per_class > 1:
                keys = [pl.ds(pl.multiple_of(t * D_T, D_T), 2 * D_T) for t in tiles]
                variant = [jnp.where(t % tiles_per_class == 0, 1, 0) for t in tiles]
                bias = [[bias_ref[g, h, variant[u]] for h in range(2)] for u in range(D_UNROLL)]
            else:
                keys = [pl.ds(pl.multiple_of(t * D_T + D_T, D_T), D_T) for t in tiles]
                bias = [[bias_ref[g, h, 1][:, D_T:] for h in range(2)]] * D_UNROLL
            chains = [(u, h) for u in range(D_UNROLL) for h in range(2)]
            qh = [_split_heads(qc_ref[rows[u], :]) for u in range(D_UNROLL)]
            sc = [_qk(qh[u][h], kc_ref[keys[u], :]) + bias[u][h] for u, h in chains]
            m = [jnp.max(s, axis=-1, keepdims=True) for s in sc]
            e = [jnp.exp(s - mm) for s, mm in zip(sc, m)]
            l = [jnp.sum(x, axis=-1, keepdims=True) for x in e]
            o = [jnp.dot(x.astype(BF16), vc_ref[keys[u], :], preferred_element_type=F32)
                 for x, (u, h) in zip(e, chains)]
            o = [x / ll for x, ll in zip(o, l)]
            lse = [jnp.broadcast_to(mm + jnp.log(ll), (D_T, LANES)) for mm, ll in zip(m, l)]
            for u in range(D_UNROLL):
                oc_ref[rows[u], :] = jnp.where(first, o[2 * u], o[2 * u + 1])
                lc_ref[rows[u], :] = jnp.where(first, lse[2 * u], lse[2 * u + 1])
            return carry

        lax.fori_loop(0, n_tiles // D_UNROLL, tile_group, 0)

        for r in range(dil):
            dst = pl.ds(r, class_len, stride=dil) if dil > 1 else slice(None)
            src = slice(r * class_len, (r + 1) * class_len)
            og_ref[g, dst, :] = oc_ref[src, :]
            lse_ref[g, dst, :] = lc_ref[src, :]

    lse = [lse_ref[g] for g in range(D_GROUPS)]
    top = functools.reduce(jnp.maximum, lse)
    w = [jnp.exp(x - top) for x in lse]
    num = sum(w[g] * og_ref[g] for g in range(D_GROUPS))
    o_ref[...] = (num / sum(w)).astype(o_ref.dtype)


def _attn_d(p_d, bias_d, batch, s):
    n = p_d.shape[0]
    return pl.pallas_call(
        _attn_d_kernel,
        grid=(batch,),
        in_specs=([pl.BlockSpec((s, LANES), functools.partial(lambda b, c: (b, c), c=c))
                   for c in range(3 * D_GROUPS)]
                  + [pl.BlockSpec(bias_d.shape, lambda b: (0, 0, 0, 0, 0))]),
        out_specs=pl.BlockSpec((s, LANES), lambda b: (b, 0)),
        out_shape=jax.ShapeDtypeStruct((n, LANES), BF16),
        scratch_shapes=[pltpu.VMEM((s, LANES), BF16), pltpu.VMEM((s + D_T, LANES), BF16),
                        pltpu.VMEM((s + D_T, LANES), BF16),
                        pltpu.VMEM((s, LANES), F32), pltpu.VMEM((s, LANES), F32),
                        pltpu.VMEM((D_GROUPS, s, LANES), F32),
                        pltpu.VMEM((D_GROUPS, s, LANES), F32)],
        compiler_params=_cparams(1),
        name="attn_dilated",
    )(*([p_d] * (3 * D_GROUPS)), bias_d)


def _merge_kernel(x_ref, g_ref, wg_ref, gb_ref, oa_ref, ob_ref, oc_ref, od_ref,
                  wa_ref, wb_ref, wc_ref, wd_ref, wout_ref, o_ref):
    x = x_ref[...]
    h = _rms(x, g_ref[...]).astype(BF16)
    merged = jnp.zeros(x.shape, F32)
    branches = ((oa_ref, wa_ref), (ob_ref, wb_ref), (oc_ref, wc_ref), (od_ref, wd_ref))
    for i, (br_ref, w_ref) in enumerate(branches):
        cols = slice(i * D_MODEL, (i + 1) * D_MODEL)
        logits = jnp.dot(h, wg_ref[:, cols], preferred_element_type=F32) + gb_ref[:, cols]
        branch = jnp.dot(br_ref[...], w_ref[...], preferred_element_type=F32)
        merged = merged + jax.nn.sigmoid(logits) * branch
    o_ref[...] = x + jnp.dot(merged.astype(BF16), wout_ref[...], preferred_element_type=F32)


def _merge(x2, gain, w_all, layer, gate_bias, outs, w_branch, w_out):
    n = x2.shape[0]
    gate_w = N_BRANCH * D_MODEL
    widths = (A_W, B_W, C_W, 2 * HEAD_DIM)
    offs = np.concatenate([[0], np.cumsum(widths)])
    w_br = [w_branch[offs[i]:offs[i + 1]].astype(BF16) for i in range(N_BRANCH)]
    const = lambda i: (0, 0)
    row = lambda i: (i, 0)
    return pl.pallas_call(
        _merge_kernel,
        grid=(n // TOK_TILE,),
        in_specs=([pl.BlockSpec((TOK_TILE, D_MODEL), row),
                   pl.BlockSpec((1, D_MODEL), const),
                   pl.BlockSpec((None, D_MODEL, gate_w), lambda i: (layer, 0, 0),
                                pipeline_mode=pl.Buffered(1)),
                   pl.BlockSpec((1, gate_w), const)]
                  + [pl.BlockSpec((TOK_TILE, w), row) for w in widths]
                  + [pl.BlockSpec((w, D_MODEL), const) for w in widths]
                  + [pl.BlockSpec((D_MODEL, D_MODEL), const)]),
        out_specs=pl.BlockSpec((TOK_TILE, D_MODEL), row),
        out_shape=jax.ShapeDtypeStruct((n, D_MODEL), F32),
        compiler_params=_cparams(1),
        name="merge",
    )(x2, gain.reshape(1, D_MODEL), w_all, gate_bias.reshape(1, -1).astype(F32),
      *outs, *w_br, w_out.astype(BF16))


def kernel(x, rel_table, ffn1_norm, ffn1_w_in, ffn1_w_out, mix_norm, w_in, gate_bias, forget_bias,
           a_q_norm, a_k_norm, a_lambda, a_subln, c_q_norm, c_k_norm, d_q_norm, d_k_norm,
           w_branch, w_out, ffn2_norm, ffn2_w_in, ffn2_w_out):
    batch, s, _ = x.shape
    depth = w_in.shape[0]
    assert s % (D_T * D_PAIRS[-1][1]) == 0 and (batch * s) % TOK_TILE == 0
    bias_a, bias_d = _bias_tiles(rel_table)
    w_all = _prep_w_in(w_in)
    x2 = x.reshape(batch * s, D_MODEL)
    for l in range(depth):
        x2 = _ffn(x2, ffn1_norm[l], ffn1_w_in[l], ffn1_w_out[l])
        p_abc, p_d, cf = _proj(x2, mix_norm[l], w_all, l, a_q_norm[l], a_k_norm[l],
                               c_q_norm[l], c_k_norm[l], d_q_norm[l], d_k_norm[l])
        lam_init = 0.8 - 0.6 * math.exp(-0.3 * l)
        oa = _attn_a(p_abc, bias_a, a_lambda[l], a_subln[l], lam_init, batch, s)
        ob = _attn_b(p_abc, batch, s)
        oc = _attn_c(p_abc, cf, forget_bias[l], batch, s)
        od = _attn_d(p_d, bias_d, batch, s)
        x2 = _merge(x2, mix_norm[l], w_all, l, gate_bias[l], (oa, ob, oc, od),
                    w_branch[l], w_out[l])
        x2 = _ffn(x2, ffn2_norm[l], ffn2_w_in[l], ffn2_w_out[l])
    return x2.reshape(batch, s, D_MODEL)
```
